```python
import math
import jax
import jax.numpy as jnp
from jax import lax
import numpy as np

D_MODEL = 1024
BATCH = 8
SEQ = 8192
DEPTH = 4

GRID_W = 64
CTX_LEN = 256

HEAD_DIM = 64
ATTN_SCALE = HEAD_DIM ** -0.5
ROPE_FREQS = HEAD_DIM // 4
ROPE_THETA = 10000.0
Q_BLOCK = 128
A_HEADS = 8
A_KV_HEADS = 2
B_HEADS = 8
B_KV_HEADS = 2
WINDOW = 128
C_HEADS = 4
C_VDIM = 2 * HEAD_DIM
A_QW = A_HEADS * HEAD_DIM
A_KVW = A_KV_HEADS * HEAD_DIM
B_QW = B_HEADS * HEAD_DIM
B_KVW = B_KV_HEADS * HEAD_DIM
C_QKW = 2 * C_HEADS * HEAD_DIM
C_VW = C_HEADS * C_VDIM
QKV_SIZES = (A_QW, A_KVW, A_KVW, B_QW, B_KVW, B_KVW, C_QKW, C_QKW, C_VW)
QKV_HEADS = (A_HEADS, A_KV_HEADS, A_KV_HEADS, B_HEADS, B_KV_HEADS, B_KV_HEADS, 2 * C_HEADS, 2 * C_HEADS, C_HEADS)
QKV_WIDTH = A_QW + 2 * A_KVW + B_QW + 2 * B_KVW + 2 * C_QKW + C_VW
N_BRANCHES = 3

N_EXPERTS = 128
TOP_K = 8
N_GROUPS = 8
TOPK_GROUPS = 4
D_EXPERT = 256
D_SHARED = 256
ROUTE_SCALE = 2.5
EXPERT_BLOCK = 128

N_MOD = 6
EPS = 1e-6

kernel_name = 'hybrid_gated_attn_moe_dit'


def _rmsnorm(x, g):
    xf = x.astype(jnp.float32)
    y = xf * lax.rsqrt(jnp.mean(xf * xf, axis=-1, keepdims=True) + EPS)
    return y.astype(x.dtype) * g


def _swiglu(x, w1, w3, w2):
    return (jax.nn.silu(x @ w1) * (x @ w3)) @ w2


def _rope_tables(n_lat):
    rows = n_lat // GRID_W
    t = jnp.arange(rows * GRID_W, dtype=jnp.int32)
    row_pos = (t // GRID_W).astype(jnp.float32)
    col_pos = (t % GRID_W).astype(jnp.float32)
    inv_freq = jnp.power(ROPE_THETA, -jnp.arange(ROPE_FREQS, dtype=jnp.float32) / ROPE_FREQS)
    ang = jnp.stack([row_pos[:, None] * inv_freq, col_pos[:, None] * inv_freq], axis=1)
    return jnp.cos(ang), jnp.sin(ang)


def _rope(x, cos, sin):
    xf = x.astype(jnp.float32).reshape(x.shape[:-1] + (2, 2, ROPE_FREQS))
    x1, x2 = xf[..., 0, :], xf[..., 1, :]
    out = jnp.stack([x1 * cos - x2 * sin, x1 * sin + x2 * cos], axis=-2)
    return out.reshape(x.shape).astype(x.dtype)


def _heads(t, n):
    b, s, _ = t.shape
    return t.reshape(b, s, n, -1).transpose(0, 2, 1, 3)


def _group(q, n_kv):
    b, h, s, d = q.shape
    return q.reshape(b, n_kv, h // n_kv, s, d)


def _merge_heads(y):
    b, hk, g, s, dv = y.shape
    return y.transpose(0, 3, 1, 2, 4).reshape(b, s, hk * g * dv)


def _qkv_heads(a, w_qkv):
    p = a @ w_qkv
    split_at = np.cumsum(QKV_SIZES)[:-1].tolist()
    pieces = jnp.split(p, split_at, axis=-1)
    return [_heads(t, n) for t, n in zip(pieces, QKV_HEADS)]


def _sweep_blocks(q, fn):
    b, hk, g, s, d = q.shape
    nb = s // Q_BLOCK
    qb = q.reshape(b, hk, g, nb, Q_BLOCK, d).transpose(3, 0, 1, 2, 4, 5)
    out = lax.map(lambda a: fn(a[0], a[1]), (qb, jnp.arange(nb, dtype=jnp.int32)))
    go, dv = out.shape[3], out.shape[-1]
    return out.transpose(1, 2, 3, 0, 4, 5).reshape(b, hk, go, s, dv)


def _gqa_block(qi, k, v):
    s = jnp.einsum('bhgqd,bhkd->bhgqk', qi, k, preferred_element_type=jnp.float32) * ATTN_SCALE
    p = jax.nn.softmax(s, axis=-1).astype(v.dtype)
    return jnp.einsum('bhgqk,bhkd->bhgqd', p, v)


def _sink_probs(s, sink):
    m = jnp.maximum(jnp.max(s, axis=-1, keepdims=True), sink)
    p = jnp.exp(s - m)
    return p / (jnp.sum(p, axis=-1, keepdims=True) + jnp.exp(sink - m))


def _sink_block(qi, k, v, sink):
    s = jnp.einsum('bhgqd,bhkd->bhgqk', qi, k, preferred_element_type=jnp.float32) * ATTN_SCALE
    p = _sink_probs(s, sink).astype(v.dtype)
    return jnp.einsum('bhgqk,bhkd->bhgqd', p, v)


def _window_sink_attention(q, k, v, k_ctx, v_ctx, sink):
    s_len = q.shape[3]
    t_len = k_ctx.shape[2]
    span = Q_BLOCK + 2 * WINDOW
    k_pad = jnp.pad(k, ((0, 0), (0, 0), (WINDOW, WINDOW), (0, 0)))
    v_pad = jnp.pad(v, ((0, 0), (0, 0), (WINDOW, WINDOW), (0, 0)))
    offs = jnp.arange(span, dtype=jnp.int32)
    band = jnp.abs(offs[None, :] - WINDOW - jnp.arange(Q_BLOCK, dtype=jnp.int32)[:, None]) <= WINDOW

    def blk(qi, i):
        start = i * Q_BLOCK
        kw = lax.dynamic_slice_in_dim(k_pad, start, span, axis=2)
        vw = lax.dynamic_slice_in_dim(v_pad, start, span, axis=2)
        kpos = start - WINDOW + offs
        mask = band & ((kpos >= 0) & (kpos < s_len))[None, :]
        s_win = jnp.einsum('bhgqd,bhkd->bhgqk', qi, kw, preferred_element_type=jnp.float32) * ATTN_SCALE
        s_win = jnp.where(mask, s_win, -jnp.inf)
        s_ctx = jnp.einsum('bhgqd,bhkd->bhgqk', qi, k_ctx, preferred_element_type=jnp.float32) * ATTN_SCALE
        p = _sink_probs(jnp.concatenate([s_ctx, s_win], axis=-1), sink).astype(v.dtype)
        return (jnp.einsum('bhgqk,bhkd->bhgqd', p[..., :t_len], v_ctx)
                + jnp.einsum('bhgqk,bhkd->bhgqd', p[..., t_len:], vw))

    return _sweep_blocks(q, blk)


def _diff_block(qi, k, v, lam):
    s = jnp.einsum('bhiqd,bhikd->bhiqk', qi, k, preferred_element_type=jnp.float32) * ATTN_SCALE
    p = jax.nn.softmax(s, axis=-1)
    pd = (p[:, :, :1] - lam * p[:, :, 1:]).astype(v.dtype)
    return jnp.einsum('bhiqk,bhkv->bhiqv', pd, v)


def _merge_branches(a, ya, yb, yc, w_gate, b_gate, w_br_a, w_br_b, w_br_c, w_out):
    ga, gb, gc = jnp.split(jax.nn.sigmoid(a @ w_gate + b_gate), N_BRANCHES, axis=-1)
    return (ga * (ya @ w_br_a) + gb * (yb @ w_br_b) + gc * (yc @ w_br_c)) @ w_out


def _token_mixers(a_lat, a_ctx, cos, sin, w_qkv, g_qa, g_ka, sink, lam, lam_init, g_sub,
                  w_br_a, w_br_b, w_br_c, w_gate, b_gate, w_out, with_ctx_out):
    qa, ka, va, qb, kb, vb, qc, kc, vc = _qkv_heads(a_lat, w_qkv)
    qa_x, ka_x, va_x, qb_x, kb_x, vb_x, qc_x, kc_x, vc_x = _qkv_heads(a_ctx, w_qkv)
    bsz, _, s_len, _ = qa.shape
    t_len = ka_x.shape[2]

    qa = _rope(_rmsnorm(qa, g_qa), cos, sin)
    ka = _rope(_rmsnorm(ka, g_ka), cos, sin)
    qa_x = _rmsnorm(qa_x, g_qa)
    ka_x = _rmsnorm(ka_x, g_ka)
    ka_all = jnp.concatenate([ka_x, ka], axis=2)
    va_all = jnp.concatenate([va_x, va], axis=2)
    ya = _sweep_blocks(_group(qa, A_KV_HEADS), lambda qi, i: _gqa_block(qi, ka_all, va_all))

    sink_g = sink.astype(jnp.float32).reshape(1, B_KV_HEADS, B_HEADS // B_KV_HEADS, 1, 1)
    qb = _rope(qb, cos, sin)
    kb = _rope(kb, cos, sin)
    yb = _window_sink_attention(_group(qb, B_KV_HEADS), kb, vb, kb_x, vb_x, sink_g)

    qc = _rope(qc, cos, sin)
    kc = _rope(kc, cos, sin)
    kc_all = jnp.concatenate([kc_x, kc], axis=2).reshape(bsz, C_HEADS, 2, t_len + s_len, HEAD_DIM)
    vc_all = jnp.concatenate([vc_x, vc], axis=2)
    yc = _sweep_blocks(qc.reshape(bsz, C_HEADS, 2, s_len, HEAD_DIM),
                       lambda qi, i: _diff_block(qi, kc_all, vc_all, lam))
    yc = _rmsnorm(yc, g_sub) * (1.0 - lam_init)

    mix_lat = _merge_branches(a_lat, _merge_heads(ya), _merge_heads(yb), _merge_heads(yc),
                              w_gate, b_gate, w_br_a, w_br_b, w_br_c, w_out)
    if not with_ctx_out:
        return mix_lat, None

    ya_x = _sweep_blocks(_group(qa_x, A_KV_HEADS), lambda qi, i: _gqa_block(qi, ka_x, va_x))
    yb_x = _sweep_blocks(_group(qb_x, B_KV_HEADS), lambda qi, i: _sink_block(qi, kb_x, vb_x, sink_g))
    kc_xp = kc_x.reshape(bsz, C_HEADS, 2, t_len, HEAD_DIM)
    yc_x = _sweep_blocks(qc_x.reshape(bsz, C_HEADS, 2, t_len, HEAD_DIM),
                         lambda qi, i: _diff_block(qi, kc_xp, vc_x, lam))
    yc_x = _rmsnorm(yc_x, g_sub) * (1.0 - lam_init)
    mix_ctx = _merge_branches(a_ctx, _merge_heads(ya_x), _merge_heads(yb_x), _merge_heads(yc_x),
                              w_gate, b_gate, w_br_a, w_br_b, w_br_c, w_out)
    return mix_lat, mix_ctx


def _moe_tokens(h, w_router, e_bias, w1, w3, w2, ws1, ws3, ws2):
    n, d = h.shape
    scores = jax.nn.sigmoid(jnp.dot(h, w_router, preferred_element_type=jnp.float32))
    choice = scores + e_bias.astype(jnp.float32)
    grouped = choice.reshape(n, N_GROUPS, N_EXPERTS // N_GROUPS)
    group_score = jnp.sum(lax.top_k(grouped, 2)[0], axis=-1)
    _, top_groups = lax.top_k(group_score, TOPK_GROUPS)
    group_mask = jnp.any(top_groups[:, :, None] == jnp.arange(N_GROUPS, dtype=jnp.int32)[None, None, :], axis=1)
    expert_mask = jnp.repeat(group_mask, N_EXPERTS // N_GROUPS, axis=1)
    _, idx = lax.top_k(jnp.where(expert_mask, choice, -jnp.inf), TOP_K)
    w = jnp.take_along_axis(scores, idx, axis=1)
    w = w / jnp.sum(w, axis=-1, keepdims=True) * ROUTE_SCALE

    n_assign = n * TOP_K
    flat_e = idx.reshape(-1).astype(jnp.int32)
    order = jnp.argsort(flat_e).astype(jnp.int32)
    e_sorted = flat_e[order]
    tok_sorted = order // TOP_K
    w_sorted = w.reshape(-1)[order].astype(h.dtype)
    counts = jnp.zeros((N_EXPERTS,), jnp.int32).at[flat_e].add(1)
    starts = jnp.cumsum(counts) - counts
    padded = (counts + EXPERT_BLOCK - 1) // EXPERT_BLOCK * EXPERT_BLOCK
    padded_end = jnp.cumsum(padded)
    padded_start = padded_end - padded
    dest = padded_start[e_sorted] + jnp.arange(n_assign, dtype=jnp.int32) - starts[e_sorted]
    n_blocks = -(-n_assign // EXPERT_BLOCK) + N_EXPERTS
    n_rows = n_blocks * EXPERT_BLOCK
    row_tok = jnp.full((n_rows,), n, jnp.int32).at[dest].set(tok_sorted)
    block_start = jnp.arange(n_blocks, dtype=jnp.int32) * EXPERT_BLOCK
    block_expert = jnp.minimum(jnp.searchsorted(padded_end, block_start, side='right'), N_EXPERTS - 1)
    h_pad = jnp.concatenate([h, jnp.zeros((1, d), h.dtype)], axis=0)
    xb = h_pad[row_tok].reshape(n_blocks, EXPERT_BLOCK, d)

    def expert_block(args):
        xblk, e = args
        return _swiglu(xblk, w1[e], w3[e], w2[e])

    y_rows = lax.map(expert_block, (xb, block_expert)).reshape(n_rows, d)
    routed = jnp.zeros((n, d), h.dtype).at[tok_sorted].add(y_rows[dest] * w_sorted[:, None])
    return _swiglu(h, ws1, ws3, ws2) + routed


def _moe(h, w_router, e_bias, w1, w3, w2, ws1, ws3, ws2):
    return lax.map(lambda hs: _moe_tokens(hs, w_router, e_bias, w1, w3, w2, ws1, ws3, ws2), h)


def setup_inputs(seed: int = 0) -> dict:
    key = jax.random.key(seed)
    ks = jax.random.split(key, 32)
    f32 = jnp.float32

    def nrm(k, shape, scale):
        return jax.random.normal(k, shape, f32) * scale

    dm = D_MODEL
    return {
        'x': nrm(ks[0], (BATCH, SEQ, dm), 1.0),
        'c': nrm(ks[1], (BATCH, dm), 1.0),
        'ctx': nrm(ks[2], (BATCH, CTX_LEN, dm), 1.0),
        'c_ctx': nrm(ks[3], (dm,), 1.0),
        'w_mod': nrm(ks[4], (DEPTH, dm, N_MOD * dm), 0.3 * dm ** -0.5),
        'b_mod': nrm(ks[5], (DEPTH, N_MOD * dm), 0.02),
        'g_norm1': 1.0 + nrm(ks[6], (DEPTH, dm), 0.02),
        'w_qkv': nrm(ks[7], (DEPTH, dm, QKV_WIDTH), dm ** -0.5),
        'g_qnorm_a': 1.0 + nrm(ks[8], (DEPTH, HEAD_DIM), 0.02),
        'g_knorm_a': 1.0 + nrm(ks[9], (DEPTH, HEAD_DIM), 0.02),
        'sink_b': nrm(ks[10], (DEPTH, B_HEADS), 0.5),
        'lam_q1': nrm(ks[11], (DEPTH, HEAD_DIM), 0.1),
        'lam_k1': nrm(ks[12], (DEPTH, HEAD_DIM), 0.1),
        'lam_q2': nrm(ks[13], (DEPTH, HEAD_DIM), 0.1),
        'lam_k2': nrm(ks[14], (DEPTH, HEAD_DIM), 0.1),
        'g_subln_c': 1.0 + nrm(ks[15], (DEPTH, C_VDIM), 0.02),
        'w_br_a': nrm(ks[16], (DEPTH, A_QW, dm), A_QW ** -0.5),
        'w_br_b': nrm(ks[17], (DEPTH, B_QW, dm), B_QW ** -0.5),
        'w_br_c': nrm(ks[18], (DEPTH, C_VW, dm), C_VW ** -0.5),
        'w_gate': nrm(ks[19], (DEPTH, dm, N_BRANCHES * dm), dm ** -0.5),
        'b_gate': nrm(ks[20], (DEPTH, N_BRANCHES * dm), 0.02),
        'w_out': nrm(ks[21], (DEPTH, dm, dm), dm ** -0.5),
        'g_norm2': 1.0 + nrm(ks[22], (DEPTH, dm), 0.02),
        'w_router': nrm(ks[23], (DEPTH, dm, N_EXPERTS), dm ** -0.5),
        'e_bias': nrm(ks[24], (DEPTH, N_EXPERTS), 0.01),
        'w1': nrm(ks[25], (DEPTH, N_EXPERTS, dm, D_EXPERT), dm ** -0.5),
        'w3': nrm(ks[26], (DEPTH, N_EXPERTS, dm, D_EXPERT), dm ** -0.5),
        'w2': nrm(ks[27], (DEPTH, N_EXPERTS, D_EXPERT, dm), D_EXPERT ** -0.5),
        'ws1': nrm(ks[28], (DEPTH, dm, D_SHARED), dm ** -0.5),
        'ws3': nrm(ks[29], (DEPTH, dm, D_SHARED), dm ** -0.5),
        'ws2': nrm(ks[30], (DEPTH, D_SHARED, dm), D_SHARED ** -0.5),
        'g_final': 1.0 + nrm(ks[31], (dm,), 0.02),
    }


def reference(x, c, ctx, c_ctx, w_mod, b_mod, g_norm1, w_qkv, g_qnorm_a, g_knorm_a, sink_b,
              lam_q1, lam_k1, lam_q2, lam_k2, g_subln_c, w_br_a, w_br_b, w_br_c, w_gate, b_gate,
              w_out, g_norm2, w_router, e_bias, w1, w3, w2, ws1, ws3, ws2, g_final):
    n_lat = x.shape[1]
    cos, sin = _rope_tables(n_lat)
    for l in range(DEPTH):
        last = l == DEPTH - 1
        lam_init = 0.8 - 0.6 * math.exp(-0.3 * l)
        lam = (jnp.exp(jnp.dot(lam_q1[l], lam_k1[l])) - jnp.exp(jnp.dot(lam_q2[l], lam_k2[l]))).astype(jnp.float32) + lam_init
        sh1, sc1, gt1, sh2, sc2, gt2 = jnp.split((jax.nn.silu(c) @ w_mod[l] + b_mod[l])[:, None, :], N_MOD, axis=-1)
        csh1, csc1, cgt1, csh2, csc2, cgt2 = jnp.split(jax.nn.silu(c_ctx) @ w_mod[l] + b_mod[l], N_MOD, axis=-1)

        a_lat = _rmsnorm(x, g_norm1[l]) * (1.0 + sc1) + sh1
        a_ctx = _rmsnorm(ctx, g_norm1[l]) * (1.0 + csc1) + csh1
        mix_lat, mix_ctx = _token_mixers(a_lat, a_ctx, cos, sin, w_qkv[l], g_qnorm_a[l], g_knorm_a[l],
                                         sink_b[l], lam, lam_init, g_subln_c[l], w_br_a[l], w_br_b[l],
                                         w_br_c[l], w_gate[l], b_gate[l], w_out[l], not last)
        x = x + gt1 * mix_lat
        f_lat = _rmsnorm(x, g_norm2[l]) * (1.0 + sc2) + sh2
        moe_args = (w_router[l], e_bias[l], w1[l], w3[l], w2[l], ws1[l], ws3[l], ws2[l])
        if last:
            x = x + gt2 * _moe(f_lat, *moe_args)
        else:
            ctx = ctx + cgt1 * mix_ctx
            f_ctx = _rmsnorm(ctx, g_norm2[l]) * (1.0 + csc2) + csh2
            n_ctx = ctx.shape[1]
            y = _moe(jnp.concatenate([f_ctx, f_lat], axis=1), *moe_args)
            ctx = ctx + cgt2 * y[:, :n_ctx]
            x = x + gt2 * y[:, n_ctx:]
    return _rmsnorm(x, g_final)
```

```python
import functools
import math

import numpy as np
import jax
import jax.numpy as jnp
from jax import lax
from jax.experimental import pallas as pl
from jax.experimental.pallas import tpu as pltpu

F32 = jnp.float32
BF16 = jnp.bfloat16

HEAD_DIM = 64
ROPE_FREQS = HEAD_DIM // 4
ROPE_THETA = 10000.0
GRID_W = 64
WINDOW = 128
A_HEADS, A_KV = 8, 2
B_HEADS, B_KV = 8, 2
C_HEADS = 4
N_EXPERTS = 128
TOP_K = 8
N_GROUPS = 8
TOPK_GROUPS = 4
GROUP_SIZE = N_EXPERTS // N_GROUPS
ROUTE_SCALE = 2.5
EXPERT_BLOCK = 128
N_MOD = 6
EPS = 1e-6
NEG = -1e30

R_A = A_HEADS * HEAD_DIM + A_KV * HEAD_DIM
R_B = B_HEADS * HEAD_DIM + B_KV * HEAD_DIM
R_C = 4 * C_HEADS * HEAD_DIM
R_W = R_A + R_B + R_C
V_W = A_KV * HEAD_DIM + B_KV * HEAD_DIM + C_HEADS * 2 * HEAD_DIM

TOK_TILE = 256
KEY_CHUNK = 512
VMEM_LIMIT = 56 * 1024 * 1024


def _cparams(sem):
    return pltpu.CompilerParams(dimension_semantics=sem, vmem_limit_bytes=VMEM_LIMIT)


def _dot(a, b):
    return jnp.dot(a, b, preferred_element_type=F32)


def _sigmoid(x):
    return 1.0 / (1.0 + jnp.exp(-x))


def _silu(x):
    return x * _sigmoid(x)


def _rms_rows(x, g):
    return x * lax.rsqrt(jnp.mean(x * x, axis=-1, keepdims=True) + EPS) * g


def _mod_kernel(c_ref, w_ref, b_ref, o_ref):
    cs = _silu(c_ref[...])
    o_ref[0] = jnp.dot(cs, w_ref[0], preferred_element_type=F32,
                       precision=lax.Precision.HIGHEST) + b_ref[0]


def _modulation(cvec, w_mod, b_mod):
    depth, d, n = w_mod.shape
    rows = cvec.shape[0]
    bn = 1536
    return pl.pallas_call(
        _mod_kernel,
        grid=(depth, n // bn),
        in_specs=[
            pl.BlockSpec((rows, d), lambda l, j: (0, 0)),
            pl.BlockSpec((1, d, bn), lambda l, j: (l, 0, j)),
            pl.BlockSpec((1, 1, bn), lambda l, j: (l, 0, j)),
        ],
        out_specs=pl.BlockSpec((1, rows, bn), lambda l, j: (l, 0, j)),
        out_shape=jax.ShapeDtypeStruct((depth, rows, n), F32),
        compiler_params=_cparams(("parallel", "parallel")),
        name="modulation",
    )(cvec, w_mod, b_mod.reshape(depth, 1, n))


def _pre_kernel(x_ref, mod_ref, g1_ref, wqkv_ref, wsw_ref, ones_ref, grow_ref, gsrow_ref,
                cos_ref, sin_ref, r_out, v_out, *, d):
    x = x_ref[0]
    mod = mod_ref[0]
    a = (_rms_rows(x, g1_ref[...]) * (1.0 + mod[:, d:2 * d]) + mod[:, 0:d]).astype(BF16)
    p = _dot(a, wqkv_ref[...])
    ps = _dot(a, wsw_ref[...])
    pa = p[:, :R_A]
    sq = pa * pa
    hi = sq.astype(BF16)
    lo = (sq - hi.astype(F32)).astype(BF16)
    ssq = _dot(hi, ones_ref[...]) + _dot(lo, ones_ref[...])
    rinv = lax.rsqrt(ssq * (1.0 / HEAD_DIM) + EPS)
    cos = cos_ref[...]
    sin = sin_ref[...]
    n_tiles = R_W // 128
    for j in range(n_tiles):
        sl = slice(j * 128, (j + 1) * 128)
        o = p[:, sl] * (grow_ref[:, sl] * cos) + ps[:, sl] * (gsrow_ref[:, sl] * sin)
        if (j + 1) * 128 <= R_A:
            o = o * rinv[:, sl]
        r_out[0, :, sl] = o.astype(BF16)
    v_out[0] = p[:, R_W:].astype(BF16)


def _pre_attention(x, mod, g1, wqkv, wsw, ones_blk, grow, gsrow, cos_t, sin_t, n_ctx_tiles):
    b, n, d = x.shape
    tn = TOK_TILE
    nb = b

    def mod_idx(bi, t):
        return (jnp.where(t < n_ctx_tiles, nb, bi), 0, 0)

    return pl.pallas_call(
        functools.partial(_pre_kernel, d=d),
        grid=(b, n // tn),
        in_specs=[
            pl.BlockSpec((1, tn, d), lambda bi, t: (bi, t, 0)),
            pl.BlockSpec((1, 1, N_MOD * d), mod_idx),
            pl.BlockSpec((1, d), lambda bi, t: (0, 0)),
            pl.BlockSpec((d, R_W + V_W), lambda bi, t: (0, 0)),
            pl.BlockSpec((d, R_W), lambda bi, t: (0, 0)),
            pl.BlockSpec((R_A, R_A), lambda bi, t: (0, 0)),
            pl.BlockSpec((1, R_W), lambda bi, t: (0, 0)),
            pl.BlockSpec((1, R_W), lambda bi, t: (0, 0)),
            pl.BlockSpec((tn, 128), lambda bi, t: (t, 0)),
            pl.BlockSpec((tn, 128), lambda bi, t: (t, 0)),
        ],
        out_specs=[
            pl.BlockSpec((1, tn, R_W), lambda bi, t: (bi, t, 0)),
            pl.BlockSpec((1, tn, V_W), lambda bi, t: (bi, t, 0)),
        ],
        out_shape=[
            jax.ShapeDtypeStruct((b, n, R_W), BF16),
            jax.ShapeDtypeStruct((b, n, V_W), BF16),
        ],
        compiler_params=_cparams(("parallel", "parallel")),
        name="pre_attention",
    )(x, mod, g1, wqkv, wsw, ones_blk, grow, gsrow, cos_t, sin_t)


def _softmax_step(q, kt, v, m, l, acc, mask=None):
    s = _dot(q, kt)
    if mask is not None:
        s = jnp.where(mask, s, NEG)
    m_new = jnp.maximum(m, jnp.max(s, axis=-1, keepdims=True))
    p = jnp.exp(s - m_new)
    alpha = jnp.exp(m - m_new)
    l_new = alpha * l + jnp.sum(p, axis=-1, keepdims=True)
    acc_new = alpha * acc + _dot(p.astype(BF16), v)
    return m_new, l_new, acc_new


def _attn_kernel(sc_ref, gsub_ref, q_ref, kt_ref, v_ref, o_ref, *, heads, shared_k, mode,
                 n_ctx, n_tok, tq, ck, dv):
    u = pl.program_id(1)
    qi = pl.program_id(2)
    is_lat = qi >= n_ctx // tq
    n_lat = n_tok - n_ctx
    outs = []
    for g in range(heads):
        gk = 0 if shared_k else g
        q = q_ref[0, g]
        if mode == "window":
            m0 = jnp.full((tq, 1), sc_ref[u * heads + g], F32)
            l0 = jnp.ones((tq, 1), F32)
        else:
            m0 = jnp.full((tq, 1), NEG, F32)
            l0 = jnp.zeros((tq, 1), F32)
        acc0 = jnp.zeros((tq, dv), F32)
        carry = _softmax_step(q, kt_ref[0, gk, :, 0:n_ctx], v_ref[0, 0, 0:n_ctx, :], m0, l0, acc0)

        if mode == "window":
            span = tq + 2 * WINDOW

            def win(c, q=q, gk=gk):
                start = jnp.clip(qi * tq - WINDOW, 0, n_tok - span)
                start = pl.multiple_of(start, 128)
                qpos = qi * tq + lax.broadcasted_iota(jnp.int32, (tq, span), 0)
                kpos = start + lax.broadcasted_iota(jnp.int32, (tq, span), 1)
                mask = (kpos >= n_ctx) & (jnp.abs(kpos - qpos) <= WINDOW)
                return _softmax_step(q, kt_ref[0, gk, :, pl.ds(start, span)],
                                     v_ref[0, 0, pl.ds(start, span), :], *c, mask=mask)

            carry = lax.cond(is_lat, win, lambda c: c, carry)
        else:
            def body(ci, c, q=q, gk=gk):
                k0 = pl.multiple_of(n_ctx + ci * ck, 128)
                return _softmax_step(q, kt_ref[0, gk, :, pl.ds(k0, ck)],
                                     v_ref[0, 0, pl.ds(k0, ck), :], *c)

            carry = lax.fori_loop(0, jnp.where(is_lat, n_lat // ck, 0), body, carry)
        _, l, acc = carry
        outs.append(acc / l)
    if mode == "diff":
        lam = sc_ref[heads]
        post = sc_ref[heads + 1]
        y = outs[0] - lam * outs[1]
        y = _rms_rows(y, gsub_ref[...]) * post
        o_ref[0, 0] = y.astype(o_ref.dtype)
    else:
        for g in range(heads):
            o_ref[0, g] = outs[g].astype(o_ref.dtype)


def _attention(scalars, gsub, q, kt, v, *, heads, shared_k, mode, n_ctx):
    b, hq, n, hd = q.shape
    units = hq // heads
    dv = v.shape[-1]
    tq = TOK_TILE
    gk = 1 if shared_k else heads
    out_heads = 1 if mode == "diff" else heads
    kern = functools.partial(_attn_kernel, heads=heads, shared_k=shared_k, mode=mode, n_ctx=n_ctx,
                             n_tok=n, tq=tq, ck=KEY_CHUNK, dv=dv)
    return pl.pallas_call(
        kern,
        grid=(b, units, n // tq),
        in_specs=[
            pl.BlockSpec(memory_space=pltpu.SMEM),
            pl.BlockSpec((1, dv), lambda bi, u, t: (0, 0)),
            pl.BlockSpec((1, heads, tq, hd), lambda bi, u, t: (bi, u, t, 0)),
            pl.BlockSpec((1, gk, hd, n), lambda bi, u, t: (bi, u, 0, 0)),
            pl.BlockSpec((1, 1, n, dv), lambda bi, u, t: (bi, u, 0, 0)),
        ],
        out_specs=pl.BlockSpec((1, out_heads, tq, dv), lambda bi, u, t: (bi, u, t, 0)),
        out_shape=jax.ShapeDtypeStruct((b, units * out_heads, n, dv), BF16),
        compiler_params=_cparams(("parallel", "parallel", "arbitrary")),
        name="attn_" + mode,
    )(scalars, gsub, q, kt, v)


def _merge_kernel(x_ref, mod_ref, g1_ref, ya_ref, yb_ref, yc_ref, wg_ref, bg_ref, wa_ref, wb_ref,
                  wc_ref, wo_ref, o_ref, *, d):
    x = x_ref[0]
    mod = mod_ref[0]
    a = (_rms_rows(x, g1_ref[...]) * (1.0 + mod[:, d:2 * d]) + mod[:, 0:d]).astype(BF16)
    gate = _sigmoid(_dot(a, wg_ref[...]) + bg_ref[...])
    m = (gate[:, 0:d] * _dot(ya_ref[0], wa_ref[...])
         + gate[:, d:2 * d] * _dot(yb_ref[0], wb_ref[...])
         + gate[:, 2 * d:3 * d] * _dot(yc_ref[0], wc_ref[...]))
    mix = _dot(m.astype(BF16), wo_ref[...])
    o_ref[0] = x + mod[:, 2 * d:3 * d] * mix


def _merge(x, mod, g1, ya, yb, yc, wg, bg, wa, wb, wc, wo, n_ctx_tiles):
    b, n, d = x.shape
    tn = TOK_TILE
    nb = b
    yw = ya.shape[-1]

    def mod_idx(bi, t):
        return (jnp.where(t < n_ctx_tiles, nb, bi), 0, 0)

    tok = lambda bi, t: (bi, t, 0)
    const = lambda bi, t: (0, 0)
    return pl.pallas_call(
        functools.partial(_merge_kernel, d=d),
        grid=(b, n // tn),
        in_specs=[
            pl.BlockSpec((1, tn, d), tok),
            pl.BlockSpec((1, 1, N_MOD * d), mod_idx),
            pl.BlockSpec((1, d), const),
            pl.BlockSpec((1, tn, yw), tok),
            pl.BlockSpec((1, tn, yw), tok),
            pl.BlockSpec((1, tn, yw), tok),
            pl.BlockSpec((d, 3 * d), const),
            pl.BlockSpec((1, 3 * d), const),
            pl.BlockSpec((yw, d), const),
            pl.BlockSpec((yw, d), const),
            pl.BlockSpec((yw, d), const),
            pl.BlockSpec((d, d), const),
        ],
        out_specs=pl.BlockSpec((1, tn, d), tok),
        out_shape=jax.ShapeDtypeStruct((b, n, d), F32),
        compiler_params=_cparams(("parallel", "parallel")),
        name="merge",
    )(x, mod, g1, ya, yb, yc, wg, bg, wa, wb, wc, wo)


def _first_index(hit, idx, big):
    return jnp.min(jnp.where(hit, idx, big), axis=0, keepdims=True)


def _route_kernel(x_ref, mod_ref, g2_ref, wrt_ref, eb_ref, ws1_ref, ws3_ref, ws2_ref,
                  f_ref, idx_ref, w_ref, xs_ref, *, d):
    x = x_ref[0]
    mod = mod_ref[0]
    f = _rms_rows(x, g2_ref[...]) * (1.0 + mod[:, 4 * d:5 * d]) + mod[:, 3 * d:4 * d]
    f_ref[0] = f
    tn = f.shape[0]
    logits = lax.dot_general(wrt_ref[...], f, (((1,), (1,)), ((), ())),
                             preferred_element_type=F32, precision=lax.Precision.HIGHEST)
    scores = _sigmoid(logits)
    choice = scores + eb_ref[...]
    eidx = lax.broadcasted_iota(jnp.int32, (N_EXPERTS, tn), 0)
    lidx = lax.broadcasted_iota(jnp.int32, (GROUP_SIZE, tn), 0)
    gscore = []
    for g in range(N_GROUPS):
        cg = choice[g * GROUP_SIZE:(g + 1) * GROUP_SIZE, :]
        m1 = jnp.max(cg, axis=0, keepdims=True)
        first = _first_index(cg == m1, lidx, GROUP_SIZE)
        m2 = jnp.max(jnp.where(lidx == first, NEG, cg), axis=0, keepdims=True)
        gscore.append(m1 + m2)
    gs = jnp.concatenate(gscore, axis=0)
    gidx = lax.broadcasted_iota(jnp.int32, (N_GROUPS, tn), 0)
    gsel = jnp.zeros((N_GROUPS, tn), jnp.bool_)
    for _ in range(TOPK_GROUPS):
        gm = jnp.max(gs, axis=0, keepdims=True)
        first = _first_index(gs == gm, gidx, N_GROUPS)
        hit = gidx == first
        gsel = gsel | hit
        gs = jnp.where(hit, NEG, gs)
    gself = gsel.astype(F32)
    emask = jnp.concatenate(
        [jnp.broadcast_to(gself[g:g + 1, :], (GROUP_SIZE, tn)) for g in range(N_GROUPS)], axis=0)
    cur = jnp.where(emask > 0.5, choice, NEG)
    ids, ws = [], []
    for _ in range(TOP_K):
        m = jnp.max(cur, axis=0, keepdims=True)
        first = _first_index(cur == m, eidx, N_EXPERTS)
        hit = eidx == first
        ids.append(first)
        ws.append(jnp.sum(jnp.where(hit, scores, 0.0), axis=0, keepdims=True))
        cur = jnp.where(hit, NEG, cur)
    wsel = jnp.concatenate(ws, axis=0)
    idx_ref[0] = jnp.concatenate(ids, axis=0)
    w_ref[0] = wsel / jnp.sum(wsel, axis=0, keepdims=True) * ROUTE_SCALE
    fb = f.astype(BF16)
    h = _silu(_dot(fb, ws1_ref[...])) * _dot(fb, ws3_ref[...])
    xs_ref[0] = x + mod[:, 5 * d:6 * d] * _dot(h.astype(BF16), ws2_ref[...])


def _route(x, mod, g2, wrt, eb, ws1, ws3, ws2, n_ctx_tiles):
    b, n, d = x.shape
    tn = TOK_TILE
    nb = b
    ds = ws1.shape[-1]

    def mod_idx(bi, t):
        return (jnp.where(t < n_ctx_tiles, nb, bi), 0, 0)

    tok = lambda bi, t: (bi, t, 0)
    lane_tok = lambda bi, t: (bi, 0, t)
    const = lambda bi, t: (0, 0)
    return pl.pallas_call(
        functools.partial(_route_kernel, d=d),
        grid=(b, n // tn),
        in_specs=[
            pl.BlockSpec((1, tn, d), tok),
            pl.BlockSpec((1, 1, N_MOD * d), mod_idx),
            pl.BlockSpec((1, d), const),
            pl.BlockSpec((N_EXPERTS, d), const),
            pl.BlockSpec((N_EXPERTS, 1), const),
            pl.BlockSpec((d, ds), const),
            pl.BlockSpec((d, ds), const),
            pl.BlockSpec((ds, d), const),
        ],
        out_specs=[
            pl.BlockSpec((1, tn, d), tok),
            pl.BlockSpec((1, TOP_K, tn), lane_tok),
            pl.BlockSpec((1, TOP_K, tn), lane_tok),
            pl.BlockSpec((1, tn, d), tok),
        ],
        out_shape=[
            jax.ShapeDtypeStruct((b, n, d), F32),
            jax.ShapeDtypeStruct((b, TOP_K, n), jnp.int32),
            jax.ShapeDtypeStruct((b, TOP_K, n), F32),
            jax.ShapeDtypeStruct((b, n, d), F32),
        ],
        compiler_params=_cparams(("parallel", "parallel")),
        name="route_shared",
    )(x, mod, g2, wrt, eb, ws1, ws3, ws2)


def _expert_kernel(bexp_ref, nvalid_ref, tok_ref, tok_next_ref, dst_ref, wrow_ref, f_hbm,
                   w1_ref, w3_ref, w2_ref, y_hbm, xbuf, ybuf, gsem, ssem):
    i = pl.program_id(0)
    nvalid = nvalid_ref[0]
    slot = i % 2
    rows = EXPERT_BLOCK

    def gather(idx_ref, s):
        for j in range(rows):
            pltpu.make_async_copy(f_hbm.at[pl.ds(idx_ref[0, 0, j], 1)],
                                  xbuf.at[s, pl.ds(j, 1)], gsem.at[s]).start()

    def gather_wait(s):
        pltpu.make_async_copy(f_hbm.at[pl.ds(0, rows)], xbuf.at[s], gsem.at[s]).wait()

    def scatter(s):
        for j in range(rows):
            pltpu.make_async_copy(ybuf.at[s, pl.ds(j, 1)],
                                  y_hbm.at[pl.ds(dst_ref[0, 0, j], 1)], ssem.at[s]).start()

    def scatter_wait(s):
        pltpu.make_async_copy(ybuf.at[s], y_hbm.at[pl.ds(0, rows)], ssem.at[s]).wait()

    @pl.when(i == 0)
    def _():
        ybuf[1] = jnp.zeros((rows, ybuf.shape[-1]), F32)
        for h in range(2):
            cp = pltpu.make_async_copy(
                ybuf.at[1], y_hbm.at[pl.ds(y_hbm.shape[0] - (2 - h) * rows, rows)], ssem.at[1])
            cp.start()
            cp.wait()

    @pl.when((i == 0) & (nvalid > 0))
    def _():
        gather(tok_ref, 0)

    @pl.when(i + 1 < nvalid)
    def _():
        gather(tok_next_ref, 1 - slot)

    @pl.when(i < nvalid)
    def _():
        gather_wait(slot)
        xb = xbuf[slot].astype(BF16)
        h = _silu(_dot(xb, w1_ref[0])) * _dot(xb, w3_ref[0])
        ybuf[slot] = _dot(h.astype(BF16), w2_ref[0]) * wrow_ref[0]
        scatter(slot)

        @pl.when(i >= 1)
        def _():
            scatter_wait(1 - slot)

        @pl.when(i == nvalid - 1)
        def _():
            scatter_wait(slot)


def _experts(bexp, nvalid, row_tok, row_dst, wrow, f2d, w1, w3, w2, n_out_rows):
    n_blocks = bexp.shape[0]
    d = f2d.shape[-1]
    de = w1.shape[-1]
    rows = EXPERT_BLOCK
    last = n_blocks - 1
    grid_spec = pltpu.PrefetchScalarGridSpec(
        num_scalar_prefetch=2,
        grid=(n_blocks,),
        in_specs=[
            pl.BlockSpec((1, 1, rows), lambda i, be, nv: (i, 0, 0), memory_space=pltpu.SMEM),
            pl.BlockSpec((1, 1, rows), lambda i, be, nv: (jnp.minimum(i + 1, last), 0, 0),
                         memory_space=pltpu.SMEM),
            pl.BlockSpec((1, 1, rows), lambda i, be, nv: (i, 0, 0), memory_space=pltpu.SMEM),
            pl.BlockSpec((1, rows, 1), lambda i, be, nv: (i, 0, 0)),
            pl.BlockSpec(memory_space=pl.ANY),
            pl.BlockSpec((1, d, de), lambda i, be, nv: (be[i], 0, 0)),
            pl.BlockSpec((1, d, de), lambda i, be, nv: (be[i], 0, 0)),
            pl.BlockSpec((1, de, d), lambda i, be, nv: (be[i], 0, 0)),
        ],
        out_specs=pl.BlockSpec(memory_space=pl.ANY),
        scratch_shapes=[
            pltpu.VMEM((2, rows, d), F32),
            pltpu.VMEM((2, rows, d), F32),
            pltpu.SemaphoreType.DMA((2,)),
            pltpu.SemaphoreType.DMA((2,)),
        ],
    )
    return pl.pallas_call(
        _expert_kernel,
        grid_spec=grid_spec,
        out_shape=jax.ShapeDtypeStruct((n_out_rows, d), F32),
        compiler_params=_cparams(("arbitrary",)),
        name="experts",
    )(bexp, nvalid, row_tok, row_tok, row_dst, wrow, f2d, w1, w3, w2)


def _routing_tables(idx, w, n_tokens):
    n_assign = n_tokens * TOP_K
    blk = EXPERT_BLOCK
    flat_e = idx.reshape(-1).astype(jnp.int32)
    order = jnp.argsort(flat_e).astype(jnp.int32)
    e_sorted = flat_e[order]
    counts = jnp.zeros((N_EXPERTS,), jnp.int32).at[flat_e].add(1)
    starts = jnp.cumsum(counts) - counts
    padded = (counts + blk - 1) // blk * blk
    padded_end = jnp.cumsum(padded)
    padded_start = padded_end - padded
    dest = padded_start[e_sorted] + jnp.arange(n_assign, dtype=jnp.int32) - starts[e_sorted]
    n_blocks = n_assign // blk + N_EXPERTS
    n_rows = n_blocks * blk
    row_src = jnp.full((n_rows,), -1, jnp.int32).at[dest].set(order)
    valid = row_src >= 0
    src = jnp.maximum(row_src, 0)
    tok = src // TOP_K
    slot = src % TOP_K
    row_in_blk = jnp.arange(n_rows, dtype=jnp.int32) % blk
    blk_par = (jnp.arange(n_rows, dtype=jnp.int32) // blk) % 2
    trash = n_assign + blk_par * blk + row_in_blk
    row_dst = jnp.where(valid, slot * n_tokens + tok, trash)
    row_w = jnp.where(valid, w.reshape(-1)[src], 0.0).astype(F32)
    block_start = jnp.arange(n_blocks, dtype=jnp.int32) * blk
    bexp = jnp.minimum(jnp.searchsorted(padded_end, block_start, side="right"),
                       N_EXPERTS - 1).astype(jnp.int32)
    nvalid = (padded_end[-1] // blk).astype(jnp.int32).reshape(1)
    return (bexp, nvalid, tok.reshape(n_blocks, 1, blk), row_dst.reshape(n_blocks, 1, blk),
            row_w.reshape(n_blocks, blk, 1))


def _combine_kernel(xs_ref, mod_ref, *refs, d):
    y_refs, o_ref = refs[:TOP_K], refs[TOP_K]
    tot = y_refs[0][...]
    for r in y_refs[1:]:
        tot = tot + r[...]
    o_ref[0] = xs_ref[0] + mod_ref[0][:, 5 * d:6 * d] * tot


def _combine(xs, mod, y, n_ctx_tiles):
    b, n, d = xs.shape
    tn = TOK_TILE
    nb = b
    tiles = n // tn
    per_slot = b * tiles

    def mod_idx(bi, t):
        return (jnp.where(t < n_ctx_tiles, nb, bi), 0, 0)

    tok = lambda bi, t: (bi, t, 0)
    y_specs = [pl.BlockSpec((tn, d), functools.partial(lambda bi, t, k: (k * per_slot + bi * tiles + t, 0), k=k))
               for k in range(TOP_K)]
    return pl.pallas_call(
        functools.partial(_combine_kernel, d=d),
        grid=(b, tiles),
        in_specs=[pl.BlockSpec((1, tn, d), tok), pl.BlockSpec((1, 1, N_MOD * d), mod_idx)] + y_specs,
        out_specs=pl.BlockSpec((1, tn, d), tok),
        out_shape=jax.ShapeDtypeStruct((b, n, d), F32),
        compiler_params=_cparams(("parallel", "parallel")),
        name="combine",
    )(xs, mod, *([y] * TOP_K))


def _final_kernel(x_ref, g_ref, o_ref):
    o_ref[0] = _rms_rows(x_ref[0], g_ref[...])


def _final_norm(x, g, n_ctx_tiles):
    b, n, d = x.shape
    tn = TOK_TILE
    n_lat_tiles = n // tn - n_ctx_tiles
    return pl.pallas_call(
        _final_kernel,
        grid=(b, n_lat_tiles),
        in_specs=[pl.BlockSpec((1, tn, d), lambda bi, t: (bi, t + n_ctx_tiles, 0)),
                  pl.BlockSpec((1, d), lambda bi, t: (0, 0))],
        out_specs=pl.BlockSpec((1, tn, d), lambda bi, t: (bi, t, 0)),
        out_shape=jax.ShapeDtypeStruct((b, n_lat_tiles * tn, d), F32),
        compiler_params=_cparams(("parallel", "parallel")),
        name="final_norm",
    )(x, g)


def _rotary_column_tables():
    a_q, a_kv = A_HEADS * HEAD_DIM, A_KV * HEAD_DIM
    b_q, b_kv = B_HEADS * HEAD_DIM, B_KV * HEAD_DIM
    c_qk, c_v = 2 * C_HEADS * HEAD_DIM, C_HEADS * 2 * HEAD_DIM
    sizes = (a_q, a_kv, a_kv, b_q, b_kv, b_kv, c_qk, c_qk, c_v)
    off = np.concatenate([[0], np.cumsum(sizes)])
    seg = lambda i: np.arange(off[i], off[i + 1])
    perm_r = np.concatenate([seg(0), seg(1), seg(3), seg(4), seg(6), seg(7)])
    perm_v = np.concatenate([seg(2), seg(5), seg(8)])
    col = np.arange(R_W)
    i = col % HEAD_DIM
    partner_local = np.where((i % (2 * ROPE_FREQS)) < ROPE_FREQS, i + ROPE_FREQS, i - ROPE_FREQS)
    partner = col - i + partner_local
    return perm_r, perm_v, partner


def _rope_tables(n_ctx, n_lat):
    t = jnp.arange(n_lat, dtype=jnp.int32)
    row_pos = (t // GRID_W).astype(F32)
    col_pos = (t % GRID_W).astype(F32)
    inv_freq = jnp.power(ROPE_THETA, -jnp.arange(ROPE_FREQS, dtype=F32) / ROPE_FREQS)
    ang_r = row_pos[:, None] * inv_freq
    ang_c = col_pos[:, None] * inv_freq
    cos64 = jnp.concatenate([jnp.cos(ang_r), jnp.cos(ang_r), jnp.cos(ang_c), jnp.cos(ang_c)], axis=1)
    sin64 = jnp.concatenate([-jnp.sin(ang_r), jnp.sin(ang_r), -jnp.sin(ang_c), jnp.sin(ang_c)], axis=1)
    cos64 = jnp.concatenate([jnp.ones((n_ctx, HEAD_DIM), F32), cos64], axis=0)
    sin64 = jnp.concatenate([jnp.zeros((n_ctx, HEAD_DIM), F32), sin64], axis=0)
    return jnp.tile(cos64, (1, 2)), jnp.tile(sin64, (1, 2))


def _heads_major(t, n_heads):
    b, n, w = t.shape
    return t.reshape(b, n, n_heads, w // n_heads).transpose(0, 2, 1, 3)


def _heads_transposed(t, n_heads):
    b, n, w = t.shape
    return t.reshape(b, n, n_heads, w // n_heads).transpose(0, 2, 3, 1)


def _tokens_major(t):
    b, h, n, dv = t.shape
    return t.transpose(0, 2, 1, 3).reshape(b, n, h * dv)


def kernel(x, c, ctx, c_ctx, w_mod, b_mod, g_norm1, w_qkv, g_qnorm_a, g_knorm_a, sink_b, lam_q1, lam_k1, lam_q2, lam_k2, g_subln_c, w_br_a, w_br_b, w_br_c, w_gate, b_gate, w_out, g_norm2, w_router, e_bias, w1, w3, w2, ws1, ws3, ws2, g_final):
    bsz, n_lat, d = x.shape
    n_ctx = ctx.shape[1]
    depth = w_mod.shape[0]
    n_tok = n_ctx + n_lat
    assert n_ctx % TOK_TILE == 0 and n_lat % KEY_CHUNK == 0 and n_ctx % 128 == 0
    n_ctx_tiles = n_ctx // TOK_TILE
    n_all = bsz * n_tok

    rows = -(-(bsz + 1) // 8) * 8
    cvec = jnp.concatenate([c, c_ctx[None, :], jnp.zeros((rows - bsz - 1, d), F32)], axis=0)
    mod_all = _modulation(cvec, w_mod, b_mod)

    perm_r, perm_v, partner = _rotary_column_tables()
    cos_t, sin_t = _rope_tables(n_ctx, n_lat)
    head_of = np.arange(R_A) // HEAD_DIM
    ones_blk = jnp.asarray((head_of[:, None] == head_of[None, :]).astype(np.float32), BF16)
    scale = HEAD_DIM ** -0.5
    unit = jnp.ones((HEAD_DIM,), F32)

    xs = jnp.concatenate([ctx, x], axis=1)
    for l in range(depth):
        lam_init = 0.8 - 0.6 * math.exp(-0.3 * l)
        lam = (jnp.exp(jnp.dot(lam_q1[l], lam_k1[l])) - jnp.exp(jnp.dot(lam_q2[l], lam_k2[l]))).astype(F32) + lam_init
        mod = mod_all[l].reshape(rows, 1, N_MOD * d)
        g1 = g_norm1[l].reshape(1, d)

        w_r = w_qkv[l][:, perm_r]
        wqkv = jnp.concatenate([w_r, w_qkv[l][:, perm_v]], axis=1).astype(BF16)
        wsw = w_r[:, partner].astype(BF16)
        grow = jnp.concatenate([jnp.tile(g_qnorm_a[l] * scale, A_HEADS), jnp.tile(g_knorm_a[l], A_KV),
                                jnp.tile(unit * scale, B_HEADS), jnp.tile(unit, B_KV),
                                jnp.tile(unit * scale, 2 * C_HEADS), jnp.tile(unit, 2 * C_HEADS)])
        gsrow = grow[partner]
        r_all, v_all = _pre_attention(xs, mod, g1, wqkv, wsw, ones_blk, grow.reshape(1, R_W),
                                      gsrow.reshape(1, R_W), cos_t, sin_t, n_ctx_tiles)

        o = 0
        qa = _heads_major(r_all[..., o:o + A_HEADS * HEAD_DIM], A_HEADS); o += A_HEADS * HEAD_DIM
        kta = _heads_transposed(r_all[..., o:o + A_KV * HEAD_DIM], A_KV); o += A_KV * HEAD_DIM
        qb = _heads_major(r_all[..., o:o + B_HEADS * HEAD_DIM], B_HEADS); o += B_HEADS * HEAD_DIM
        ktb = _heads_transposed(r_all[..., o:o + B_KV * HEAD_DIM], B_KV); o += B_KV * HEAD_DIM
        qc = _heads_major(r_all[..., o:o + 2 * C_HEADS * HEAD_DIM], 2 * C_HEADS); o += 2 * C_HEADS * HEAD_DIM
        ktc = _heads_transposed(r_all[..., o:o + 2 * C_HEADS * HEAD_DIM], 2 * C_HEADS)
        o = 0
        va = _heads_major(v_all[..., o:o + A_KV * HEAD_DIM], A_KV); o += A_KV * HEAD_DIM
        vb = _heads_major(v_all[..., o:o + B_KV * HEAD_DIM], B_KV); o += B_KV * HEAD_DIM
        vc = _heads_major(v_all[..., o:], C_HEADS)

        scal = jnp.concatenate([sink_b[l].astype(F32), lam.reshape(1), jnp.full((1,), 1.0 - lam_init, F32)])
        scal_c = jnp.concatenate([jnp.zeros((2,), F32), lam.reshape(1), jnp.full((1,), 1.0 - lam_init, F32)])
        gsub = g_subln_c[l].reshape(1, 2 * HEAD_DIM)
        g64 = jnp.ones((1, HEAD_DIM), F32)
        ya = _attention(scal, g64, qa, kta, va, heads=A_HEADS // A_KV, shared_k=True, mode="global", n_ctx=n_ctx)
        yb = _attention(scal, g64, qb, ktb, vb, heads=B_HEADS // B_KV, shared_k=True, mode="window", n_ctx=n_ctx)
        yc = _attention(scal_c, gsub, qc, ktc, vc, heads=2, shared_k=False, mode="diff", n_ctx=n_ctx)

        x1 = _merge(xs, mod, g1, _tokens_major(ya), _tokens_major(yb), _tokens_major(yc),
                    w_gate[l].astype(BF16), b_gate[l].reshape(1, -1), w_br_a[l].astype(BF16),
                    w_br_b[l].astype(BF16), w_br_c[l].astype(BF16), w_out[l].astype(BF16), n_ctx_tiles)

        f, idx_t, w_t, x_sh = _route(x1, mod, g_norm2[l].reshape(1, d), w_router[l].T,
                                     e_bias[l].reshape(N_EXPERTS, 1), ws1[l].astype(BF16),
                                     ws3[l].astype(BF16), ws2[l].astype(BF16), n_ctx_tiles)

        idx = idx_t.transpose(0, 2, 1).reshape(n_all, TOP_K)
        wts = w_t.transpose(0, 2, 1).reshape(n_all, TOP_K)
        bexp, nvalid, row_tok, row_dst, row_w = _routing_tables(idx, wts, n_all)
        y = _experts(bexp, nvalid, row_tok, row_dst, row_w, f.reshape(n_all, d),
                     w1[l].astype(BF16), w3[l].astype(BF16), w2[l].astype(BF16),
                     n_all * TOP_K + 2 * EXPERT_BLOCK)
        xs = _combine(x_sh, mod, y, n_ctx_tiles)

    return _final_norm(xs, g_final.reshape(1, d), n_ctx_tiles)
```

```python
import functools
import math

import numpy as np
import jax
import jax.numpy as jnp
from jax import lax
from jax.experimental import pallas as pl
from jax.experimental.pallas import tpu as pltpu

F32 = jnp.float32
BF16 = jnp.bfloat16

HEAD_DIM = 64
ROPE_FREQS = HEAD_DIM // 4
ROPE_THETA = 10000.0
GRID_W = 64
WINDOW = 128
A_HEADS, A_KV = 8, 2
B_HEADS, B_KV = 8, 2
C_HEADS = 4
N_EXPERTS = 128
TOP_K = 8
N_GROUPS = 8
TOPK_GROUPS = 4
GROUP_SIZE = N_EXPERTS // N_GROUPS
ROUTE_SCALE = 2.5
EXPERT_BLOCK = 128
N_MOD = 6
EPS = 1e-6
NEG = -1e30

R_A = A_HEADS * HEAD_DIM + A_KV * HEAD_DIM
R_B = B_HEADS * HEAD_DIM + B_KV * HEAD_DIM
R_C = 4 * C_HEADS * HEAD_DIM
R_W = R_A + R_B + R_C
V_W = A_KV * HEAD_DIM + B_KV * HEAD_DIM + C_HEADS * 2 * HEAD_DIM

TOK_TILE = 256
KEY_CHUNK = 512
MAX_KEY_CHUNK = {"global": 8448, "diff": 4224, "window": 128}
LOG2E = math.log2(math.e)
VMEM_LIMIT = 56 * 1024 * 1024


def _cparams(sem, **kw):
    return pltpu.CompilerParams(dimension_semantics=sem, vmem_limit_bytes=VMEM_LIMIT, **kw)


def _dot(a, b):
    return jnp.dot(a, b, preferred_element_type=F32)


def _sigmoid(x):
    return 1.0 / (1.0 + jnp.exp(-x))


def _silu(x):
    return x * _sigmoid(x)


def _rms_rows(x, g):
    return x * lax.rsqrt(jnp.mean(x * x, axis=-1, keepdims=True) + EPS) * g


def _mod_kernel(c_ref, w_ref, b_ref, o_ref):
    cs = _silu(c_ref[...])
    o_ref[0] = jnp.dot(cs, w_ref[0], preferred_element_type=F32,
                       precision=lax.Precision.HIGHEST) + b_ref[0]


def _modulation(cvec, w_mod, b_mod):
    depth, d, n = w_mod.shape
    rows = cvec.shape[0]
    bn = 1536
    return pl.pallas_call(
        _mod_kernel,
        grid=(depth, n // bn),
        in_specs=[
            pl.BlockSpec((rows, d), lambda l, j: (0, 0)),
            pl.BlockSpec((1, d, bn), lambda l, j: (l, 0, j)),
            pl.BlockSpec((1, 1, bn), lambda l, j: (l, 0, j)),
        ],
        out_specs=pl.BlockSpec((1, rows, bn), lambda l, j: (l, 0, j)),
        out_shape=jax.ShapeDtypeStruct((depth, rows, n), F32),
        compiler_params=_cparams(("parallel", "parallel")),
        name="modulation",
    )(cvec, w_mod, b_mod.reshape(depth, 1, n))


def _pre_kernel(x_ref, mod_ref, g1_ref, wqkv_ref, wsw_ref, ones_ref, grow_ref, gsrow_ref,
                cos_ref, sin_ref, r_out, v_out, *, d):
    x = x_ref[0]
    mod = mod_ref[0]
    a = (_rms_rows(x, g1_ref[...]) * (1.0 + mod[:, d:2 * d]) + mod[:, 0:d]).astype(BF16)
    p = _dot(a, wqkv_ref[...])
    ps = _dot(a, wsw_ref[...])
    pa = p[:, :R_A]
    sq = pa * pa
    hi = sq.astype(BF16)
    lo = (sq - hi.astype(F32)).astype(BF16)
    ssq = _dot(hi, ones_ref[...]) + _dot(lo, ones_ref[...])
    rinv = lax.rsqrt(ssq * (1.0 / HEAD_DIM) + EPS)
    cos = cos_ref[...]
    sin = sin_ref[...]
    n_tiles = R_W // 128
    for j in range(n_tiles):
        sl = slice(j * 128, (j + 1) * 128)
        o = p[:, sl] * (grow_ref[:, sl] * cos) + ps[:, sl] * (gsrow_ref[:, sl] * sin)
        if (j + 1) * 128 <= R_A:
            o = o * rinv[:, sl]
        r_out[0, :, sl] = o.astype(BF16)
    v_out[0] = p[:, R_W:].astype(BF16)


def _pre_attention(x, mod, g1, wqkv, wsw, ones_blk, grow, gsrow, cos_t, sin_t, n_ctx_tiles):
    b, n, d = x.shape
    tn = TOK_TILE
    nb = b

    def mod_idx(bi, t):
        return (jnp.where(t < n_ctx_tiles, nb, bi), 0, 0)

    return pl.pallas_call(
        functools.partial(_pre_kernel, d=d),
        grid=(b, n // tn),
        in_specs=[
            pl.BlockSpec((1, tn, d), lambda bi, t: (bi, t, 0)),
            pl.BlockSpec((1, 1, N_MOD * d), mod_idx),
            pl.BlockSpec((1, d), lambda bi, t: (0, 0)),
            pl.BlockSpec((d, R_W + V_W), lambda bi, t: (0, 0)),
            pl.BlockSpec((d, R_W), lambda bi, t: (0, 0)),
            pl.BlockSpec((R_A, R_A), lambda bi, t: (0, 0)),
            pl.BlockSpec((1, R_W), lambda bi, t: (0, 0)),
            pl.BlockSpec((1, R_W), lambda bi, t: (0, 0)),
            pl.BlockSpec((tn, 128), lambda bi, t: (t, 0)),
            pl.BlockSpec((tn, 128), lambda bi, t: (t, 0)),
        ],
        out_specs=[
            pl.BlockSpec((1, tn, R_W), lambda bi, t: (bi, t, 0)),
            pl.BlockSpec((1, tn, V_W), lambda bi, t: (bi, t, 0)),
        ],
        out_shape=[
            jax.ShapeDtypeStruct((b, n, R_W), BF16),
            jax.ShapeDtypeStruct((b, n, V_W), BF16),
        ],
        compiler_params=_cparams(("parallel", "parallel")),
        name="pre_attention",
    )(x, mod, g1, wqkv, wsw, ones_blk, grow, gsrow, cos_t, sin_t)


def _attn_kernel(sc_ref, gsub_ref, q_ref, kt_ref, v_ref, o_ref, m_sc, acc_sc, *, heads,
                 shared_k, mode, n_ctx, n_tok, tq, ck, dv):
    u = pl.program_id(1)
    qi = pl.program_id(2)
    is_lat = qi >= n_ctx // tq
    dvx = acc_sc.shape[-1]

    for g in range(heads):
        if mode == "window":
            m_sc[g] = jnp.full((tq, 1), sc_ref[u * heads + g], F32)
            lane = lax.broadcasted_iota(jnp.int32, (tq, dvx), 1)
            acc_sc[g] = jnp.where(lane == dv, 1.0, 0.0).astype(F32)
        else:
            m_sc[g] = jnp.full((tq, 1), NEG, F32)
            acc_sc[g] = jnp.zeros((tq, dvx), F32)

    def update(g, kt, v, mask=None):
        s = _dot(q_ref[0, g], kt)
        if mask is not None:
            s = jnp.where(mask, s, NEG)
        m = m_sc[g]
        m_new = jnp.maximum(m, jnp.max(s, axis=-1, keepdims=True))
        p = jnp.exp2(s - m_new).astype(BF16)
        acc_sc[g] = jnp.exp2(m - m_new) * acc_sc[g] + _dot(p, v)
        m_sc[g] = m_new

    def step(k0, size):
        v = v_ref[0, 0, pl.ds(k0, size), :]
        for g in range(heads):
            update(g, kt_ref[0, 0 if shared_k else g, :, pl.ds(k0, size)], v)

    if mode == "window":
        span = tq + 2 * WINDOW
        start = pl.multiple_of(jnp.clip(qi * tq - WINDOW, 0, n_tok - span), 128)
        qpos = qi * tq + lax.broadcasted_iota(jnp.int32, (tq, n_ctx + span), 0)
        col = lax.broadcasted_iota(jnp.int32, (tq, n_ctx + span), 1)
        kpos = start + col - n_ctx
        mask = (col < n_ctx) | (is_lat & (kpos >= n_ctx) & (jnp.abs(kpos - qpos) <= WINDOW))
        v = jnp.concatenate([v_ref[0, 0, 0:n_ctx, :], v_ref[0, 0, pl.ds(start, span), :]], axis=0)
        for g in range(heads):
            gk = 0 if shared_k else g
            kt = jnp.concatenate([kt_ref[0, gk, :, 0:n_ctx], kt_ref[0, gk, :, pl.ds(start, span)]], axis=1)
            update(g, kt, v, mask)
    else:
        @pl.when(jnp.logical_not(is_lat))
        def _():
            step(0, n_ctx)

        def body(ci, carry):
            step(pl.multiple_of(ci * ck, 128), ck)
            return carry

        lax.fori_loop(0, jnp.where(is_lat, n_tok // ck, 0), body, 0)

    def result(g):
        acc = acc_sc[g]
        return acc[:, 0:dv] / acc[:, dv:dv + 1]

    if mode == "diff":
        lam = sc_ref[heads]
        post = sc_ref[heads + 1]
        y = result(0) - lam * result(1)
        o_ref[0, 0] = (_rms_rows(y, gsub_ref[...]) * post).astype(o_ref.dtype)
    else:
        for g in range(heads):
            o_ref[0, g] = result(g).astype(o_ref.dtype)


def _attention(scalars, gsub, q, kt, v, *, heads, shared_k, mode, n_ctx):
    b, hq, n, hd = q.shape
    units = hq // heads
    dv = v.shape[-1]
    dvx = (dv // 128 + 1) * 128
    tq = TOK_TILE
    gk = 1 if shared_k else heads
    out_heads = 1 if mode == "diff" else heads
    ck = max(c for c in range(128, MAX_KEY_CHUNK[mode] + 1, 128) if n % c == 0)
    ones_col = (lax.broadcasted_iota(jnp.int32, v.shape[:-1] + (dvx - dv,), v.ndim - 1) == 0).astype(v.dtype)
    vx = jnp.concatenate([v, ones_col], axis=-1)
    kern = functools.partial(_attn_kernel, heads=heads, shared_k=shared_k, mode=mode, n_ctx=n_ctx,
                             n_tok=n, tq=tq, ck=ck, dv=dv)
    return pl.pallas_call(
        kern,
        grid=(b, units, n // tq),
        in_specs=[
            pl.BlockSpec(memory_space=pltpu.SMEM),
            pl.BlockSpec((1, dv), lambda bi, u, t: (0, 0)),
            pl.BlockSpec((1, heads, tq, hd), lambda bi, u, t: (bi, u, t, 0)),
            pl.BlockSpec((1, gk, hd, n), lambda bi, u, t: (bi, u, 0, 0)),
            pl.BlockSpec((1, 1, n, dvx), lambda bi, u, t: (bi, u, 0, 0)),
        ],
        out_specs=pl.BlockSpec((1, out_heads, tq, dv), lambda bi, u, t: (bi, u, t, 0)),
        out_shape=jax.ShapeDtypeStruct((b, units * out_heads, n, dv), BF16),
        scratch_shapes=[
            pltpu.VMEM((heads, tq, 1), F32),
            pltpu.VMEM((heads, tq, dvx), F32),
        ],
        compiler_params=_cparams(("parallel", "parallel", "arbitrary")),
        name="attn_" + mode,
    )(scalars, gsub, q, kt, vx)


def _merge_kernel(x_ref, mod_ref, g1_ref, ya_ref, yb_ref, yc_ref, wg_ref, bg_ref, wa_ref, wb_ref,
                  wc_ref, wo_ref, o_ref, *, d):
    x = x_ref[0]
    mod = mod_ref[0]
    a = (_rms_rows(x, g1_ref[...]) * (1.0 + mod[:, d:2 * d]) + mod[:, 0:d]).astype(BF16)
    gate = _sigmoid(_dot(a, wg_ref[...]) + bg_ref[...])
    m = (gate[:, 0:d] * _dot(ya_ref[0], wa_ref[...])
         + gate[:, d:2 * d] * _dot(yb_ref[0], wb_ref[...])
         + gate[:, 2 * d:3 * d] * _dot(yc_ref[0], wc_ref[...]))
    mix = _dot(m.astype(BF16), wo_ref[...])
    o_ref[0] = x + mod[:, 2 * d:3 * d] * mix


def _merge(x, mod, g1, ya, yb, yc, wg, bg, wa, wb, wc, wo, n_ctx_tiles):
    b, n, d = x.shape
    tn = TOK_TILE
    nb = b
    yw = ya.shape[-1]

    def mod_idx(bi, t):
        return (jnp.where(t < n_ctx_tiles, nb, bi), 0, 0)

    tok = lambda bi, t: (bi, t, 0)
    const = lambda bi, t: (0, 0)
    return pl.pallas_call(
        functools.partial(_merge_kernel, d=d),
        grid=(b, n // tn),
        in_specs=[
            pl.BlockSpec((1, tn, d), tok),
            pl.BlockSpec((1, 1, N_MOD * d), mod_idx),
            pl.BlockSpec((1, d), const),
            pl.BlockSpec((1, tn, yw), tok),
            pl.BlockSpec((1, tn, yw), tok),
            pl.BlockSpec((1, tn, yw), tok),
            pl.BlockSpec((d, 3 * d), const),
            pl.BlockSpec((1, 3 * d), const),
            pl.BlockSpec((yw, d), const),
            pl.BlockSpec((yw, d), const),
            pl.BlockSpec((yw, d), const),
            pl.BlockSpec((d, d), const),
        ],
        out_specs=pl.BlockSpec((1, tn, d), tok),
        out_shape=jax.ShapeDtypeStruct((b, n, d), F32),
        compiler_params=_cparams(("parallel", "parallel")),
        name="merge",
    )(x, mod, g1, ya, yb, yc, wg, bg, wa, wb, wc, wo)


def _first_index(hit, idx, big):
    return jnp.min(jnp.where(hit, idx, big), axis=0, keepdims=True)


def _route_kernel(x_ref, mod_ref, g2_ref, wrt_ref, eb_ref, ws1_ref, ws3_ref, ws2_ref,
                  f_ref, idx_ref, w_ref, xs_ref, *, d):
    x = x_ref[0]
    mod = mod_ref[0]
    f = _rms_rows(x, g2_ref[...]) * (1.0 + mod[:, 4 * d:5 * d]) + mod[:, 3 * d:4 * d]
    f_ref[0] = f
    tn = f.shape[0]
    logits = lax.dot_general(wrt_ref[...], f, (((1,), (1,)), ((), ())),
                             preferred_element_type=F32, precision=lax.Precision.HIGHEST)
    scores = _sigmoid(logits)
    choice = scores + eb_ref[...]
    eidx = lax.broadcasted_iota(jnp.int32, (N_EXPERTS, tn), 0)
    lidx = lax.broadcasted_iota(jnp.int32, (GROUP_SIZE, tn), 0)
    gscore = []
    for g in range(N_GROUPS):
        cg = choice[g * GROUP_SIZE:(g + 1) * GROUP_SIZE, :]
        m1 = jnp.max(cg, axis=0, keepdims=True)
        first = _first_index(cg == m1, lidx, GROUP_SIZE)
        m2 = jnp.max(jnp.where(lidx == first, NEG, cg), axis=0, keepdims=True)
        gscore.append(m1 + m2)
    gs = jnp.concatenate(gscore, axis=0)
    gidx = lax.broadcasted_iota(jnp.int32, (N_GROUPS, tn), 0)
    gsel = jnp.zeros((N_GROUPS, tn), jnp.bool_)
    for _ in range(TOPK_GROUPS):
        gm = jnp.max(gs, axis=0, keepdims=True)
        first = _first_index(gs == gm, gidx, N_GROUPS)
        hit = gidx == first
        gsel = gsel | hit
        gs = jnp.where(hit, NEG, gs)
    gself = gsel.astype(F32)
    emask = jnp.concatenate(
        [jnp.broadcast_to(gself[g:g + 1, :], (GROUP_SIZE, tn)) for g in range(N_GROUPS)], axis=0)
    cur = jnp.where(emask > 0.5, choice, NEG)
    ids, ws = [], []
    for _ in range(TOP_K):
        m = jnp.max(cur, axis=0, keepdims=True)
        first = _first_index(cur == m, eidx, N_EXPERTS)
        hit = eidx == first
        ids.append(first)
        ws.append(jnp.sum(jnp.where(hit, scores, 0.0), axis=0, keepdims=True))
        cur = jnp.where(hit, NEG, cur)
    wsel = jnp.concatenate(ws, axis=0)
    idx_ref[0] = jnp.concatenate(ids, axis=0)
    w_ref[0] = wsel / jnp.sum(wsel, axis=0, keepdims=True) * ROUTE_SCALE
    fb = f.astype(BF16)
    h = _silu(_dot(fb, ws1_ref[...])) * _dot(fb, ws3_ref[...])
    xs_ref[0] = x + mod[:, 5 * d:6 * d] * _dot(h.astype(BF16), ws2_ref[...])


def _route(x, mod, g2, wrt, eb, ws1, ws3, ws2, n_ctx_tiles):
    b, n, d = x.shape
    tn = TOK_TILE
    nb = b
    ds = ws1.shape[-1]

    def mod_idx(bi, t):
        return (jnp.where(t < n_ctx_tiles, nb, bi), 0, 0)

    tok = lambda bi, t: (bi, t, 0)
    lane_tok = lambda bi, t: (bi, 0, t)
    const = lambda bi, t: (0, 0)
    return pl.pallas_call(
        functools.partial(_route_kernel, d=d),
        grid=(b, n // tn),
        in_specs=[
            pl.BlockSpec((1, tn, d), tok),
            pl.BlockSpec((1, 1, N_MOD * d), mod_idx),
            pl.BlockSpec((1, d), const),
            pl.BlockSpec((N_EXPERTS, d), const),
            pl.BlockSpec((N_EXPERTS, 1), const),
            pl.BlockSpec((d, ds), const),
            pl.BlockSpec((d, ds), const),
            pl.BlockSpec((ds, d), const),
        ],
        out_specs=[
            pl.BlockSpec((1, tn, d), tok),
            pl.BlockSpec((1, TOP_K, tn), lane_tok),
            pl.BlockSpec((1, TOP_K, tn), lane_tok),
            pl.BlockSpec((1, tn, d), tok),
        ],
        out_shape=[
            jax.ShapeDtypeStruct((b, n, d), F32),
            jax.ShapeDtypeStruct((b, TOP_K, n), jnp.int32),
            jax.ShapeDtypeStruct((b, TOP_K, n), F32),
            jax.ShapeDtypeStruct((b, n, d), F32),
        ],
        compiler_params=_cparams(("parallel", "parallel")),
        name="route_shared",
    )(x, mod, g2, wrt, eb, ws1, ws3, ws2)


def _expert_kernel(iexp_ref, iblk_ref, iflag_ref, nitems_ref, tok_ref, tok_next_ref, dst_ref,
                   erow_ref, f_hbm, w1_ref, w3_ref, w2_ref, y_hbm, xbuf, ybuf, gsem, ssem):
    i = pl.program_id(0)
    nitems = nitems_ref[0]
    blk = iblk_ref[i]
    expert = iexp_ref[i]
    first = (iflag_ref[i] & 1) != 0
    last = (iflag_ref[i] & 2) != 0
    slot = blk % 2
    rows = EXPERT_BLOCK
    n_blocks = y_hbm.shape[0] // rows

    def gather(idx_ref, s):
        for j in range(rows):
            pltpu.make_async_copy(f_hbm.at[pl.ds(idx_ref[0, 0, j], 1)],
                                  xbuf.at[s, pl.ds(j, 1)], gsem.at[s]).start()

    def gather_wait(s):
        pltpu.make_async_copy(f_hbm.at[pl.ds(0, rows)], xbuf.at[s], gsem.at[s]).wait()

    def scatter(s):
        for j in range(rows):
            pltpu.make_async_copy(ybuf.at[s, pl.ds(j, 1)],
                                  y_hbm.at[pl.ds(dst_ref[0, 0, j], 1)], ssem.at[s]).start()

    def scatter_wait(s):
        pltpu.make_async_copy(ybuf.at[s], y_hbm.at[pl.ds(0, rows)], ssem.at[s]).wait()

    valid = i < nitems

    @pl.when(valid & first & (blk == 0))
    def _():
        gather(tok_ref, 0)

    @pl.when(valid & first & (blk + 1 < n_blocks))
    def _():
        gather(tok_next_ref, 1 - slot)

    @pl.when(valid & first)
    def _():
        gather_wait(slot)

    @pl.when(valid)
    def _():
        xb = xbuf[slot].astype(BF16)
        h = _silu(_dot(xb, w1_ref[0])) * _dot(xb, w3_ref[0])
        y = _dot(h.astype(BF16), w2_ref[0]) * (erow_ref[0] == expert).astype(F32)

        @pl.when(first)
        def _():
            ybuf[slot] = y

        @pl.when(jnp.logical_not(first))
        def _():
            ybuf[slot] = ybuf[slot] + y

    @pl.when(valid & last)
    def _():
        scatter(slot)

        @pl.when(blk >= 1)
        def _():
            scatter_wait(1 - slot)

        @pl.when(blk == n_blocks - 1)
        def _():
            scatter_wait(slot)


def _experts(iexp, iblk, iflag, nitems, row_tok, row_dst, erow, f2d, w1, w3, w2):
    n_items = iexp.shape[0]
    n_blocks = row_tok.shape[0]
    d = f2d.shape[-1]
    de = w1.shape[-1]
    rows = EXPERT_BLOCK
    last = n_blocks - 1
    cur = lambda i, ie, ib, fl, nt: (ib[i], 0, 0)
    nxt = lambda i, ie, ib, fl, nt: (jnp.minimum(ib[i] + 1, last), 0, 0)
    wsel = lambda i, ie, ib, fl, nt: (ie[i], 0, 0)
    grid_spec = pltpu.PrefetchScalarGridSpec(
        num_scalar_prefetch=4,
        grid=(n_items,),
        in_specs=[
            pl.BlockSpec((1, 1, rows), cur, memory_space=pltpu.SMEM),
            pl.BlockSpec((1, 1, rows), nxt, memory_space=pltpu.SMEM),
            pl.BlockSpec((1, 1, rows), cur, memory_space=pltpu.SMEM),
            pl.BlockSpec((1, rows, 1), cur),
            pl.BlockSpec(memory_space=pl.ANY),
            pl.BlockSpec((1, d, de), wsel),
            pl.BlockSpec((1, d, de), wsel),
            pl.BlockSpec((1, de, d), wsel),
        ],
        out_specs=pl.BlockSpec(memory_space=pl.ANY),
        scratch_shapes=[
            pltpu.VMEM((2, rows, d), F32),
            pltpu.VMEM((2, rows, d), F32),
            pltpu.SemaphoreType.DMA((2,)),
            pltpu.SemaphoreType.DMA((2,)),
        ],
    )
    return pl.pallas_call(
        _expert_kernel,
        grid_spec=grid_spec,
        out_shape=jax.ShapeDtypeStruct((n_blocks * rows, d), F32),
        compiler_params=_cparams(("arbitrary",)),
        name="experts",
    )(iexp, iblk, iflag, nitems, row_tok, row_tok, row_dst, erow, f2d, w1, w3, w2)


ASSIGN_BITS = 20


def _routing_tables(idx, n_tokens):
    n_assign = n_tokens * TOP_K
    blk = EXPERT_BLOCK
    assert n_assign % blk == 0 and n_assign <= (1 << ASSIGN_BITS)
    n_blocks = n_assign // blk
    n_items = n_blocks + N_EXPERTS
    flat_e = idx.reshape(-1).astype(jnp.int32)
    key = jnp.sort((flat_e << ASSIGN_BITS) | jnp.arange(n_assign, dtype=jnp.int32))
    e_sorted = key >> ASSIGN_BITS
    order = key & ((1 << ASSIGN_BITS) - 1)
    tok = order // TOP_K
    row_dst = (order % TOP_K) * n_tokens + tok

    ends = jnp.searchsorted(e_sorted, jnp.arange(1, N_EXPERTS + 1, dtype=jnp.int32), side="left").astype(jnp.int32)
    starts = jnp.concatenate([jnp.zeros((1,), jnp.int32), ends[:-1]])
    first_blk = starts // blk
    n_be = jnp.where(ends > starts, (ends - 1) // blk - first_blk + 1, 0)
    item_end = jnp.cumsum(n_be)
    item_off = item_end - n_be
    total = item_end[-1]
    i = jnp.arange(n_items, dtype=jnp.int32)
    iexp = jnp.minimum(jnp.searchsorted(item_end, i, side="right"), N_EXPERTS - 1).astype(jnp.int32)
    iblk = jnp.where(i < total, first_blk[iexp] + i - item_off[iexp], n_blocks - 1).astype(jnp.int32)
    prev_blk = jnp.concatenate([jnp.full((1,), -1, jnp.int32), iblk[:-1]])
    next_blk = jnp.concatenate([iblk[1:], jnp.full((1,), -1, jnp.int32)])
    is_first = iblk != prev_blk
    is_last = (iblk != next_blk) | (i == total - 1)
    iflag = is_first.astype(jnp.int32) + 2 * is_last.astype(jnp.int32)
    return (iexp, iblk, iflag, total.astype(jnp.int32).reshape(1), tok.reshape(n_blocks, 1, blk),
            row_dst.reshape(n_blocks, 1, blk), e_sorted.reshape(n_blocks, blk, 1))


def _combine_kernel(xs_ref, mod_ref, w_ref, *refs, d):
    y_refs, o_ref = refs[:TOP_K], refs[TOP_K]
    w = w_ref[0]
    tot = y_refs[0][...] * w[:, 0:1]
    for k in range(1, TOP_K):
        tot = tot + y_refs[k][...] * w[:, k:k + 1]
    o_ref[0] = xs_ref[0] + mod_ref[0][:, 5 * d:6 * d] * tot


def _combine(xs, mod, w, y, n_ctx_tiles):
    b, n, d = xs.shape
    tn = TOK_TILE
    nb = b
    tiles = n // tn
    per_slot = b * tiles

    def mod_idx(bi, t):
        return (jnp.where(t < n_ctx_tiles, nb, bi), 0, 0)

    tok = lambda bi, t: (bi, t, 0)
    y_specs = [pl.BlockSpec((tn, d), functools.partial(lambda bi, t, k: (k * per_slot + bi * tiles + t, 0), k=k))
               for k in range(TOP_K)]
    return pl.pallas_call(
        functools.partial(_combine_kernel, d=d),
        grid=(b, tiles),
        in_specs=[pl.BlockSpec((1, tn, d), tok), pl.BlockSpec((1, 1, N_MOD * d), mod_idx),
                  pl.BlockSpec((1, tn, TOP_K), tok)] + y_specs,
        out_specs=pl.BlockSpec((1, tn, d), tok),
        out_shape=jax.ShapeDtypeStruct((b, n, d), F32),
        compiler_params=_cparams(("parallel", "parallel")),
        name="combine",
    )(xs, mod, w, *([y] * TOP_K))


def _final_kernel(x_ref, g_ref, o_ref):
    o_ref[0] = _rms_rows(x_ref[0], g_ref[...])


def _final_norm(x, g, n_ctx_tiles):
    b, n, d = x.shape
    tn = TOK_TILE
    n_lat_tiles = n // tn - n_ctx_tiles
    return pl.pallas_call(
        _final_kernel,
        grid=(b, n_lat_tiles),
        in_specs=[pl.BlockSpec((1, tn, d), lambda bi, t: (bi, t + n_ctx_tiles, 0)),
                  pl.BlockSpec((1, d), lambda bi, t: (0, 0))],
        out_specs=pl.BlockSpec((1, tn, d), lambda bi, t: (bi, t, 0)),
        out_shape=jax.ShapeDtypeStruct((b, n_lat_tiles * tn, d), F32),
        compiler_params=_cparams(("parallel", "parallel")),
        name="final_norm",
    )(x, g)


def _rotary_column_tables():
    a_q, a_kv = A_HEADS * HEAD_DIM, A_KV * HEAD_DIM
    b_q, b_kv = B_HEADS * HEAD_DIM, B_KV * HEAD_DIM
    c_qk, c_v = 2 * C_HEADS * HEAD_DIM, C_HEADS * 2 * HEAD_DIM
    sizes = (a_q, a_kv, a_kv, b_q, b_kv, b_kv, c_qk, c_qk, c_v)
    off = np.concatenate([[0], np.cumsum(sizes)])
    seg = lambda i: np.arange(off[i], off[i + 1])
    perm_r = np.concatenate([seg(0), seg(1), seg(3), seg(4), seg(6), seg(7)])
    perm_v = np.concatenate([seg(2), seg(5), seg(8)])
    col = np.arange(R_W)
    i = col % HEAD_DIM
    partner_local = np.where((i % (2 * ROPE_FREQS)) < ROPE_FREQS, i + ROPE_FREQS, i - ROPE_FREQS)
    partner = col - i + partner_local
    return perm_r, perm_v, partner


def _rope_tables(n_ctx, n_lat):
    t = jnp.arange(n_lat, dtype=jnp.int32)
    row_pos = (t // GRID_W).astype(F32)
    col_pos = (t % GRID_W).astype(F32)
    inv_freq = jnp.power(ROPE_THETA, -jnp.arange(ROPE_FREQS, dtype=F32) / ROPE_FREQS)
    ang_r = row_pos[:, None] * inv_freq
    ang_c = col_pos[:, None] * inv_freq
    cos64 = jnp.concatenate([jnp.cos(ang_r), jnp.cos(ang_r), jnp.cos(ang_c), jnp.cos(ang_c)], axis=1)
    sin64 = jnp.concatenate([-jnp.sin(ang_r), jnp.sin(ang_r), -jnp.sin(ang_c), jnp.sin(ang_c)], axis=1)
    cos64 = jnp.concatenate([jnp.ones((n_ctx, HEAD_DIM), F32), cos64], axis=0)
    sin64 = jnp.concatenate([jnp.zeros((n_ctx, HEAD_DIM), F32), sin64], axis=0)
    return jnp.tile(cos64, (1, 2)), jnp.tile(sin64, (1, 2))


def _heads_major(t, n_heads):
    b, n, w = t.shape
    return t.reshape(b, n, n_heads, w // n_heads).transpose(0, 2, 1, 3)


def _heads_transposed(t, n_heads):
    b, n, w = t.shape
    return t.reshape(b, n, n_heads, w // n_heads).transpose(0, 2, 3, 1)


def _tokens_major(t):
    b, h, n, dv = t.shape
    return t.transpose(0, 2, 1, 3).reshape(b, n, h * dv)


def kernel(x, c, ctx, c_ctx, w_mod, b_mod, g_norm1, w_qkv, g_qnorm_a, g_knorm_a, sink_b, lam_q1, lam_k1, lam_q2, lam_k2, g_subln_c, w_br_a, w_br_b, w_br_c, w_gate, b_gate, w_out, g_norm2, w_router, e_bias, w1, w3, w2, ws1, ws3, ws2, g_final):
    bsz, n_lat, d = x.shape
    n_ctx = ctx.shape[1]
    depth = w_mod.shape[0]
    n_tok = n_ctx + n_lat
    assert n_ctx % TOK_TILE == 0 and n_lat % KEY_CHUNK == 0 and n_ctx % 128 == 0
    n_ctx_tiles = n_ctx // TOK_TILE
    n_all = bsz * n_tok

    rows = -(-(bsz + 1) // 8) * 8
    cvec = jnp.concatenate([c, c_ctx[None, :], jnp.zeros((rows - bsz - 1, d), F32)], axis=0)
    mod_all = _modulation(cvec, w_mod, b_mod)

    perm_r, perm_v, partner = _rotary_column_tables()
    cos_t, sin_t = _rope_tables(n_ctx, n_lat)
    head_of = np.arange(R_A) // HEAD_DIM
    ones_blk = jnp.asarray((head_of[:, None] == head_of[None, :]).astype(np.float32), BF16)
    scale = HEAD_DIM ** -0.5 * LOG2E
    unit = jnp.ones((HEAD_DIM,), F32)

    xs = jnp.concatenate([ctx, x], axis=1)
    for l in range(depth):
        lam_init = 0.8 - 0.6 * math.exp(-0.3 * l)
        lam = (jnp.exp(jnp.dot(lam_q1[l], lam_k1[l])) - jnp.exp(jnp.dot(lam_q2[l], lam_k2[l]))).astype(F32) + lam_init
        mod = mod_all[l].reshape(rows, 1, N_MOD * d)
        g1 = g_norm1[l].reshape(1, d)

        w_r = w_qkv[l][:, perm_r]
        wqkv = jnp.concatenate([w_r, w_qkv[l][:, perm_v]], axis=1).astype(BF16)
        wsw = w_r[:, partner].astype(BF16)
        grow = jnp.concatenate([jnp.tile(g_qnorm_a[l] * scale, A_HEADS), jnp.tile(g_knorm_a[l], A_KV),
                                jnp.tile(unit * scale, B_HEADS), jnp.tile(unit, B_KV),
                                jnp.tile(unit * scale, 2 * C_HEADS), jnp.tile(unit, 2 * C_HEADS)])
        gsrow = grow[partner]
        r_all, v_all = _pre_attention(xs, mod, g1, wqkv, wsw, ones_blk, grow.reshape(1, R_W),
                                      gsrow.reshape(1, R_W), cos_t, sin_t, n_ctx_tiles)

        o = 0
        qa = _heads_major(r_all[..., o:o + A_HEADS * HEAD_DIM], A_HEADS); o += A_HEADS * HEAD_DIM
        kta = _heads_transposed(r_all[..., o:o + A_KV * HEAD_DIM], A_KV); o += A_KV * HEAD_DIM
        qb = _heads_major(r_all[..., o:o + B_HEADS * HEAD_DIM], B_HEADS); o += B_HEADS * HEAD_DIM
        ktb = _heads_transposed(r_all[..., o:o + B_KV * HEAD_DIM], B_KV); o += B_KV * HEAD_DIM
        qc = _heads_major(r_all[..., o:o + 2 * C_HEADS * HEAD_DIM], 2 * C_HEADS); o += 2 * C_HEADS * HEAD_DIM
        ktc = _heads_transposed(r_all[..., o:o + 2 * C_HEADS * HEAD_DIM], 2 * C_HEADS)
        o = 0
        va = _heads_major(v_all[..., o:o + A_KV * HEAD_DIM], A_KV); o += A_KV * HEAD_DIM
        vb = _heads_major(v_all[..., o:o + B_KV * HEAD_DIM], B_KV); o += B_KV * HEAD_DIM
        vc = _heads_major(v_all[..., o:], C_HEADS)

        scal = jnp.concatenate([sink_b[l].astype(F32) * LOG2E, lam.reshape(1), jnp.full((1,), 1.0 - lam_init, F32)])
        scal_c = jnp.concatenate([jnp.zeros((2,), F32), lam.reshape(1), jnp.full((1,), 1.0 - lam_init, F32)])
        gsub = g_subln_c[l].reshape(1, 2 * HEAD_DIM)
        g64 = jnp.ones((1, HEAD_DIM), F32)
        ya = _attention(scal, g64, qa, kta, va, heads=A_HEADS // A_KV, shared_k=True, mode="global", n_ctx=n_ctx)
        yb = _attention(scal, g64, qb, ktb, vb, heads=B_HEADS // B_KV, shared_k=True, mode="window", n_ctx=n_ctx)
        yc = _attention(scal_c, gsub, qc, ktc, vc, heads=2, shared_k=False, mode="diff", n_ctx=n_ctx)

        x1 = _merge(xs, mod, g1, _tokens_major(ya), _tokens_major(yb), _tokens_major(yc),
                    w_gate[l].astype(BF16), b_gate[l].reshape(1, -1), w_br_a[l].astype(BF16),
                    w_br_b[l].astype(BF16), w_br_c[l].astype(BF16), w_out[l].astype(BF16), n_ctx_tiles)

        f, idx_t, w_t, x_sh = _route(x1, mod, g_norm2[l].reshape(1, d), w_router[l].T,
                                     e_bias[l].reshape(N_EXPERTS, 1), ws1[l].astype(BF16),
                                     ws3[l].astype(BF16), ws2[l].astype(BF16), n_ctx_tiles)

        idx = idx_t.transpose(0, 2, 1).reshape(n_all, TOP_K)
        iexp, iblk, iflag, nitems, row_tok, row_dst, erow = _routing_tables(idx, n_all)
        y = _experts(iexp, iblk, iflag, nitems, row_tok, row_dst, erow, f.reshape(n_all, d),
                     w1[l].astype(BF16), w3[l].astype(BF16), w2[l].astype(BF16))
        xs = _combine(x_sh, mod, w_t.transpose(0, 2, 1), y, n_ctx_tiles)

    return _final_norm(xs, g_final.reshape(1, d), n_ctx_tiles)
```

```python
import functools
import math

import numpy as np
import jax
import jax.numpy as jnp
from jax import lax
from jax.experimental import pallas as pl
from jax.experimental.pallas import tpu as pltpu

F32 = jnp.float32
BF16 = jnp.bfloat16

HEAD_DIM = 64
ROPE_FREQS = HEAD_DIM // 4
ROPE_THETA = 10000.0
GRID_W = 64
WINDOW = 128
A_HEADS, A_KV = 8, 2
B_HEADS, B_KV = 8, 2
C_HEADS = 4
N_EXPERTS = 128
TOP_K = 8
N_GROUPS = 8
TOPK_GROUPS = 4
GROUP_SIZE = N_EXPERTS // N_GROUPS
ROUTE_SCALE = 2.5
EXPERT_BLOCK = 128
N_MOD = 6
EPS = 1e-6
NEG = -1e30

R_A = A_HEADS * HEAD_DIM + A_KV * HEAD_DIM
R_B = B_HEADS * HEAD_DIM + B_KV * HEAD_DIM
R_C = 4 * C_HEADS * HEAD_DIM
R_W = R_A + R_B + R_C
V_W = A_KV * HEAD_DIM + B_KV * HEAD_DIM + C_HEADS * 2 * HEAD_DIM

TOK_TILE = 256
KEY_CHUNK = 512
MAX_KEY_CHUNK = {"global": 8448, "diff": 4224, "window": 128}
LOG2E = math.log2(math.e)
VMEM_LIMIT = 56 * 1024 * 1024


def _cparams(sem, **kw):
    return pltpu.CompilerParams(dimension_semantics=sem, vmem_limit_bytes=VMEM_LIMIT, **kw)


def _dot(a, b):
    return jnp.dot(a, b, preferred_element_type=F32)


def _sigmoid(x):
    return 1.0 / (1.0 + jnp.exp(-x))


def _silu(x):
    return x * _sigmoid(x)


def _rms_rows(x, g):
    return x * lax.rsqrt(jnp.mean(x * x, axis=-1, keepdims=True) + EPS) * g


def _mod_kernel(c_ref, w_ref, b_ref, o_ref):
    cs = _silu(c_ref[...])
    o_ref[0] = jnp.dot(cs, w_ref[0], preferred_element_type=F32,
                       precision=lax.Precision.HIGHEST) + b_ref[0]


def _modulation(cvec, w_mod, b_mod):
    depth, d, n = w_mod.shape
    rows = cvec.shape[0]
    bn = 1536
    return pl.pallas_call(
        _mod_kernel,
        grid=(depth, n // bn),
        in_specs=[
            pl.BlockSpec((rows, d), lambda l, j: (0, 0)),
            pl.BlockSpec((1, d, bn), lambda l, j: (l, 0, j)),
            pl.BlockSpec((1, 1, bn), lambda l, j: (l, 0, j)),
        ],
        out_specs=pl.BlockSpec((1, rows, bn), lambda l, j: (l, 0, j)),
        out_shape=jax.ShapeDtypeStruct((depth, rows, n), F32),
        compiler_params=_cparams(("parallel", "parallel")),
        name="modulation",
    )(cvec, w_mod, b_mod.reshape(depth, 1, n))


def _pre_kernel(x_ref, mod_ref, g1_ref, wqkv_ref, wsw_ref, ones_ref, grow_ref, gsrow_ref,
                cos_ref, sin_ref, r_out, v_out, *, d):
    x = x_ref[0]
    mod = mod_ref[0]
    a = (_rms_rows(x, g1_ref[...]) * (1.0 + mod[:, d:2 * d]) + mod[:, 0:d]).astype(BF16)
    p = _dot(a, wqkv_ref[...])
    ps = _dot(a, wsw_ref[...])
    pa = p[:, :R_A]
    sq = pa * pa
    hi = sq.astype(BF16)
    lo = (sq - hi.astype(F32)).astype(BF16)
    ssq = _dot(hi, ones_ref[...]) + _dot(lo, ones_ref[...])
    rinv = lax.rsqrt(ssq * (1.0 / HEAD_DIM) + EPS)
    cos = cos_ref[...]
    sin = sin_ref[...]
    n_tiles = R_W // 128
    for j in range(n_tiles):
        sl = slice(j * 128, (j + 1) * 128)
        o = p[:, sl] * (grow_ref[:, sl] * cos) + ps[:, sl] * (gsrow_ref[:, sl] * sin)
        if (j + 1) * 128 <= R_A:
            o = o * rinv[:, sl]
        r_out[0, :, sl] = o.astype(BF16)
    v_out[0] = p[:, R_W:].astype(BF16)


def _pre_attention(x, mod, g1, wqkv, wsw, ones_blk, grow, gsrow, cos_t, sin_t, n_ctx_tiles):
    b, n, d = x.shape
    tn = TOK_TILE
    nb = b

    def mod_idx(bi, t):
        return (jnp.where(t < n_ctx_tiles, nb, bi), 0, 0)

    return pl.pallas_call(
        functools.partial(_pre_kernel, d=d),
        grid=(b, n // tn),
        in_specs=[
            pl.BlockSpec((1, tn, d), lambda bi, t: (bi, t, 0)),
            pl.BlockSpec((1, 1, N_MOD * d), mod_idx),
            pl.BlockSpec((1, d), lambda bi, t: (0, 0)),
            pl.BlockSpec((d, R_W + V_W), lambda bi, t: (0, 0)),
            pl.BlockSpec((d, R_W), lambda bi, t: (0, 0)),
            pl.BlockSpec((R_A, R_A), lambda bi, t: (0, 0)),
            pl.BlockSpec((1, R_W), lambda bi, t: (0, 0)),
            pl.BlockSpec((1, R_W), lambda bi, t: (0, 0)),
            pl.BlockSpec((tn, 128), lambda bi, t: (t, 0)),
            pl.BlockSpec((tn, 128), lambda bi, t: (t, 0)),
        ],
        out_specs=[
            pl.BlockSpec((1, tn, R_W), lambda bi, t: (bi, t, 0)),
            pl.BlockSpec((1, tn, V_W), lambda bi, t: (bi, t, 0)),
        ],
        out_shape=[
            jax.ShapeDtypeStruct((b, n, R_W), BF16),
            jax.ShapeDtypeStruct((b, n, V_W), BF16),
        ],
        compiler_params=_cparams(("parallel", "parallel")),
        name="pre_attention",
    )(x, mod, g1, wqkv, wsw, ones_blk, grow, gsrow, cos_t, sin_t)


def _attn_kernel(sc_ref, gsub_ref, q_ref, kt_ref, v_ref, o_ref, m_sc, acc_sc, *, heads,
                 shared_k, mode, n_ctx, n_tok, tq, ck, dv):
    u = pl.program_id(1)
    qi = pl.program_id(2)
    is_lat = qi >= n_ctx // tq
    dvx = acc_sc.shape[-1]

    for g in range(heads):
        if mode == "window":
            m_sc[g] = jnp.full((tq, 1), sc_ref[u * heads + g], F32)
            lane = lax.broadcasted_iota(jnp.int32, (tq, dvx), 1)
            acc_sc[g] = jnp.where(lane == dv, 1.0, 0.0).astype(F32)
        else:
            m_sc[g] = jnp.full((tq, 1), NEG, F32)
            acc_sc[g] = jnp.zeros((tq, dvx), F32)

    def update(g, kt, v, mask=None):
        s = _dot(q_ref[0, g], kt)
        if mask is not None:
            s = jnp.where(mask, s, NEG)
        m = m_sc[g]
        m_new = jnp.maximum(m, jnp.max(s, axis=-1, keepdims=True))
        p = jnp.exp2(s - m_new).astype(BF16)
        acc_sc[g] = jnp.exp2(m - m_new) * acc_sc[g] + _dot(p, v)
        m_sc[g] = m_new

    def step(k0, size):
        v = v_ref[0, 0, pl.ds(k0, size), :]
        for g in range(heads):
            update(g, kt_ref[0, 0 if shared_k else g, :, pl.ds(k0, size)], v)

    if mode == "window":
        span = tq + 2 * WINDOW
        start = pl.multiple_of(jnp.clip(qi * tq - WINDOW, 0, n_tok - span), 128)
        qpos = qi * tq + lax.broadcasted_iota(jnp.int32, (tq, n_ctx + span), 0)
        col = lax.broadcasted_iota(jnp.int32, (tq, n_ctx + span), 1)
        kpos = start + col - n_ctx
        mask = (col < n_ctx) | (is_lat & (kpos >= n_ctx) & (jnp.abs(kpos - qpos) <= WINDOW))
        v = jnp.concatenate([v_ref[0, 0, 0:n_ctx, :], v_ref[0, 0, pl.ds(start, span), :]], axis=0)
        for g in range(heads):
            gk = 0 if shared_k else g
            kt = jnp.concatenate([kt_ref[0, gk, :, 0:n_ctx], kt_ref[0, gk, :, pl.ds(start, span)]], axis=1)
            update(g, kt, v, mask)
    else:
        @pl.when(jnp.logical_not(is_lat))
        def _():
            step(0, n_ctx)

        def body(ci, carry):
            step(pl.multiple_of(ci * ck, 128), ck)
            return carry

        lax.fori_loop(0, jnp.where(is_lat, n_tok // ck, 0), body, 0)

    def result(g):
        acc = acc_sc[g]
        return acc[:, 0:dv] / acc[:, dv:dv + 1]

    if mode == "diff":
        lam = sc_ref[heads]
        post = sc_ref[heads + 1]
        y = result(0) - lam * result(1)
        o_ref[0, 0] = (_rms_rows(y, gsub_ref[...]) * post).astype(o_ref.dtype)
    else:
        for g in range(heads):
            o_ref[0, g] = result(g).astype(o_ref.dtype)


def _attention(scalars, gsub, q, kt, v, *, heads, shared_k, mode, n_ctx):
    b, hq, n, hd = q.shape
    units = hq // heads
    dv = v.shape[-1]
    dvx = (dv // 128 + 1) * 128
    tq = TOK_TILE
    gk = 1 if shared_k else heads
    out_heads = 1 if mode == "diff" else heads
    ck = max(c for c in range(128, MAX_KEY_CHUNK[mode] + 1, 128) if n % c == 0)
    ones_col = (lax.broadcasted_iota(jnp.int32, v.shape[:-1] + (dvx - dv,), v.ndim - 1) == 0).astype(v.dtype)
    vx = jnp.concatenate([v, ones_col], axis=-1)
    kern = functools.partial(_attn_kernel, heads=heads, shared_k=shared_k, mode=mode, n_ctx=n_ctx,
                             n_tok=n, tq=tq, ck=ck, dv=dv)
    return pl.pallas_call(
        kern,
        grid=(b, units, n // tq),
        in_specs=[
            pl.BlockSpec(memory_space=pltpu.SMEM),
            pl.BlockSpec((1, dv), lambda bi, u, t: (0, 0)),
            pl.BlockSpec((1, heads, tq, hd), lambda bi, u, t: (bi, u, t, 0)),
            pl.BlockSpec((1, gk, hd, n), lambda bi, u, t: (bi, u, 0, 0)),
            pl.BlockSpec((1, 1, n, dvx), lambda bi, u, t: (bi, u, 0, 0)),
        ],
        out_specs=pl.BlockSpec((1, out_heads, tq, dv), lambda bi, u, t: (bi, u, t, 0)),
        out_shape=jax.ShapeDtypeStruct((b, units * out_heads, n, dv), BF16),
        scratch_shapes=[
            pltpu.VMEM((heads, tq, 1), F32),
            pltpu.VMEM((heads, tq, dvx), F32),
        ],
        compiler_params=_cparams(("parallel", "parallel", "arbitrary")),
        name="attn_" + mode,
    )(scalars, gsub, q, kt, vx)


def _merge_kernel(x_ref, mod_ref, g1_ref, ya_ref, yb_ref, yc_ref, wg_ref, bg_ref, wa_ref, wb_ref,
                  wc_ref, wo_ref, o_ref, *, d):
    x = x_ref[0]
    mod = mod_ref[0]
    a = (_rms_rows(x, g1_ref[...]) * (1.0 + mod[:, d:2 * d]) + mod[:, 0:d]).astype(BF16)
    gate = _sigmoid(_dot(a, wg_ref[...]) + bg_ref[...])
    m = (gate[:, 0:d] * _dot(ya_ref[0], wa_ref[...])
         + gate[:, d:2 * d] * _dot(yb_ref[0], wb_ref[...])
         + gate[:, 2 * d:3 * d] * _dot(yc_ref[0], wc_ref[...]))
    mix = _dot(m.astype(BF16), wo_ref[...])
    o_ref[0] = x + mod[:, 2 * d:3 * d] * mix


def _merge(x, mod, g1, ya, yb, yc, wg, bg, wa, wb, wc, wo, n_ctx_tiles):
    b, n, d = x.shape
    tn = TOK_TILE
    nb = b
    yw = ya.shape[-1]

    def mod_idx(bi, t):
        return (jnp.where(t < n_ctx_tiles, nb, bi), 0, 0)

    tok = lambda bi, t: (bi, t, 0)
    const = lambda bi, t: (0, 0)
    return pl.pallas_call(
        functools.partial(_merge_kernel, d=d),
        grid=(b, n // tn),
        in_specs=[
            pl.BlockSpec((1, tn, d), tok),
            pl.BlockSpec((1, 1, N_MOD * d), mod_idx),
            pl.BlockSpec((1, d), const),
            pl.BlockSpec((1, tn, yw), tok),
            pl.BlockSpec((1, tn, yw), tok),
            pl.BlockSpec((1, tn, yw), tok),
            pl.BlockSpec((d, 3 * d), const),
            pl.BlockSpec((1, 3 * d), const),
            pl.BlockSpec((yw, d), const),
            pl.BlockSpec((yw, d), const),
            pl.BlockSpec((yw, d), const),
            pl.BlockSpec((d, d), const),
        ],
        out_specs=pl.BlockSpec((1, tn, d), tok),
        out_shape=jax.ShapeDtypeStruct((b, n, d), F32),
        compiler_params=_cparams(("parallel", "parallel")),
        name="merge",
    )(x, mod, g1, ya, yb, yc, wg, bg, wa, wb, wc, wo)


def _first_index(hit, idx, big):
    return jnp.min(jnp.where(hit, idx, big), axis=0, keepdims=True)


def _route_kernel(x_ref, mod_ref, g2_ref, wrt_ref, eb_ref, ws1_ref, ws3_ref, ws2_ref,
                  f_ref, idx_ref, w_ref, xs_ref, *, d):
    x = x_ref[0]
    mod = mod_ref[0]
    f = _rms_rows(x, g2_ref[...]) * (1.0 + mod[:, 4 * d:5 * d]) + mod[:, 3 * d:4 * d]
    tn = f.shape[0]
    for s in range(d // 128):
        f_ref[0, pl.ds(s, tn, stride=d // 128), :] = f[:, s * 128:(s + 1) * 128]
    logits = lax.dot_general(wrt_ref[...], f, (((1,), (1,)), ((), ())),
                             preferred_element_type=F32, precision=lax.Precision.HIGHEST)
    scores = _sigmoid(logits)
    choice = scores + eb_ref[...]
    eidx = lax.broadcasted_iota(jnp.int32, (N_EXPERTS, tn), 0)
    lidx = lax.broadcasted_iota(jnp.int32, (GROUP_SIZE, tn), 0)
    gscore = []
    for g in range(N_GROUPS):
        cg = choice[g * GROUP_SIZE:(g + 1) * GROUP_SIZE, :]
        m1 = jnp.max(cg, axis=0, keepdims=True)
        first = _first_index(cg == m1, lidx, GROUP_SIZE)
        m2 = jnp.max(jnp.where(lidx == first, NEG, cg), axis=0, keepdims=True)
        gscore.append(m1 + m2)
    gs = jnp.concatenate(gscore, axis=0)
    gidx = lax.broadcasted_iota(jnp.int32, (N_GROUPS, tn), 0)
    gsel = jnp.zeros((N_GROUPS, tn), jnp.bool_)
    for _ in range(TOPK_GROUPS):
        gm = jnp.max(gs, axis=0, keepdims=True)
        first = _first_index(gs == gm, gidx, N_GROUPS)
        hit = gidx == first
        gsel = gsel | hit
        gs = jnp.where(hit, NEG, gs)
    gself = gsel.astype(F32)
    emask = jnp.concatenate(
        [jnp.broadcast_to(gself[g:g + 1, :], (GROUP_SIZE, tn)) for g in range(N_GROUPS)], axis=0)
    cur = jnp.where(emask > 0.5, choice, NEG)
    ids, ws = [], []
    for _ in range(TOP_K):
        m = jnp.max(cur, axis=0, keepdims=True)
        first = _first_index(cur == m, eidx, N_EXPERTS)
        hit = eidx == first
        ids.append(first)
        ws.append(jnp.sum(jnp.where(hit, scores, 0.0), axis=0, keepdims=True))
        cur = jnp.where(hit, NEG, cur)
    wsel = jnp.concatenate(ws, axis=0)
    idx_ref[0] = jnp.concatenate(ids, axis=0)
    w_ref[0] = wsel / jnp.sum(wsel, axis=0, keepdims=True) * ROUTE_SCALE
    fb = f.astype(BF16)
    h = _silu(_dot(fb, ws1_ref[...])) * _dot(fb, ws3_ref[...])
    xs_ref[0] = x + mod[:, 5 * d:6 * d] * _dot(h.astype(BF16), ws2_ref[...])


def _route(x, mod, g2, wrt, eb, ws1, ws3, ws2, n_ctx_tiles):
    b, n, d = x.shape
    tn = TOK_TILE
    nb = b
    ds = ws1.shape[-1]

    def mod_idx(bi, t):
        return (jnp.where(t < n_ctx_tiles, nb, bi), 0, 0)

    tok = lambda bi, t: (bi, t, 0)
    lane_tok = lambda bi, t: (bi, 0, t)
    const = lambda bi, t: (0, 0)
    return pl.pallas_call(
        functools.partial(_route_kernel, d=d),
        grid=(b, n // tn),
        in_specs=[
            pl.BlockSpec((1, tn, d), tok),
            pl.BlockSpec((1, 1, N_MOD * d), mod_idx),
            pl.BlockSpec((1, d), const),
            pl.BlockSpec((N_EXPERTS, d), const),
            pl.BlockSpec((N_EXPERTS, 1), const),
            pl.BlockSpec((d, ds), const),
            pl.BlockSpec((d, ds), const),
            pl.BlockSpec((ds, d), const),
        ],
        out_specs=[
            pl.BlockSpec((1, tn * (d // 128), 128), tok),
            pl.BlockSpec((1, TOP_K, tn), lane_tok),
            pl.BlockSpec((1, TOP_K, tn), lane_tok),
            pl.BlockSpec((1, tn, d), tok),
        ],
        out_shape=[
            jax.ShapeDtypeStruct((b, n * (d // 128), 128), F32),
            jax.ShapeDtypeStruct((b, TOP_K, n), jnp.int32),
            jax.ShapeDtypeStruct((b, TOP_K, n), F32),
            jax.ShapeDtypeStruct((b, n, d), F32),
        ],
        compiler_params=_cparams(("parallel", "parallel")),
        name="route_shared",
    )(x, mod, g2, wrt, eb, ws1, ws3, ws2)


def _expert_kernel(iexp_ref, iblk_ref, iflag_ref, nitems_ref, tok_ref, tok_next_ref, dst_prev_ref,
                   dst_ref, erow_ref, f_hbm, w1_ref, w3_ref, w2_ref, y_hbm, xbuf, ybuf, wb1, wb3, wb2,
                   gsem, ssem):
    i = pl.program_id(0)
    nitems = nitems_ref[0]
    blk = iblk_ref[i]
    expert = iexp_ref[i]
    first = (iflag_ref[i] & 1) != 0
    last = (iflag_ref[i] & 2) != 0
    slot = blk % 2
    rows = EXPERT_BLOCK
    parts = xbuf.shape[1] // rows
    n_blocks = y_hbm.shape[0] // (rows * parts)

    def gather(idx_ref, s):
        for j in range(rows):
            src = pl.multiple_of(idx_ref[0, 0, j], parts)
            pltpu.make_async_copy(f_hbm.at[pl.ds(src, parts)], xbuf.at[s, pl.ds(j * parts, parts)],
                                  gsem.at[s]).start(priority=j % 2)

    def gather_wait(s):
        pltpu.make_async_copy(f_hbm.at[pl.ds(0, rows * parts)], xbuf.at[s], gsem.at[s]).wait()

    def scatter(idx_ref, s):
        for j in range(rows):
            dst = pl.multiple_of(idx_ref[0, 0, j], parts)
            pltpu.make_async_copy(ybuf.at[s, pl.ds(j * parts, parts)], y_hbm.at[pl.ds(dst, parts)],
                                  ssem.at[s]).start(priority=j % 2)

    def scatter_wait(s):
        pltpu.make_async_copy(ybuf.at[s], y_hbm.at[pl.ds(0, rows * parts)], ssem.at[s]).wait()

    def expert_rows(s, after_up=None, after_down=None):
        xb = jnp.concatenate([xbuf[s, pl.ds(p, rows, stride=parts), :] for p in range(parts)],
                             axis=1).astype(BF16)
        h1 = _dot(xb, wb1[...])
        h3 = _dot(xb, wb3[...])
        if after_up is not None:
            after_up()
        y = _dot((_silu(h1) * h3).astype(BF16), wb2[...])
        if after_down is not None:
            after_down()
        return y

    def store_rows(s, y, accumulate):
        for p in range(parts):
            part = y[:, p * 128:(p + 1) * 128]
            if accumulate:
                part = part + ybuf[s, pl.ds(p, rows, stride=parts), :]
            ybuf[s, pl.ds(p, rows, stride=parts), :] = part

    valid = i < nitems

    @pl.when(valid & ((iflag_ref[i] & 4) != 0))
    def _():
        wb1[...] = w1_ref[0].astype(BF16)
        wb3[...] = w3_ref[0].astype(BF16)
        wb2[...] = w2_ref[0].astype(BF16)

    fast = valid & first & last & (blk >= 2) & (blk <= n_blocks - 2)

    @pl.when(fast)
    def _():
        gather_wait(slot)
        scatter_wait(slot)
        y = expert_rows(slot, after_up=lambda: gather(tok_next_ref, 1 - slot),
                        after_down=lambda: scatter(dst_prev_ref, 1 - slot))
        store_rows(slot, y, False)

    @pl.when(valid & jnp.logical_not(fast))
    def _():
        @pl.when(first)
        def _():
            @pl.when(blk == 0)
            def _():
                gather(tok_ref, 0)

            @pl.when(blk + 1 < n_blocks)
            def _():
                gather(tok_next_ref, 1 - slot)

            @pl.when(blk >= 1)
            def _():
                scatter(dst_prev_ref, 1 - slot)

            gather_wait(slot)

            @pl.when(blk >= 2)
            def _():
                scatter_wait(slot)

        y = expert_rows(slot) * (erow_ref[0] == expert).astype(F32)

        @pl.when(first)
        def _():
            store_rows(slot, y, False)

        @pl.when(jnp.logical_not(first))
        def _():
            store_rows(slot, y, True)

        @pl.when(last & (blk == n_blocks - 1))
        def _():
            scatter(dst_ref, slot)

            @pl.when(blk >= 1)
            def _():
                scatter_wait(1 - slot)

            scatter_wait(slot)


def _experts(iexp, iblk, iflag, nitems, row_tok, row_dst, erow, f_rows, w1, w3, w2):
    n_items = iexp.shape[0]
    n_blocks = row_tok.shape[0]
    d, de = w1.shape[1], w1.shape[2]
    parts = d // 128
    rows = EXPERT_BLOCK
    last = n_blocks - 1
    cur = lambda i, ie, ib, fl, nt: (ib[i], 0, 0)
    nxt = lambda i, ie, ib, fl, nt: (jnp.minimum(ib[i] + 1, last), 0, 0)
    prv = lambda i, ie, ib, fl, nt: (jnp.maximum(ib[i] - 1, 0), 0, 0)
    wsel = lambda i, ie, ib, fl, nt: (ie[i], 0, 0)
    grid_spec = pltpu.PrefetchScalarGridSpec(
        num_scalar_prefetch=4,
        grid=(n_items,),
        in_specs=[
            pl.BlockSpec((1, 1, rows), cur, memory_space=pltpu.SMEM),
            pl.BlockSpec((1, 1, rows), nxt, memory_space=pltpu.SMEM),
            pl.BlockSpec((1, 1, rows), prv, memory_space=pltpu.SMEM),
            pl.BlockSpec((1, 1, rows), cur, memory_space=pltpu.SMEM),
            pl.BlockSpec((1, rows, 1), cur),
            pl.BlockSpec(memory_space=pl.ANY),
            pl.BlockSpec((1, d, de), wsel),
            pl.BlockSpec((1, d, de), wsel),
            pl.BlockSpec((1, de, d), wsel),
        ],
        out_specs=pl.BlockSpec(memory_space=pl.ANY),
        scratch_shapes=[
            pltpu.VMEM((2, rows * parts, 128), F32),
            pltpu.VMEM((2, rows * parts, 128), F32),
            pltpu.VMEM((d, de), BF16),
            pltpu.VMEM((d, de), BF16),
            pltpu.VMEM((de, d), BF16),
            pltpu.SemaphoreType.DMA((2,)),
            pltpu.SemaphoreType.DMA((2,)),
        ],
    )
    return pl.pallas_call(
        _expert_kernel,
        grid_spec=grid_spec,
        out_shape=jax.ShapeDtypeStruct((n_blocks * rows * parts, 128), F32),
        compiler_params=_cparams(("arbitrary",)),
        name="experts",
    )(iexp, iblk, iflag, nitems, row_tok, row_tok, row_dst, row_dst, erow, f_rows, w1, w3, w2)


ASSIGN_BITS = 20


def _routing_tables(idx, n_tokens, parts):
    n_assign = n_tokens * TOP_K
    blk = EXPERT_BLOCK
    assert n_assign % blk == 0 and n_assign <= (1 << ASSIGN_BITS)
    n_blocks = n_assign // blk
    n_items = n_blocks + N_EXPERTS
    flat_e = idx.reshape(-1).astype(jnp.int32)
    key = jnp.sort((flat_e << ASSIGN_BITS) | jnp.arange(n_assign, dtype=jnp.int32))
    e_sorted = key >> ASSIGN_BITS
    order = key & ((1 << ASSIGN_BITS) - 1)
    tok = order // TOP_K
    row_dst = ((order % TOP_K) * n_tokens + tok) * parts
    tok = tok * parts

    ends = jnp.searchsorted(e_sorted, jnp.arange(1, N_EXPERTS + 1, dtype=jnp.int32), side="left").astype(jnp.int32)
    starts = jnp.concatenate([jnp.zeros((1,), jnp.int32), ends[:-1]])
    first_blk = starts // blk
    n_be = jnp.where(ends > starts, (ends - 1) // blk - first_blk + 1, 0)
    item_end = jnp.cumsum(n_be)
    item_off = item_end - n_be
    total = item_end[-1]
    i = jnp.arange(n_items, dtype=jnp.int32)
    iexp = jnp.minimum(jnp.searchsorted(item_end, i, side="right"), N_EXPERTS - 1).astype(jnp.int32)
    iblk = jnp.where(i < total, first_blk[iexp] + i - item_off[iexp], n_blocks - 1).astype(jnp.int32)
    prev_blk = jnp.concatenate([jnp.full((1,), -1, jnp.int32), iblk[:-1]])
    next_blk = jnp.concatenate([iblk[1:], jnp.full((1,), -1, jnp.int32)])
    is_first = iblk != prev_blk
    is_last = (iblk != next_blk) | (i == total - 1)
    new_expert = iexp != jnp.concatenate([jnp.full((1,), -1, jnp.int32), iexp[:-1]])
    iflag = is_first.astype(jnp.int32) + 2 * is_last.astype(jnp.int32) + 4 * new_expert.astype(jnp.int32)
    return (iexp, iblk, iflag, total.astype(jnp.int32).reshape(1), tok.reshape(n_blocks, 1, blk),
            row_dst.reshape(n_blocks, 1, blk), e_sorted.reshape(n_blocks, blk, 1))


def _combine_kernel(xs_ref, mod_ref, w_ref, *refs, d):
    y_refs, o_ref = refs[:TOP_K], refs[TOP_K]
    w = w_ref[0]
    tn = w.shape[0]
    parts = d // 128
    wk = [jnp.broadcast_to(w[:, k:k + 1], (tn, 128)) for k in range(TOP_K)]
    for p in range(parts):
        tot = y_refs[0][pl.ds(p, tn, stride=parts), :] * wk[0]
        for k in range(1, TOP_K):
            tot = tot + y_refs[k][pl.ds(p, tn, stride=parts), :] * wk[k]
        sl = slice(p * 128, (p + 1) * 128)
        o_ref[0, :, sl] = xs_ref[0, :, sl] + mod_ref[0][:, 5 * d + p * 128:5 * d + (p + 1) * 128] * tot


def _combine(xs, mod, w, y, n_ctx_tiles):
    b, n, d = xs.shape
    tn = TOK_TILE
    nb = b
    tiles = n // tn
    per_slot = b * tiles

    def mod_idx(bi, t):
        return (jnp.where(t < n_ctx_tiles, nb, bi), 0, 0)

    tok = lambda bi, t: (bi, t, 0)
    y_specs = [pl.BlockSpec((tn * (d // 128), 128),
                            functools.partial(lambda bi, t, k: (k * per_slot + bi * tiles + t, 0), k=k))
               for k in range(TOP_K)]
    return pl.pallas_call(
        functools.partial(_combine_kernel, d=d),
        grid=(b, tiles),
        in_specs=[pl.BlockSpec((1, tn, d), tok), pl.BlockSpec((1, 1, N_MOD * d), mod_idx),
                  pl.BlockSpec((1, tn, TOP_K), tok)] + y_specs,
        out_specs=pl.BlockSpec((1, tn, d), tok),
        out_shape=jax.ShapeDtypeStruct((b, n, d), F32),
        compiler_params=_cparams(("parallel", "parallel")),
        name="combine",
    )(xs, mod, w, *([y] * TOP_K))


def _final_kernel(x_ref, g_ref, o_ref):
    o_ref[0] = _rms_rows(x_ref[0], g_ref[...])


def _final_norm(x, g, n_ctx_tiles):
    b, n, d = x.shape
    tn = TOK_TILE
    n_lat_tiles = n // tn - n_ctx_tiles
    return pl.pallas_call(
        _final_kernel,
        grid=(b, n_lat_tiles),
        in_specs=[pl.BlockSpec((1, tn, d), lambda bi, t: (bi, t + n_ctx_tiles, 0)),
                  pl.BlockSpec((1, d), lambda bi, t: (0, 0))],
        out_specs=pl.BlockSpec((1, tn, d), lambda bi, t: (bi, t, 0)),
        out_shape=jax.ShapeDtypeStruct((b, n_lat_tiles * tn, d), F32),
        compiler_params=_cparams(("parallel", "parallel")),
        name="final_norm",
    )(x, g)


def _rotary_column_tables():
    a_q, a_kv = A_HEADS * HEAD_DIM, A_KV * HEAD_DIM
    b_q, b_kv = B_HEADS * HEAD_DIM, B_KV * HEAD_DIM
    c_qk, c_v = 2 * C_HEADS * HEAD_DIM, C_HEADS * 2 * HEAD_DIM
    sizes = (a_q, a_kv, a_kv, b_q, b_kv, b_kv, c_qk, c_qk, c_v)
    off = np.concatenate([[0], np.cumsum(sizes)])
    seg = lambda i: np.arange(off[i], off[i + 1])
    perm_r = np.concatenate([seg(0), seg(1), seg(3), seg(4), seg(6), seg(7)])
    perm_v = np.concatenate([seg(2), seg(5), seg(8)])
    col = np.arange(R_W)
    i = col % HEAD_DIM
    partner_local = np.where((i % (2 * ROPE_FREQS)) < ROPE_FREQS, i + ROPE_FREQS, i - ROPE_FREQS)
    partner = col - i + partner_local
    return perm_r, perm_v, partner


def _rope_tables(n_ctx, n_lat):
    t = jnp.arange(n_lat, dtype=jnp.int32)
    row_pos = (t // GRID_W).astype(F32)
    col_pos = (t % GRID_W).astype(F32)
    inv_freq = jnp.power(ROPE_THETA, -jnp.arange(ROPE_FREQS, dtype=F32) / ROPE_FREQS)
    ang_r = row_pos[:, None] * inv_freq
    ang_c = col_pos[:, None] * inv_freq
    cos64 = jnp.concatenate([jnp.cos(ang_r), jnp.cos(ang_r), jnp.cos(ang_c), jnp.cos(ang_c)], axis=1)
    sin64 = jnp.concatenate([-jnp.sin(ang_r), jnp.sin(ang_r), -jnp.sin(ang_c), jnp.sin(ang_c)], axis=1)
    cos64 = jnp.concatenate([jnp.ones((n_ctx, HEAD_DIM), F32), cos64], axis=0)
    sin64 = jnp.concatenate([jnp.zeros((n_ctx, HEAD_DIM), F32), sin64], axis=0)
    return jnp.tile(cos64, (1, 2)), jnp.tile(sin64, (1, 2))


def _heads_major(t, n_heads):
    b, n, w = t.shape
    return t.reshape(b, n, n_heads, w // n_heads).transpose(0, 2, 1, 3)


def _heads_transposed(t, n_heads):
    b, n, w = t.shape
    return t.reshape(b, n, n_heads, w // n_heads).transpose(0, 2, 3, 1)


def _tokens_major(t):
    b, h, n, dv = t.shape
    return t.transpose(0, 2, 1, 3).reshape(b, n, h * dv)


def kernel(x, c, ctx, c_ctx, w_mod, b_mod, g_norm1, w_qkv, g_qnorm_a, g_knorm_a, sink_b, lam_q1, lam_k1, lam_q2, lam_k2, g_subln_c, w_br_a, w_br_b, w_br_c, w_gate, b_gate, w_out, g_norm2, w_router, e_bias, w1, w3, w2, ws1, ws3, ws2, g_final):
    bsz, n_lat, d = x.shape
    n_ctx = ctx.shape[1]
    depth = w_mod.shape[0]
    n_tok = n_ctx + n_lat
    assert n_ctx % TOK_TILE == 0 and n_lat % KEY_CHUNK == 0 and n_ctx % 128 == 0
    n_ctx_tiles = n_ctx // TOK_TILE
    n_all = bsz * n_tok

    rows = -(-(bsz + 1) // 8) * 8
    cvec = jnp.concatenate([c, c_ctx[None, :], jnp.zeros((rows - bsz - 1, d), F32)], axis=0)
    mod_all = _modulation(cvec, w_mod, b_mod)

    perm_r, perm_v, partner = _rotary_column_tables()
    cos_t, sin_t = _rope_tables(n_ctx, n_lat)
    head_of = np.arange(R_A) // HEAD_DIM
    ones_blk = jnp.asarray((head_of[:, None] == head_of[None, :]).astype(np.float32), BF16)
    scale = HEAD_DIM ** -0.5 * LOG2E
    unit = jnp.ones((HEAD_DIM,), F32)

    xs = jnp.concatenate([ctx, x], axis=1)
    for l in range(depth):
        lam_init = 0.8 - 0.6 * math.exp(-0.3 * l)
        lam = (jnp.exp(jnp.dot(lam_q1[l], lam_k1[l])) - jnp.exp(jnp.dot(lam_q2[l], lam_k2[l]))).astype(F32) + lam_init
        mod = mod_all[l].reshape(rows, 1, N_MOD * d)
        g1 = g_norm1[l].reshape(1, d)

        w_r = w_qkv[l][:, perm_r]
        wqkv = jnp.concatenate([w_r, w_qkv[l][:, perm_v]], axis=1).astype(BF16)
        wsw = w_r[:, partner].astype(BF16)
        grow = jnp.concatenate([jnp.tile(g_qnorm_a[l] * scale, A_HEADS), jnp.tile(g_knorm_a[l], A_KV),
                                jnp.tile(unit * scale, B_HEADS), jnp.tile(unit, B_KV),
                                jnp.tile(unit * scale, 2 * C_HEADS), jnp.tile(unit, 2 * C_HEADS)])
        gsrow = grow[partner]
        r_all, v_all = _pre_attention(xs, mod, g1, wqkv, wsw, ones_blk, grow.reshape(1, R_W),
                                      gsrow.reshape(1, R_W), cos_t, sin_t, n_ctx_tiles)

        o = 0
        qa = _heads_major(r_all[..., o:o + A_HEADS * HEAD_DIM], A_HEADS); o += A_HEADS * HEAD_DIM
        kta = _heads_transposed(r_all[..., o:o + A_KV * HEAD_DIM], A_KV); o += A_KV * HEAD_DIM
        qb = _heads_major(r_all[..., o:o + B_HEADS * HEAD_DIM], B_HEADS); o += B_HEADS * HEAD_DIM
        ktb = _heads_transposed(r_all[..., o:o + B_KV * HEAD_DIM], B_KV); o += B_KV * HEAD_DIM
        qc = _heads_major(r_all[..., o:o + 2 * C_HEADS * HEAD_DIM], 2 * C_HEADS); o += 2 * C_HEADS * HEAD_DIM
        ktc = _heads_transposed(r_all[..., o:o + 2 * C_HEADS * HEAD_DIM], 2 * C_HEADS)
        o = 0
        va = _heads_major(v_all[..., o:o + A_KV * HEAD_DIM], A_KV); o += A_KV * HEAD_DIM
        vb = _heads_major(v_all[..., o:o + B_KV * HEAD_DIM], B_KV); o += B_KV * HEAD_DIM
        vc = _heads_major(v_all[..., o:], C_HEADS)

        scal = jnp.concatenate([sink_b[l].astype(F32) * LOG2E, lam.reshape(1), jnp.full((1,), 1.0 - lam_init, F32)])
        scal_c = jnp.concatenate([jnp.zeros((2,), F32), lam.reshape(1), jnp.full((1,), 1.0 - lam_init, F32)])
        gsub = g_subln_c[l].reshape(1, 2 * HEAD_DIM)
        g64 = jnp.ones((1, HEAD_DIM), F32)
        ya = _attention(scal, g64, qa, kta, va, heads=A_HEADS // A_KV, shared_k=True, mode="global", n_ctx=n_ctx)
        yb = _attention(scal, g64, qb, ktb, vb, heads=B_HEADS // B_KV, shared_k=True, mode="window", n_ctx=n_ctx)
        yc = _attention(scal_c, gsub, qc, ktc, vc, heads=2, shared_k=False, mode="diff", n_ctx=n_ctx)

        x1 = _merge(xs, mod, g1, _tokens_major(ya), _tokens_major(yb), _tokens_major(yc),
                    w_gate[l].astype(BF16), b_gate[l].reshape(1, -1), w_br_a[l].astype(BF16),
                    w_br_b[l].astype(BF16), w_br_c[l].astype(BF16), w_out[l].astype(BF16), n_ctx_tiles)

        f, idx_t, w_t, x_sh = _route(x1, mod, g_norm2[l].reshape(1, d), w_router[l].T,
                                     e_bias[l].reshape(N_EXPERTS, 1), ws1[l].astype(BF16),
                                     ws3[l].astype(BF16), ws2[l].astype(BF16), n_ctx_tiles)

        idx = idx_t.transpose(0, 2, 1).reshape(n_all, TOP_K)
        iexp, iblk, iflag, nitems, row_tok, row_dst, erow = _routing_tables(idx, n_all, d // 128)
        y = _experts(iexp, iblk, iflag, nitems, row_tok, row_dst, erow, f.reshape(n_all * (d // 128), 128),
                     w1[l], w3[l], w2[l])
        xs = _combine(x_sh, mod, w_t.transpose(0, 2, 1), y, n_ctx_tiles)

    return _final_norm(xs, g_final.reshape(1, d), n_ctx_tiles)
```

```python
import functools
import math

import numpy as np
import jax
import jax.numpy as jnp
from jax import lax
from jax.experimental import pallas as pl
from jax.experimental.pallas import tpu as pltpu

F32 = jnp.float32
BF16 = jnp.bfloat16

HEAD_DIM = 64
ROPE_FREQS = HEAD_DIM // 4
ROPE_THETA = 10000.0
GRID_W = 64
WINDOW = 128
A_HEADS, A_KV = 8, 2
B_HEADS, B_KV = 8, 2
C_HEADS = 4
N_EXPERTS = 128
TOP_K = 8
N_GROUPS = 8
TOPK_GROUPS = 4
GROUP_SIZE = N_EXPERTS // N_GROUPS
ROUTE_SCALE = 2.5
EXPERT_BLOCK = 128
EXPERT_RING = 3
N_MOD = 6
EPS = 1e-6
NEG = -1e30

R_A = A_HEADS * HEAD_DIM + A_KV * HEAD_DIM
R_B = B_HEADS * HEAD_DIM + B_KV * HEAD_DIM
R_C = 4 * C_HEADS * HEAD_DIM
R_W = R_A + R_B + R_C
V_W = A_KV * HEAD_DIM + B_KV * HEAD_DIM + C_HEADS * 2 * HEAD_DIM

TOK_TILE = 256
KEY_CHUNK = 512
MAX_KEY_CHUNK = {"global": 8448, "diff": 4224, "window": 128}
LOG2E = math.log2(math.e)
VMEM_LIMIT = 56 * 1024 * 1024


def _cparams(sem, **kw):
    return pltpu.CompilerParams(dimension_semantics=sem, vmem_limit_bytes=VMEM_LIMIT, **kw)


def _dot(a, b):
    return jnp.dot(a, b, preferred_element_type=F32)


def _sigmoid(x):
    return 1.0 / (1.0 + jnp.exp(-x))


def _silu(x):
    return x * _sigmoid(x)


def _rms_rows(x, g):
    return x * lax.rsqrt(jnp.mean(x * x, axis=-1, keepdims=True) + EPS) * g


def _mod_kernel(c_ref, w_ref, b_ref, o_ref):
    cs = _silu(c_ref[...])
    o_ref[0] = jnp.dot(cs, w_ref[0], preferred_element_type=F32,
                       precision=lax.Precision.HIGHEST) + b_ref[0]


def _modulation(cvec, w_mod, b_mod):
    depth, d, n = w_mod.shape
    rows = cvec.shape[0]
    bn = 1536
    return pl.pallas_call(
        _mod_kernel,
        grid=(depth, n // bn),
        in_specs=[
            pl.BlockSpec((rows, d), lambda l, j: (0, 0)),
            pl.BlockSpec((1, d, bn), lambda l, j: (l, 0, j)),
            pl.BlockSpec((1, 1, bn), lambda l, j: (l, 0, j)),
        ],
        out_specs=pl.BlockSpec((1, rows, bn), lambda l, j: (l, 0, j)),
        out_shape=jax.ShapeDtypeStruct((depth, rows, n), F32),
        compiler_params=_cparams(("parallel", "parallel")),
        name="modulation",
    )(cvec, w_mod, b_mod.reshape(depth, 1, n))


def _pre_kernel(x_ref, mod_ref, g1_ref, wqkv_ref, wsw_ref, ones_ref, grow_ref, gsrow_ref,
                cos_ref, sin_ref, r_out, v_out, *, d):
    x = x_ref[0]
    mod = mod_ref[0]
    a = (_rms_rows(x, g1_ref[...]) * (1.0 + mod[:, d:2 * d]) + mod[:, 0:d]).astype(BF16)
    p = _dot(a, wqkv_ref[...])
    ps = _dot(a, wsw_ref[...])
    pa = p[:, :R_A]
    sq = pa * pa
    hi = sq.astype(BF16)
    lo = (sq - hi.astype(F32)).astype(BF16)
    ssq = _dot(hi, ones_ref[...]) + _dot(lo, ones_ref[...])
    rinv = lax.rsqrt(ssq * (1.0 / HEAD_DIM) + EPS)
    cos = cos_ref[...]
    sin = sin_ref[...]
    n_tiles = R_W // 128
    for j in range(n_tiles):
        sl = slice(j * 128, (j + 1) * 128)
        o = p[:, sl] * (grow_ref[:, sl] * cos) + ps[:, sl] * (gsrow_ref[:, sl] * sin)
        if (j + 1) * 128 <= R_A:
            o = o * rinv[:, sl]
        r_out[0, :, sl] = o.astype(BF16)
    v_out[0] = p[:, R_W:].astype(BF16)


def _pre_attention(x, mod, g1, wqkv, wsw, ones_blk, grow, gsrow, cos_t, sin_t, n_ctx_tiles):
    b, n, d = x.shape
    tn = TOK_TILE
    nb = b

    def mod_idx(bi, t):
        return (jnp.where(t < n_ctx_tiles, nb, bi), 0, 0)

    return pl.pallas_call(
        functools.partial(_pre_kernel, d=d),
        grid=(b, n // tn),
        in_specs=[
            pl.BlockSpec((1, tn, d), lambda bi, t: (bi, t, 0)),
            pl.BlockSpec((1, 1, N_MOD * d), mod_idx),
            pl.BlockSpec((1, d), lambda bi, t: (0, 0)),
            pl.BlockSpec((d, R_W + V_W), lambda bi, t: (0, 0)),
            pl.BlockSpec((d, R_W), lambda bi, t: (0, 0)),
            pl.BlockSpec((R_A, R_A), lambda bi, t: (0, 0)),
            pl.BlockSpec((1, R_W), lambda bi, t: (0, 0)),
            pl.BlockSpec((1, R_W), lambda bi, t: (0, 0)),
            pl.BlockSpec((tn, 128), lambda bi, t: (t, 0)),
            pl.BlockSpec((tn, 128), lambda bi, t: (t, 0)),
        ],
        out_specs=[
            pl.BlockSpec((1, tn, R_W), lambda bi, t: (bi, t, 0)),
            pl.BlockSpec((1, tn, V_W), lambda bi, t: (bi, t, 0)),
        ],
        out_shape=[
            jax.ShapeDtypeStruct((b, n, R_W), BF16),
            jax.ShapeDtypeStruct((b, n, V_W), BF16),
        ],
        compiler_params=_cparams(("parallel", "parallel")),
        name="pre_attention",
    )(x, mod, g1, wqkv, wsw, ones_blk, grow, gsrow, cos_t, sin_t)


def _attn_kernel(sc_ref, gsub_ref, q_ref, kt_ref, v_ref, o_ref, m_sc, acc_sc, *, heads,
                 shared_k, mode, n_ctx, n_tok, tq, ck, dv):
    u = pl.program_id(1)
    qi = pl.program_id(2)
    is_lat = qi >= n_ctx // tq
    dvx = acc_sc.shape[-1]

    for g in range(heads):
        if mode == "window":
            m_sc[g] = jnp.full((tq, 1), sc_ref[u * heads + g], F32)
            lane = lax.broadcasted_iota(jnp.int32, (tq, dvx), 1)
            acc_sc[g] = jnp.where(lane == dv, 1.0, 0.0).astype(F32)
        else:
            m_sc[g] = jnp.full((tq, 1), NEG, F32)
            acc_sc[g] = jnp.zeros((tq, dvx), F32)

    def update(g, kt, v, mask=None):
        s = _dot(q_ref[0, g], kt)
        if mask is not None:
            s = jnp.where(mask, s, NEG)
        m = m_sc[g]
        m_new = jnp.maximum(m, jnp.max(s, axis=-1, keepdims=True))
        p = jnp.exp2(s - m_new).astype(BF16)
        acc_sc[g] = jnp.exp2(m - m_new) * acc_sc[g] + _dot(p, v)
        m_sc[g] = m_new

    def step(k0, size):
        v = v_ref[0, 0, pl.ds(k0, size), :]
        for g in range(heads):
            update(g, kt_ref[0, 0 if shared_k else g, :, pl.ds(k0, size)], v)

    if mode == "window":
        span = tq + 2 * WINDOW
        start = pl.multiple_of(jnp.clip(qi * tq - WINDOW, 0, n_tok - span), 128)
        qpos = qi * tq + lax.broadcasted_iota(jnp.int32, (tq, n_ctx + span), 0)
        col = lax.broadcasted_iota(jnp.int32, (tq, n_ctx + span), 1)
        kpos = start + col - n_ctx
        mask = (col < n_ctx) | (is_lat & (kpos >= n_ctx) & (jnp.abs(kpos - qpos) <= WINDOW))
        v = jnp.concatenate([v_ref[0, 0, 0:n_ctx, :], v_ref[0, 0, pl.ds(start, span), :]], axis=0)
        for g in range(heads):
            gk = 0 if shared_k else g
            kt = jnp.concatenate([kt_ref[0, gk, :, 0:n_ctx], kt_ref[0, gk, :, pl.ds(start, span)]], axis=1)
            update(g, kt, v, mask)
    else:
        @pl.when(jnp.logical_not(is_lat))
        def _():
            step(0, n_ctx)

        def body(ci, carry):
            step(pl.multiple_of(ci * ck, 128), ck)
            return carry

        lax.fori_loop(0, jnp.where(is_lat, n_tok // ck, 0), body, 0)

    def result(g):
        acc = acc_sc[g]
        return acc[:, 0:dv] / acc[:, dv:dv + 1]

    if mode == "diff":
        lam = sc_ref[heads]
        post = sc_ref[heads + 1]
        y = result(0) - lam * result(1)
        o_ref[0, 0] = (_rms_rows(y, gsub_ref[...]) * post).astype(o_ref.dtype)
    else:
        for g in range(heads):
            o_ref[0, g] = result(g).astype(o_ref.dtype)


def _attention(scalars, gsub, q, kt, v, *, heads, shared_k, mode, n_ctx):
    b, hq, n, hd = q.shape
    units = hq // heads
    dv = v.shape[-1]
    dvx = (dv // 128 + 1) * 128
    tq = TOK_TILE
    gk = 1 if shared_k else heads
    out_heads = 1 if mode == "diff" else heads
    ck = max(c for c in range(128, MAX_KEY_CHUNK[mode] + 1, 128) if n % c == 0)
    ones_col = (lax.broadcasted_iota(jnp.int32, v.shape[:-1] + (dvx - dv,), v.ndim - 1) == 0).astype(v.dtype)
    vx = jnp.concatenate([v, ones_col], axis=-1)
    kern = functools.partial(_attn_kernel, heads=heads, shared_k=shared_k, mode=mode, n_ctx=n_ctx,
                             n_tok=n, tq=tq, ck=ck, dv=dv)
    return pl.pallas_call(
        kern,
        grid=(b, units, n // tq),
        in_specs=[
            pl.BlockSpec(memory_space=pltpu.SMEM),
            pl.BlockSpec((1, dv), lambda bi, u, t: (0, 0)),
            pl.BlockSpec((1, heads, tq, hd), lambda bi, u, t: (bi, u, t, 0)),
            pl.BlockSpec((1, gk, hd, n), lambda bi, u, t: (bi, u, 0, 0)),
            pl.BlockSpec((1, 1, n, dvx), lambda bi, u, t: (bi, u, 0, 0)),
        ],
        out_specs=pl.BlockSpec((1, out_heads, tq, dv), lambda bi, u, t: (bi, u, t, 0)),
        out_shape=jax.ShapeDtypeStruct((b, units * out_heads, n, dv), BF16),
        scratch_shapes=[
            pltpu.VMEM((heads, tq, 1), F32),
            pltpu.VMEM((heads, tq, dvx), F32),
        ],
        compiler_params=_cparams(("parallel", "parallel", "arbitrary")),
        name="attn_" + mode,
    )(scalars, gsub, q, kt, vx)


def _merge_kernel(x_ref, mod_ref, g1_ref, ya_ref, yb_ref, yc_ref, wg_ref, bg_ref, wa_ref, wb_ref,
                  wc_ref, wo_ref, o_ref, *, d):
    x = x_ref[0]
    mod = mod_ref[0]
    a = (_rms_rows(x, g1_ref[...]) * (1.0 + mod[:, d:2 * d]) + mod[:, 0:d]).astype(BF16)
    gate = _sigmoid(_dot(a, wg_ref[...]) + bg_ref[...])
    m = (gate[:, 0:d] * _dot(ya_ref[0], wa_ref[...])
         + gate[:, d:2 * d] * _dot(yb_ref[0], wb_ref[...])
         + gate[:, 2 * d:3 * d] * _dot(yc_ref[0], wc_ref[...]))
    mix = _dot(m.astype(BF16), wo_ref[...])
    o_ref[0] = x + mod[:, 2 * d:3 * d] * mix


def _merge(x, mod, g1, ya, yb, yc, wg, bg, wa, wb, wc, wo, n_ctx_tiles):
    b, n, d = x.shape
    tn = TOK_TILE
    nb = b
    yw = ya.shape[-1]

    def mod_idx(bi, t):
        return (jnp.where(t < n_ctx_tiles, nb, bi), 0, 0)

    tok = lambda bi, t: (bi, t, 0)
    const = lambda bi, t: (0, 0)
    return pl.pallas_call(
        functools.partial(_merge_kernel, d=d),
        grid=(b, n // tn),
        in_specs=[
            pl.BlockSpec((1, tn, d), tok),
            pl.BlockSpec((1, 1, N_MOD * d), mod_idx),
            pl.BlockSpec((1, d), const),
            pl.BlockSpec((1, tn, yw), tok),
            pl.BlockSpec((1, tn, yw), tok),
            pl.BlockSpec((1, tn, yw), tok),
            pl.BlockSpec((d, 3 * d), const),
            pl.BlockSpec((1, 3 * d), const),
            pl.BlockSpec((yw, d), const),
            pl.BlockSpec((yw, d), const),
            pl.BlockSpec((yw, d), const),
            pl.BlockSpec((d, d), const),
        ],
        out_specs=pl.BlockSpec((1, tn, d), tok),
        out_shape=jax.ShapeDtypeStruct((b, n, d), F32),
        compiler_params=_cparams(("parallel", "parallel")),
        name="merge",
    )(x, mod, g1, ya, yb, yc, wg, bg, wa, wb, wc, wo)


def _first_index(hit, idx, big):
    return jnp.min(jnp.where(hit, idx, big), axis=0, keepdims=True)


def _route_kernel(x_ref, mod_ref, g2_ref, wrt_ref, eb_ref, ws1_ref, ws3_ref, ws2_ref,
                  f_ref, idx_ref, w_ref, xs_ref, *, d):
    x = x_ref[0]
    mod = mod_ref[0]
    f = _rms_rows(x, g2_ref[...]) * (1.0 + mod[:, 4 * d:5 * d]) + mod[:, 3 * d:4 * d]
    tn = f.shape[0]
    for s in range(d // 128):
        f_ref[0, pl.ds(s, tn, stride=d // 128), :] = f[:, s * 128:(s + 1) * 128]
    logits = lax.dot_general(wrt_ref[...], f, (((1,), (1,)), ((), ())),
                             preferred_element_type=F32, precision=lax.Precision.HIGHEST)
    scores = _sigmoid(logits)
    choice = scores + eb_ref[...]
    eidx = lax.broadcasted_iota(jnp.int32, (N_EXPERTS, tn), 0)
    lidx = lax.broadcasted_iota(jnp.int32, (GROUP_SIZE, tn), 0)
    gscore = []
    for g in range(N_GROUPS):
        cg = choice[g * GROUP_SIZE:(g + 1) * GROUP_SIZE, :]
        m1 = jnp.max(cg, axis=0, keepdims=True)
        first = _first_index(cg == m1, lidx, GROUP_SIZE)
        m2 = jnp.max(jnp.where(lidx == first, NEG, cg), axis=0, keepdims=True)
        gscore.append(m1 + m2)
    gs = jnp.concatenate(gscore, axis=0)
    gidx = lax.broadcasted_iota(jnp.int32, (N_GROUPS, tn), 0)
    gsel = jnp.zeros((N_GROUPS, tn), jnp.bool_)
    for _ in range(TOPK_GROUPS):
        gm = jnp.max(gs, axis=0, keepdims=True)
        first = _first_index(gs == gm, gidx, N_GROUPS)
        hit = gidx == first
        gsel = gsel | hit
        gs = jnp.where(hit, NEG, gs)
    gself = gsel.astype(F32)
    emask = jnp.concatenate(
        [jnp.broadcast_to(gself[g:g + 1, :], (GROUP_SIZE, tn)) for g in range(N_GROUPS)], axis=0)
    cur = jnp.where(emask > 0.5, choice, NEG)
    ids, ws = [], []
    for _ in range(TOP_K):
        m = jnp.max(cur, axis=0, keepdims=True)
        first = _first_index(cur == m, eidx, N_EXPERTS)
        hit = eidx == first
        ids.append(first)
        ws.append(jnp.sum(jnp.where(hit, scores, 0.0), axis=0, keepdims=True))
        cur = jnp.where(hit, NEG, cur)
    wsel = jnp.concatenate(ws, axis=0)
    idx_ref[0] = jnp.concatenate(ids, axis=0)
    w_ref[0] = wsel / jnp.sum(wsel, axis=0, keepdims=True) * ROUTE_SCALE
    fb = f.astype(BF16)
    h = _silu(_dot(fb, ws1_ref[...])) * _dot(fb, ws3_ref[...])
    xs_ref[0] = x + mod[:, 5 * d:6 * d] * _dot(h.astype(BF16), ws2_ref[...])


def _route(x, mod, g2, wrt, eb, ws1, ws3, ws2, n_ctx_tiles):
    b, n, d = x.shape
    tn = TOK_TILE
    nb = b
    ds = ws1.shape[-1]

    def mod_idx(bi, t):
        return (jnp.where(t < n_ctx_tiles, nb, bi), 0, 0)

    tok = lambda bi, t: (bi, t, 0)
    lane_tok = lambda bi, t: (bi, 0, t)
    const = lambda bi, t: (0, 0)
    return pl.pallas_call(
        functools.partial(_route_kernel, d=d),
        grid=(b, n // tn),
        in_specs=[
            pl.BlockSpec((1, tn, d), tok),
            pl.BlockSpec((1, 1, N_MOD * d), mod_idx),
            pl.BlockSpec((1, d), const),
            pl.BlockSpec((N_EXPERTS, d), const),
            pl.BlockSpec((N_EXPERTS, 1), const),
            pl.BlockSpec((d, ds), const),
            pl.BlockSpec((d, ds), const),
            pl.BlockSpec((ds, d), const),
        ],
        out_specs=[
            pl.BlockSpec((1, tn * (d // 128), 128), tok),
            pl.BlockSpec((1, TOP_K, tn), lane_tok),
            pl.BlockSpec((1, TOP_K, tn), lane_tok),
            pl.BlockSpec((1, tn, d), tok),
        ],
        out_shape=[
            jax.ShapeDtypeStruct((b, n * (d // 128), 128), F32),
            jax.ShapeDtypeStruct((b, TOP_K, n), jnp.int32),
            jax.ShapeDtypeStruct((b, TOP_K, n), F32),
            jax.ShapeDtypeStruct((b, n, d), F32),
        ],
        compiler_params=_cparams(("parallel", "parallel")),
        name="route_shared",
    )(x, mod, g2, wrt, eb, ws1, ws3, ws2)


def _expert_kernel(iexp_ref, iblk_ref, iflag_ref, nitems_ref, tok_ref, tok_next_ref, tok_ahead_ref,
                   dst_prev_ref, dst_ref, erow_ref, f_hbm, w1_ref, w3_ref, w2_ref, y_hbm, xbuf, ybuf,
                   wb1, wb3, wb2, gsem, ssem):
    i = pl.program_id(0)
    nitems = nitems_ref[0]
    blk = iblk_ref[i]
    expert = iexp_ref[i]
    first = (iflag_ref[i] & 1) != 0
    last = (iflag_ref[i] & 2) != 0
    ring = xbuf.shape[0]
    slot = blk % ring
    slot_prev = (blk + ring - 1) % ring
    slot_prev2 = (blk + ring - 2) % ring
    slot_ahead = (blk + 2) % ring
    rows = EXPERT_BLOCK
    parts = xbuf.shape[1] // rows
    n_blocks = y_hbm.shape[0] // (rows * parts)

    def gather(idx_ref, s):
        for j in range(rows):
            src = pl.multiple_of(idx_ref[0, 0, j], parts)
            pltpu.make_async_copy(f_hbm.at[pl.ds(src, parts)], xbuf.at[s, pl.ds(j * parts, parts)],
                                  gsem.at[s]).start(priority=j % 2)

    def gather_wait(s):
        pltpu.make_async_copy(f_hbm.at[pl.ds(0, rows * parts)], xbuf.at[s], gsem.at[s]).wait()

    def scatter(idx_ref, s):
        for j in range(rows):
            dst = pl.multiple_of(idx_ref[0, 0, j], parts)
            pltpu.make_async_copy(ybuf.at[s, pl.ds(j * parts, parts)], y_hbm.at[pl.ds(dst, parts)],
                                  ssem.at[s]).start(priority=j % 2)

    def scatter_wait(s):
        pltpu.make_async_copy(ybuf.at[s], y_hbm.at[pl.ds(0, rows * parts)], ssem.at[s]).wait()

    def expert_rows(s, after_up=None, after_down=None):
        xb = jnp.concatenate([xbuf[s, pl.ds(p, rows, stride=parts), :] for p in range(parts)],
                             axis=1).astype(BF16)
        h1 = _dot(xb, wb1[...])
        h3 = _dot(xb, wb3[...])
        if after_up is not None:
            after_up()
        y = _dot((_silu(h1) * h3).astype(BF16), wb2[...])
        if after_down is not None:
            after_down()
        return y

    def store_rows(s, y, accumulate):
        for p in range(parts):
            part = y[:, p * 128:(p + 1) * 128]
            if accumulate:
                part = part + ybuf[s, pl.ds(p, rows, stride=parts), :]
            ybuf[s, pl.ds(p, rows, stride=parts), :] = part

    valid = i < nitems

    @pl.when(valid & ((iflag_ref[i] & 4) != 0))
    def _():
        wb1[...] = w1_ref[0].astype(BF16)
        wb3[...] = w3_ref[0].astype(BF16)
        wb2[...] = w2_ref[0].astype(BF16)

    fast = valid & first & last & (blk >= 3) & (blk <= n_blocks - 3)

    @pl.when(fast)
    def _():
        gather_wait(slot)
        scatter_wait(slot)
        y = expert_rows(slot, after_up=lambda: gather(tok_ahead_ref, slot_ahead),
                        after_down=lambda: scatter(dst_prev_ref, slot_prev))
        store_rows(slot, y, False)

    @pl.when(valid & jnp.logical_not(fast))
    def _():
        @pl.when(first)
        def _():
            @pl.when(blk == 0)
            def _():
                gather(tok_ref, 0)
                if n_blocks > 1:
                    gather(tok_next_ref, 1)

            @pl.when(blk + 2 < n_blocks)
            def _():
                gather(tok_ahead_ref, slot_ahead)

            @pl.when(blk >= 1)
            def _():
                scatter(dst_prev_ref, slot_prev)

            gather_wait(slot)

            @pl.when(blk >= 3)
            def _():
                scatter_wait(slot)

        y = expert_rows(slot) * (erow_ref[0] == expert).astype(F32)

        @pl.when(first)
        def _():
            store_rows(slot, y, False)

        @pl.when(jnp.logical_not(first))
        def _():
            store_rows(slot, y, True)

        @pl.when(last & (blk == n_blocks - 1))
        def _():
            scatter(dst_ref, slot)

            @pl.when(blk >= 2)
            def _():
                scatter_wait(slot_prev2)

            @pl.when(blk >= 1)
            def _():
                scatter_wait(slot_prev)

            scatter_wait(slot)


def _experts(iexp, iblk, iflag, nitems, row_tok, row_dst, erow, f_rows, w1, w3, w2):
    n_items = iexp.shape[0]
    n_blocks = row_tok.shape[0]
    d, de = w1.shape[1], w1.shape[2]
    parts = d // 128
    rows = EXPERT_BLOCK
    last = n_blocks - 1
    cur = lambda i, ie, ib, fl, nt: (ib[i], 0, 0)
    nxt = lambda i, ie, ib, fl, nt: (jnp.minimum(ib[i] + 1, last), 0, 0)
    ahd = lambda i, ie, ib, fl, nt: (jnp.minimum(ib[i] + 2, last), 0, 0)
    prv = lambda i, ie, ib, fl, nt: (jnp.maximum(ib[i] - 1, 0), 0, 0)
    wsel = lambda i, ie, ib, fl, nt: (ie[i], 0, 0)
    grid_spec = pltpu.PrefetchScalarGridSpec(
        num_scalar_prefetch=4,
        grid=(n_items,),
        in_specs=[
            pl.BlockSpec((1, 1, rows), cur, memory_space=pltpu.SMEM),
            pl.BlockSpec((1, 1, rows), nxt, memory_space=pltpu.SMEM),
            pl.BlockSpec((1, 1, rows), ahd, memory_space=pltpu.SMEM),
            pl.BlockSpec((1, 1, rows), prv, memory_space=pltpu.SMEM),
            pl.BlockSpec((1, 1, rows), cur, memory_space=pltpu.SMEM),
            pl.BlockSpec((1, rows, 1), cur),
            pl.BlockSpec(memory_space=pl.ANY),
            pl.BlockSpec((1, d, de), wsel),
            pl.BlockSpec((1, d, de), wsel),
            pl.BlockSpec((1, de, d), wsel),
        ],
        out_specs=pl.BlockSpec(memory_space=pl.ANY),
        scratch_shapes=[
            pltpu.VMEM((EXPERT_RING, rows * parts, 128), F32),
            pltpu.VMEM((EXPERT_RING, rows * parts, 128), F32),
            pltpu.VMEM((d, de), BF16),
            pltpu.VMEM((d, de), BF16),
            pltpu.VMEM((de, d), BF16),
            pltpu.SemaphoreType.DMA((EXPERT_RING,)),
            pltpu.SemaphoreType.DMA((EXPERT_RING,)),
        ],
    )
    return pl.pallas_call(
        _expert_kernel,
        grid_spec=grid_spec,
        out_shape=jax.ShapeDtypeStruct((n_blocks * rows * parts, 128), F32),
        compiler_params=_cparams(("arbitrary",)),
        name="experts",
    )(iexp, iblk, iflag, nitems, row_tok, row_tok, row_tok, row_dst, row_dst, erow, f_rows, w1, w3, w2)


ASSIGN_BITS = 20


def _routing_tables(idx, n_tokens, parts):
    n_assign = n_tokens * TOP_K
    blk = EXPERT_BLOCK
    assert n_assign % blk == 0 and n_assign <= (1 << ASSIGN_BITS)
    n_blocks = n_assign // blk
    n_items = n_blocks + N_EXPERTS
    flat_e = idx.reshape(-1).astype(jnp.int32)
    key = jnp.sort((flat_e << ASSIGN_BITS) | jnp.arange(n_assign, dtype=jnp.int32))
    e_sorted = key >> ASSIGN_BITS
    order = key & ((1 << ASSIGN_BITS) - 1)
    tok = order // TOP_K
    row_dst = ((order % TOP_K) * n_tokens + tok) * parts
    tok = tok * parts

    experts = jnp.arange(N_EXPERTS, dtype=jnp.int32)
    counts = jnp.sum((flat_e.reshape(-1, 1, blk) == experts[None, :, None]).astype(jnp.int32), axis=(0, 2))
    ends = jnp.cumsum(counts)
    starts = ends - counts
    first_blk = starts // blk
    n_be = jnp.where(ends > starts, (ends - 1) // blk - first_blk + 1, 0)
    item_end = jnp.cumsum(n_be)
    item_off = item_end - n_be
    total = item_end[-1]
    i = jnp.arange(n_items, dtype=jnp.int32)
    iexp = jnp.minimum(jnp.sum((item_end[None, :] <= i[:, None]).astype(jnp.int32), axis=1), N_EXPERTS - 1)
    pick = (iexp[:, None] == experts[None, :]).astype(jnp.int32)
    base = jnp.sum(pick * (first_blk - item_off)[None, :], axis=1)
    iblk = jnp.where(i < total, base + i, n_blocks - 1).astype(jnp.int32)
    prev_blk = jnp.concatenate([jnp.full((1,), -1, jnp.int32), iblk[:-1]])
    next_blk = jnp.concatenate([iblk[1:], jnp.full((1,), -1, jnp.int32)])
    is_first = iblk != prev_blk
    is_last = (iblk != next_blk) | (i == total - 1)
    new_expert = iexp != jnp.concatenate([jnp.full((1,), -1, jnp.int32), iexp[:-1]])
    iflag = is_first.astype(jnp.int32) + 2 * is_last.astype(jnp.int32) + 4 * new_expert.astype(jnp.int32)
    return (iexp, iblk, iflag, total.astype(jnp.int32).reshape(1), tok.reshape(n_blocks, 1, blk),
            row_dst.reshape(n_blocks, 1, blk), e_sorted.reshape(n_blocks, blk, 1))


def _combine_kernel(xs_ref, mod_ref, w_ref, *refs, d):
    y_refs, o_ref = refs[:TOP_K], refs[TOP_K]
    w = w_ref[0]
    tn = w.shape[0]
    parts = d // 128
    wk = [jnp.broadcast_to(w[:, k:k + 1], (tn, 128)) for k in range(TOP_K)]
    for p in range(parts):
        tot = y_refs[0][pl.ds(p, tn, stride=parts), :] * wk[0]
        for k in range(1, TOP_K):
            tot = tot + y_refs[k][pl.ds(p, tn, stride=parts), :] * wk[k]
        sl = slice(p * 128, (p + 1) * 128)
        o_ref[0, :, sl] = xs_ref[0, :, sl] + mod_ref[0][:, 5 * d + p * 128:5 * d + (p + 1) * 128] * tot


def _combine(xs, mod, w, y, n_ctx_tiles):
    b, n, d = xs.shape
    tn = TOK_TILE
    nb = b
    tiles = n // tn
    per_slot = b * tiles

    def mod_idx(bi, t):
        return (jnp.where(t < n_ctx_tiles, nb, bi), 0, 0)

    tok = lambda bi, t: (bi, t, 0)
    y_specs = [pl.BlockSpec((tn * (d // 128), 128),
                            functools.partial(lambda bi, t, k: (k * per_slot + bi * tiles + t, 0), k=k))
               for k in range(TOP_K)]
    return pl.pallas_call(
        functools.partial(_combine_kernel, d=d),
        grid=(b, tiles),
        in_specs=[pl.BlockSpec((1, tn, d), tok), pl.BlockSpec((1, 1, N_MOD * d), mod_idx),
                  pl.BlockSpec((1, tn, TOP_K), tok)] + y_specs,
        out_specs=pl.BlockSpec((1, tn, d), tok),
        out_shape=jax.ShapeDtypeStruct((b, n, d), F32),
        compiler_params=_cparams(("parallel", "parallel")),
        name="combine",
    )(xs, mod, w, *([y] * TOP_K))


def _final_kernel(x_ref, g_ref, o_ref):
    o_ref[0] = _rms_rows(x_ref[0], g_ref[...])


def _final_norm(x, g, n_ctx_tiles):
    b, n, d = x.shape
    tn = TOK_TILE
    n_lat_tiles = n // tn - n_ctx_tiles
    return pl.pallas_call(
        _final_kernel,
        grid=(b, n_lat_tiles),
        in_specs=[pl.BlockSpec((1, tn, d), lambda bi, t: (bi, t + n_ctx_tiles, 0)),
                  pl.BlockSpec((1, d), lambda bi, t: (0, 0))],
        out_specs=pl.BlockSpec((1, tn, d), lambda bi, t: (bi, t, 0)),
        out_shape=jax.ShapeDtypeStruct((b, n_lat_tiles * tn, d), F32),
        compiler_params=_cparams(("parallel", "parallel")),
        name="final_norm",
    )(x, g)


def _rotary_column_tables():
    a_q, a_kv = A_HEADS * HEAD_DIM, A_KV * HEAD_DIM
    b_q, b_kv = B_HEADS * HEAD_DIM, B_KV * HEAD_DIM
    c_qk, c_v = 2 * C_HEADS * HEAD_DIM, C_HEADS * 2 * HEAD_DIM
    sizes = (a_q, a_kv, a_kv, b_q, b_kv, b_kv, c_qk, c_qk, c_v)
    off = np.concatenate([[0], np.cumsum(sizes)])
    seg = lambda i: np.arange(off[i], off[i + 1])
    perm_r = np.concatenate([seg(0), seg(1), seg(3), seg(4), seg(6), seg(7)])
    perm_v = np.concatenate([seg(2), seg(5), seg(8)])
    col = np.arange(R_W)
    i = col % HEAD_DIM
    partner_local = np.where((i % (2 * ROPE_FREQS)) < ROPE_FREQS, i + ROPE_FREQS, i - ROPE_FREQS)
    partner = col - i + partner_local
    return perm_r, perm_v, partner


def _rope_tables(n_ctx, n_lat):
    t = jnp.arange(n_lat, dtype=jnp.int32)
    row_pos = (t // GRID_W).astype(F32)
    col_pos = (t % GRID_W).astype(F32)
    inv_freq = jnp.power(ROPE_THETA, -jnp.arange(ROPE_FREQS, dtype=F32) / ROPE_FREQS)
    ang_r = row_pos[:, None] * inv_freq
    ang_c = col_pos[:, None] * inv_freq
    cos64 = jnp.concatenate([jnp.cos(ang_r), jnp.cos(ang_r), jnp.cos(ang_c), jnp.cos(ang_c)], axis=1)
    sin64 = jnp.concatenate([-jnp.sin(ang_r), jnp.sin(ang_r), -jnp.sin(ang_c), jnp.sin(ang_c)], axis=1)
    cos64 = jnp.concatenate([jnp.ones((n_ctx, HEAD_DIM), F32), cos64], axis=0)
    sin64 = jnp.concatenate([jnp.zeros((n_ctx, HEAD_DIM), F32), sin64], axis=0)
    return jnp.tile(cos64, (1, 2)), jnp.tile(sin64, (1, 2))


def _heads_major(t, n_heads):
    b, n, w = t.shape
    return t.reshape(b, n, n_heads, w // n_heads).transpose(0, 2, 1, 3)


def _heads_transposed(t, n_heads):
    b, n, w = t.shape
    return t.reshape(b, n, n_heads, w // n_heads).transpose(0, 2, 3, 1)


def _tokens_major(t):
    b, h, n, dv = t.shape
    return t.transpose(0, 2, 1, 3).reshape(b, n, h * dv)


def kernel(x, c, ctx, c_ctx, w_mod, b_mod, g_norm1, w_qkv, g_qnorm_a, g_knorm_a, sink_b, lam_q1, lam_k1, lam_q2, lam_k2, g_subln_c, w_br_a, w_br_b, w_br_c, w_gate, b_gate, w_out, g_norm2, w_router, e_bias, w1, w3, w2, ws1, ws3, ws2, g_final):
    bsz, n_lat, d = x.shape
    n_ctx = ctx.shape[1]
    depth = w_mod.shape[0]
    n_tok = n_ctx + n_lat
    assert n_ctx % TOK_TILE == 0 and n_lat % KEY_CHUNK == 0 and n_ctx % 128 == 0
    n_ctx_tiles = n_ctx // TOK_TILE
    n_all = bsz * n_tok

    rows = -(-(bsz + 1) // 8) * 8
    cvec = jnp.concatenate([c, c_ctx[None, :], jnp.zeros((rows - bsz - 1, d), F32)], axis=0)
    mod_all = _modulation(cvec, w_mod, b_mod)

    perm_r, perm_v, partner = _rotary_column_tables()
    cos_t, sin_t = _rope_tables(n_ctx, n_lat)
    head_of = np.arange(R_A) // HEAD_DIM
    ones_blk = jnp.asarray((head_of[:, None] == head_of[None, :]).astype(np.float32), BF16)
    scale = HEAD_DIM ** -0.5 * LOG2E
    unit = jnp.ones((HEAD_DIM,), F32)

    xs = jnp.concatenate([ctx, x], axis=1)
    for l in range(depth):
        lam_init = 0.8 - 0.6 * math.exp(-0.3 * l)
        lam = (jnp.exp(jnp.dot(lam_q1[l], lam_k1[l])) - jnp.exp(jnp.dot(lam_q2[l], lam_k2[l]))).astype(F32) + lam_init
        mod = mod_all[l].reshape(rows, 1, N_MOD * d)
        g1 = g_norm1[l].reshape(1, d)

        w_r = w_qkv[l][:, perm_r]
        wqkv = jnp.concatenate([w_r, w_qkv[l][:, perm_v]], axis=1).astype(BF16)
        wsw = w_r[:, partner].astype(BF16)
        grow = jnp.concatenate([jnp.tile(g_qnorm_a[l] * scale, A_HEADS), jnp.tile(g_knorm_a[l], A_KV),
                                jnp.tile(unit * scale, B_HEADS), jnp.tile(unit, B_KV),
                                jnp.tile(unit * scale, 2 * C_HEADS), jnp.tile(unit, 2 * C_HEADS)])
        gsrow = grow[partner]
        r_all, v_all = _pre_attention(xs, mod, g1, wqkv, wsw, ones_blk, grow.reshape(1, R_W),
                                      gsrow.reshape(1, R_W), cos_t, sin_t, n_ctx_tiles)

        o = 0
        qa = _heads_major(r_all[..., o:o + A_HEADS * HEAD_DIM], A_HEADS); o += A_HEADS * HEAD_DIM
        kta = _heads_transposed(r_all[..., o:o + A_KV * HEAD_DIM], A_KV); o += A_KV * HEAD_DIM
        qb = _heads_major(r_all[..., o:o + B_HEADS * HEAD_DIM], B_HEADS); o += B_HEADS * HEAD_DIM
        ktb = _heads_transposed(r_all[..., o:o + B_KV * HEAD_DIM], B_KV); o += B_KV * HEAD_DIM
        qc = _heads_major(r_all[..., o:o + 2 * C_HEADS * HEAD_DIM], 2 * C_HEADS); o += 2 * C_HEADS * HEAD_DIM
        ktc = _heads_transposed(r_all[..., o:o + 2 * C_HEADS * HEAD_DIM], 2 * C_HEADS)
        o = 0
        va = _heads_major(v_all[..., o:o + A_KV * HEAD_DIM], A_KV); o += A_KV * HEAD_DIM
        vb = _heads_major(v_all[..., o:o + B_KV * HEAD_DIM], B_KV); o += B_KV * HEAD_DIM
        vc = _heads_major(v_all[..., o:], C_HEADS)

        scal = jnp.concatenate([sink_b[l].astype(F32) * LOG2E, lam.reshape(1), jnp.full((1,), 1.0 - lam_init, F32)])
        scal_c = jnp.concatenate([jnp.zeros((2,), F32), lam.reshape(1), jnp.full((1,), 1.0 - lam_init, F32)])
        gsub = g_subln_c[l].reshape(1, 2 * HEAD_DIM)
        g64 = jnp.ones((1, HEAD_DIM), F32)
        ya = _attention(scal, g64, qa, kta, va, heads=A_HEADS // A_KV, shared_k=True, mode="global", n_ctx=n_ctx)
        yb = _attention(scal, g64, qb, ktb, vb, heads=B_HEADS // B_KV, shared_k=True, mode="window", n_ctx=n_ctx)
        yc = _attention(scal_c, gsub, qc, ktc, vc, heads=2, shared_k=False, mode="diff", n_ctx=n_ctx)

        x1 = _merge(xs, mod, g1, _tokens_major(ya), _tokens_major(yb), _tokens_major(yc),
                    w_gate[l].astype(BF16), b_gate[l].reshape(1, -1), w_br_a[l].astype(BF16),
                    w_br_b[l].astype(BF16), w_br_c[l].astype(BF16), w_out[l].astype(BF16), n_ctx_tiles)

        f, idx_t, w_t, x_sh = _route(x1, mod, g_norm2[l].reshape(1, d), w_router[l].T,
                                     e_bias[l].reshape(N_EXPERTS, 1), ws1[l].astype(BF16),
                                     ws3[l].astype(BF16), ws2[l].astype(BF16), n_ctx_tiles)

        idx = idx_t.transpose(0, 2, 1).reshape(n_all, TOP_K)
        iexp, iblk, iflag, nitems, row_tok, row_dst, erow = _routing_tables(idx, n_all, d // 128)
        y = _experts(iexp, iblk, iflag, nitems, row_tok, row_dst, erow, f.reshape(n_all * (d // 128), 128),
                     w1[l], w3[l], w2[l])
        xs = _combine(x_sh, mod, w_t.transpose(0, 2, 1), y, n_ctx_tiles)

    return _final_norm(xs, g_final.reshape(1, d), n_ctx_tiles)
```

```python
import functools
import math

import numpy as np
import jax
import jax.numpy as jnp
from jax import lax
from jax.experimental import pallas as pl
from jax.experimental.pallas import tpu as pltpu

F32 = jnp.float32
BF16 = jnp.bfloat16

HEAD_DIM = 64
ROPE_FREQS = HEAD_DIM // 4
ROPE_THETA = 10000.0
GRID_W = 64
WINDOW = 128
A_HEADS, A_KV = 8, 2
B_HEADS, B_KV = 8, 2
C_HEADS = 4
N_EXPERTS = 128
TOP_K = 8
N_GROUPS = 8
TOPK_GROUPS = 4
GROUP_SIZE = N_EXPERTS // N_GROUPS
ROUTE_SCALE = 2.5
EXPERT_BLOCK = 128
EXPERT_RING = 3
N_MOD = 6
EPS = 1e-6
NEG = -1e30

R_A = A_HEADS * HEAD_DIM + A_KV * HEAD_DIM
R_B = B_HEADS * HEAD_DIM + B_KV * HEAD_DIM
R_C = 4 * C_HEADS * HEAD_DIM
R_W = R_A + R_B + R_C
V_W = A_KV * HEAD_DIM + B_KV * HEAD_DIM + C_HEADS * 2 * HEAD_DIM

TOK_TILE = 256
ATTN_TQ = 256
DIFF_HEADS_PER_STEP = 4
KEY_CHUNK = 512
MAX_KEY_CHUNK = {"global": 8448, "diff": 8448, "window": 128}
LOG2E = math.log2(math.e)
VMEM_LIMIT = 56 * 1024 * 1024


def _cparams(sem, **kw):
    return pltpu.CompilerParams(dimension_semantics=sem, vmem_limit_bytes=VMEM_LIMIT, **kw)


def _dot(a, b):
    return jnp.dot(a, b, preferred_element_type=F32)


def _sigmoid(x):
    return 1.0 / (1.0 + jnp.exp(-x))


def _silu(x):
    return x * _sigmoid(x)


def _rms_rows(x, g):
    return x * lax.rsqrt(jnp.mean(x * x, axis=-1, keepdims=True) + EPS) * g


def _mod_kernel(c_ref, w_ref, b_ref, o_ref):
    cs = _silu(c_ref[...])
    o_ref[0] = jnp.dot(cs, w_ref[0], preferred_element_type=F32,
                       precision=lax.Precision.HIGHEST) + b_ref[0]


def _modulation(cvec, w_mod, b_mod):
    depth, d, n = w_mod.shape
    rows = cvec.shape[0]
    bn = 1536
    return pl.pallas_call(
        _mod_kernel,
        grid=(depth, n // bn),
        in_specs=[
            pl.BlockSpec((rows, d), lambda l, j: (0, 0)),
            pl.BlockSpec((1, d, bn), lambda l, j: (l, 0, j)),
            pl.BlockSpec((1, 1, bn), lambda l, j: (l, 0, j)),
        ],
        out_specs=pl.BlockSpec((1, rows, bn), lambda l, j: (l, 0, j)),
        out_shape=jax.ShapeDtypeStruct((depth, rows, n), F32),
        compiler_params=_cparams(("parallel", "parallel")),
        name="modulation",
    )(cvec, w_mod, b_mod.reshape(depth, 1, n))


def _pre_kernel(x_ref, mod_ref, g1_ref, wqkv_ref, wsw_ref, ones_ref, grow_ref, gsrow_ref,
                cos_ref, sin_ref, r_out, v_out, *, d):
    x = x_ref[0]
    mod = mod_ref[0]
    a = (_rms_rows(x, g1_ref[...]) * (1.0 + mod[:, d:2 * d]) + mod[:, 0:d]).astype(BF16)
    p = _dot(a, wqkv_ref[...])
    ps = _dot(a, wsw_ref[...])
    pa = p[:, :R_A]
    sq = pa * pa
    hi = sq.astype(BF16)
    lo = (sq - hi.astype(F32)).astype(BF16)
    ssq = _dot(hi, ones_ref[...]) + _dot(lo, ones_ref[...])
    rinv = lax.rsqrt(ssq * (1.0 / HEAD_DIM) + EPS)
    cos = cos_ref[...]
    sin = sin_ref[...]
    n_tiles = R_W // 128
    for j in range(n_tiles):
        sl = slice(j * 128, (j + 1) * 128)
        o = p[:, sl] * (grow_ref[:, sl] * cos) + ps[:, sl] * (gsrow_ref[:, sl] * sin)
        if (j + 1) * 128 <= R_A:
            o = o * rinv[:, sl]
        r_out[0, :, sl] = o.astype(BF16)
    v_out[0] = p[:, R_W:].astype(BF16)


def _pre_attention(x, mod, g1, wqkv, wsw, ones_blk, grow, gsrow, cos_t, sin_t, n_ctx_tiles):
    b, n, d = x.shape
    tn = TOK_TILE
    nb = b

    def mod_idx(bi, t):
        return (jnp.where(t < n_ctx_tiles, nb, bi), 0, 0)

    return pl.pallas_call(
        functools.partial(_pre_kernel, d=d),
        grid=(b, n // tn),
        in_specs=[
            pl.BlockSpec((1, tn, d), lambda bi, t: (bi, t, 0)),
            pl.BlockSpec((1, 1, N_MOD * d), mod_idx),
            pl.BlockSpec((1, d), lambda bi, t: (0, 0)),
            pl.BlockSpec((d, R_W + V_W), lambda bi, t: (0, 0)),
            pl.BlockSpec((d, R_W), lambda bi, t: (0, 0)),
            pl.BlockSpec((R_A, R_A), lambda bi, t: (0, 0)),
            pl.BlockSpec((1, R_W), lambda bi, t: (0, 0)),
            pl.BlockSpec((1, R_W), lambda bi, t: (0, 0)),
            pl.BlockSpec((tn, 128), lambda bi, t: (t, 0)),
            pl.BlockSpec((tn, 128), lambda bi, t: (t, 0)),
        ],
        out_specs=[
            pl.BlockSpec((1, tn, R_W), lambda bi, t: (bi, t, 0)),
            pl.BlockSpec((1, tn, V_W), lambda bi, t: (bi, t, 0)),
        ],
        out_shape=[
            jax.ShapeDtypeStruct((b, n, R_W), BF16),
            jax.ShapeDtypeStruct((b, n, V_W), BF16),
        ],
        compiler_params=_cparams(("parallel", "parallel")),
        name="pre_attention",
    )(x, mod, g1, wqkv, wsw, ones_blk, grow, gsrow, cos_t, sin_t)


def _attn_kernel(sc_ref, gsub_ref, q_ref, kt_ref, v_ref, o_ref, m_sc, acc_sc, *, heads, hpv,
                 shared_k, mode, n_ctx, n_tok, tq, ck, dv):
    u = pl.program_id(1)
    qi = pl.program_id(2)
    is_lat = qi >= n_ctx // tq
    dvx = acc_sc.shape[-1]

    for g in range(heads):
        if mode == "window":
            m_sc[g] = jnp.full((tq, 1), sc_ref[u * heads + g], F32)
            lane = lax.broadcasted_iota(jnp.int32, (tq, dvx), 1)
            acc_sc[g] = jnp.where(lane == dv, 1.0, 0.0).astype(F32)
        else:
            m_sc[g] = jnp.full((tq, 1), NEG, F32)
            acc_sc[g] = jnp.zeros((tq, dvx), F32)

    def probs(g, s, mask):
        if mask is not None:
            s = jnp.where(mask, s, NEG)
        m = m_sc[g]
        m_new = jnp.maximum(m, jnp.max(s, axis=-1, keepdims=True))
        m_sc[g] = m_new
        return jnp.exp2(s - m_new).astype(BF16), jnp.exp2(m - m_new)

    def update_all(kt_of, v_of, mask=None):
        s_next = _dot(q_ref[0, 0], kt_of(0))
        for g in range(heads):
            s = s_next
            if g + 1 < heads:
                s_next = _dot(q_ref[0, g + 1], kt_of(g + 1))
            p, alpha = probs(g, s, mask)
            acc_sc[g] = alpha * acc_sc[g] + _dot(p, v_of(g))

    def step(k0, size):
        update_all(lambda g: kt_ref[0, 0 if shared_k else g, :, pl.ds(k0, size)],
                   lambda g: v_ref[0, g // hpv, pl.ds(k0, size), :])

    if mode == "window":
        span = tq + 2 * WINDOW
        start = pl.multiple_of(jnp.clip(qi * tq - WINDOW, 0, n_tok - span), 128)
        qpos = qi * tq + lax.broadcasted_iota(jnp.int32, (tq, n_ctx + span), 0)
        col = lax.broadcasted_iota(jnp.int32, (tq, n_ctx + span), 1)
        kpos = start + col - n_ctx
        mask = (col < n_ctx) | (is_lat & (kpos >= n_ctx) & (jnp.abs(kpos - qpos) <= WINDOW))
        v = jnp.concatenate([v_ref[0, 0, 0:n_ctx, :], v_ref[0, 0, pl.ds(start, span), :]], axis=0)

        def kt_of(g):
            gk = 0 if shared_k else g
            return jnp.concatenate([kt_ref[0, gk, :, 0:n_ctx], kt_ref[0, gk, :, pl.ds(start, span)]], axis=1)

        update_all(kt_of, lambda g: v, mask)
    else:
        @pl.when(jnp.logical_not(is_lat))
        def _():
            step(0, n_ctx)

        def body(ci, carry):
            step(pl.multiple_of(ci * ck, 128), ck)
            return carry

        lax.fori_loop(0, jnp.where(is_lat, n_tok // ck, 0), body, 0)

    def result(g):
        acc = acc_sc[g]
        return acc[:, 0:dv] / acc[:, dv:dv + 1]

    if mode == "diff":
        lam = sc_ref[heads]
        post = sc_ref[heads + 1]
        for j in range(heads // 2):
            y = result(2 * j) - lam * result(2 * j + 1)
            o_ref[0, j] = (_rms_rows(y, gsub_ref[...]) * post).astype(o_ref.dtype)
    else:
        for g in range(heads):
            o_ref[0, g] = result(g).astype(o_ref.dtype)


def _attention(scalars, gsub, q, kt, v, *, heads, hpv, shared_k, mode, n_ctx):
    b, hq, n, hd = q.shape
    units = hq // heads
    dv = v.shape[-1]
    dvx = (dv // 128 + 1) * 128
    tq = ATTN_TQ
    gk = 1 if shared_k else heads
    vh = heads // hpv
    out_heads = heads // 2 if mode == "diff" else heads
    ck = max(c for c in range(128, MAX_KEY_CHUNK[mode] + 1, 128) if n % c == 0)
    ones_col = (lax.broadcasted_iota(jnp.int32, v.shape[:-1] + (dvx - dv,), v.ndim - 1) == 0).astype(v.dtype)
    vx = jnp.concatenate([v, ones_col], axis=-1)
    kern = functools.partial(_attn_kernel, heads=heads, hpv=hpv, shared_k=shared_k, mode=mode,
                             n_ctx=n_ctx, n_tok=n, tq=tq, ck=ck, dv=dv)
    return pl.pallas_call(
        kern,
        grid=(b, units, n // tq),
        in_specs=[
            pl.BlockSpec(memory_space=pltpu.SMEM),
            pl.BlockSpec((1, dv), lambda bi, u, t: (0, 0)),
            pl.BlockSpec((1, heads, tq, hd), lambda bi, u, t: (bi, u, t, 0)),
            pl.BlockSpec((1, gk, hd, n), lambda bi, u, t: (bi, u, 0, 0)),
            pl.BlockSpec((1, vh, n, dvx), lambda bi, u, t: (bi, u, 0, 0)),
        ],
        out_specs=pl.BlockSpec((1, out_heads, tq, dv), lambda bi, u, t: (bi, u, t, 0)),
        out_shape=jax.ShapeDtypeStruct((b, units * out_heads, n, dv), BF16),
        scratch_shapes=[
            pltpu.VMEM((heads, tq, 1), F32),
            pltpu.VMEM((heads, tq, dvx), F32),
        ],
        compiler_params=_cparams(("parallel", "parallel", "arbitrary")),
        name="attn_" + mode,
    )(scalars, gsub, q, kt, vx)


def _merge_kernel(x_ref, mod_ref, g1_ref, ya_ref, yb_ref, yc_ref, wg_ref, bg_ref, wa_ref, wb_ref,
                  wc_ref, wo_ref, o_ref, *, d):
    x = x_ref[0]
    mod = mod_ref[0]
    a = (_rms_rows(x, g1_ref[...]) * (1.0 + mod[:, d:2 * d]) + mod[:, 0:d]).astype(BF16)
    gate = _sigmoid(_dot(a, wg_ref[...]) + bg_ref[...])
    m = (gate[:, 0:d] * _dot(ya_ref[0], wa_ref[...])
         + gate[:, d:2 * d] * _dot(yb_ref[0], wb_ref[...])
         + gate[:, 2 * d:3 * d] * _dot(yc_ref[0], wc_ref[...]))
    mix = _dot(m.astype(BF16), wo_ref[...])
    o_ref[0] = x + mod[:, 2 * d:3 * d] * mix


def _merge(x, mod, g1, ya, yb, yc, wg, bg, wa, wb, wc, wo, n_ctx_tiles):
    b, n, d = x.shape
    tn = TOK_TILE
    nb = b
    yw = ya.shape[-1]

    def mod_idx(bi, t):
        return (jnp.where(t < n_ctx_tiles, nb, bi), 0, 0)

    tok = lambda bi, t: (bi, t, 0)
    const = lambda bi, t: (0, 0)
    return pl.pallas_call(
        functools.partial(_merge_kernel, d=d),
        grid=(b, n // tn),
        in_specs=[
            pl.BlockSpec((1, tn, d), tok),
            pl.BlockSpec((1, 1, N_MOD * d), mod_idx),
            pl.BlockSpec((1, d), const),
            pl.BlockSpec((1, tn, yw), tok),
            pl.BlockSpec((1, tn, yw), tok),
            pl.BlockSpec((1, tn, yw), tok),
            pl.BlockSpec((d, 3 * d), const),
            pl.BlockSpec((1, 3 * d), const),
            pl.BlockSpec((yw, d), const),
            pl.BlockSpec((yw, d), const),
            pl.BlockSpec((yw, d), const),
            pl.BlockSpec((d, d), const),
        ],
        out_specs=pl.BlockSpec((1, tn, d), tok),
        out_shape=jax.ShapeDtypeStruct((b, n, d), F32),
        compiler_params=_cparams(("parallel", "parallel")),
        name="merge",
    )(x, mod, g1, ya, yb, yc, wg, bg, wa, wb, wc, wo)


def _first_index(hit, idx, big):
    return jnp.min(jnp.where(hit, idx, big), axis=0, keepdims=True)


def _route_kernel(x_ref, mod_ref, g2_ref, wrt_ref, eb_ref, ws1_ref, ws3_ref, ws2_ref,
                  f_ref, idx_ref, w_ref, xs_ref, *, d):
    x = x_ref[0]
    mod = mod_ref[0]
    f = _rms_rows(x, g2_ref[...]) * (1.0 + mod[:, 4 * d:5 * d]) + mod[:, 3 * d:4 * d]
    tn = f.shape[0]
    for s in range(d // 128):
        f_ref[0, pl.ds(s, tn, stride=d // 128), :] = f[:, s * 128:(s + 1) * 128]
    logits = lax.dot_general(wrt_ref[...], f, (((1,), (1,)), ((), ())),
                             preferred_element_type=F32, precision=lax.Precision.HIGHEST)
    scores = _sigmoid(logits)
    choice = scores + eb_ref[...]
    eidx = lax.broadcasted_iota(jnp.int32, (N_EXPERTS, tn), 0)
    lidx = lax.broadcasted_iota(jnp.int32, (GROUP_SIZE, tn), 0)
    gscore = []
    for g in range(N_GROUPS):
        cg = choice[g * GROUP_SIZE:(g + 1) * GROUP_SIZE, :]
        m1 = jnp.max(cg, axis=0, keepdims=True)
        first = _first_index(cg == m1, lidx, GROUP_SIZE)
        m2 = jnp.max(jnp.where(lidx == first, NEG, cg), axis=0, keepdims=True)
        gscore.append(m1 + m2)
    gs = jnp.concatenate(gscore, axis=0)
    gidx = lax.broadcasted_iota(jnp.int32, (N_GROUPS, tn), 0)
    gsel = jnp.zeros((N_GROUPS, tn), jnp.bool_)
    for _ in range(TOPK_GROUPS):
        gm = jnp.max(gs, axis=0, keepdims=True)
        first = _first_index(gs == gm, gidx, N_GROUPS)
        hit = gidx == first
        gsel = gsel | hit
        gs = jnp.where(hit, NEG, gs)
    gself = gsel.astype(F32)
    emask = jnp.concatenate(
        [jnp.broadcast_to(gself[g:g + 1, :], (GROUP_SIZE, tn)) for g in range(N_GROUPS)], axis=0)
    cur = jnp.where(emask > 0.5, choice, NEG)
    ids, ws = [], []
    for _ in range(TOP_K):
        m = jnp.max(cur, axis=0, keepdims=True)
        first = _first_index(cur == m, eidx, N_EXPERTS)
        hit = eidx == first
        ids.append(first)
        ws.append(jnp.sum(jnp.where(hit, scores, 0.0), axis=0, keepdims=True))
        cur = jnp.where(hit, NEG, cur)
    wsel = jnp.concatenate(ws, axis=0)
    idx_ref[0] = jnp.concatenate(ids, axis=0)
    w_ref[0] = wsel / jnp.sum(wsel, axis=0, keepdims=True) * ROUTE_SCALE
    fb = f.astype(BF16)
    h = _silu(_dot(fb, ws1_ref[...])) * _dot(fb, ws3_ref[...])
    xs_ref[0] = x + mod[:, 5 * d:6 * d] * _dot(h.astype(BF16), ws2_ref[...])


def _route(x, mod, g2, wrt, eb, ws1, ws3, ws2, n_ctx_tiles):
    b, n, d = x.shape
    tn = TOK_TILE
    nb = b
    ds = ws1.shape[-1]

    def mod_idx(bi, t):
        return (jnp.where(t < n_ctx_tiles, nb, bi), 0, 0)

    tok = lambda bi, t: (bi, t, 0)
    lane_tok = lambda bi, t: (bi, 0, t)
    const = lambda bi, t: (0, 0)
    return pl.pallas_call(
        functools.partial(_route_kernel, d=d),
        grid=(b, n // tn),
        in_specs=[
            pl.BlockSpec((1, tn, d), tok),
            pl.BlockSpec((1, 1, N_MOD * d), mod_idx),
            pl.BlockSpec((1, d), const),
            pl.BlockSpec((N_EXPERTS, d), const),
            pl.BlockSpec((N_EXPERTS, 1), const),
            pl.BlockSpec((d, ds), const),
            pl.BlockSpec((d, ds), const),
            pl.BlockSpec((ds, d), const),
        ],
        out_specs=[
            pl.BlockSpec((1, tn * (d // 128), 128), tok),
            pl.BlockSpec((1, TOP_K, tn), lane_tok),
            pl.BlockSpec((1, TOP_K, tn), lane_tok),
            pl.BlockSpec((1, tn, d), tok),
        ],
        out_shape=[
            jax.ShapeDtypeStruct((b, n * (d // 128), 128), F32),
            jax.ShapeDtypeStruct((b, TOP_K, n), jnp.int32),
            jax.ShapeDtypeStruct((b, TOP_K, n), F32),
            jax.ShapeDtypeStruct((b, n, d), F32),
        ],
        compiler_params=_cparams(("parallel", "parallel")),
        name="route_shared",
    )(x, mod, g2, wrt, eb, ws1, ws3, ws2)


def _expert_kernel(iexp_ref, iblk_ref, iflag_ref, nitems_ref, tok_ref, tok_next_ref, tok_ahead_ref,
                   dst_prev_ref, dst_ref, erow_ref, f_hbm, w1_ref, w3_ref, w2_ref, y_hbm, xbuf, ybuf,
                   wb1, wb3, wb2, gsem, ssem):
    i = pl.program_id(0)
    nitems = nitems_ref[0]
    blk = iblk_ref[i]
    expert = iexp_ref[i]
    first = (iflag_ref[i] & 1) != 0
    last = (iflag_ref[i] & 2) != 0
    ring = xbuf.shape[0]
    slot = blk % ring
    slot_prev = (blk + ring - 1) % ring
    slot_prev2 = (blk + ring - 2) % ring
    slot_ahead = (blk + 2) % ring
    rows = EXPERT_BLOCK
    parts = xbuf.shape[1] // rows
    n_blocks = y_hbm.shape[0] // (rows * parts)

    def gather(idx_ref, s):
        for j in range(rows):
            src = pl.multiple_of(idx_ref[0, 0, j], parts)
            pltpu.make_async_copy(f_hbm.at[pl.ds(src, parts)], xbuf.at[s, pl.ds(j * parts, parts)],
                                  gsem.at[s]).start(priority=j % 2)

    def gather_wait(s):
        pltpu.make_async_copy(f_hbm.at[pl.ds(0, rows * parts)], xbuf.at[s], gsem.at[s]).wait()

    def scatter(idx_ref, s):
        for j in range(rows):
            dst = pl.multiple_of(idx_ref[0, 0, j], parts)
            pltpu.make_async_copy(ybuf.at[s, pl.ds(j * parts, parts)], y_hbm.at[pl.ds(dst, parts)],
                                  ssem.at[s]).start(priority=j % 2)

    def scatter_wait(s):
        pltpu.make_async_copy(ybuf.at[s], y_hbm.at[pl.ds(0, rows * parts)], ssem.at[s]).wait()

    def expert_rows(s, after_up=None, after_down=None):
        xb = jnp.concatenate([xbuf[s, pl.ds(p, rows, stride=parts), :] for p in range(parts)],
                             axis=1).astype(BF16)
        h1 = _dot(xb, wb1[...])
        h3 = _dot(xb, wb3[...])
        if after_up is not None:
            after_up()
        y = _dot((_silu(h1) * h3).astype(BF16), wb2[...])
        if after_down is not None:
            after_down()
        return y

    def store_rows(s, y, accumulate):
        for p in range(parts):
            part = y[:, p * 128:(p + 1) * 128]
            if accumulate:
                part = part + ybuf[s, pl.ds(p, rows, stride=parts), :]
            ybuf[s, pl.ds(p, rows, stride=parts), :] = part

    valid = i < nitems

    @pl.when(valid & ((iflag_ref[i] & 4) != 0))
    def _():
        wb1[...] = w1_ref[0].astype(BF16)
        wb3[...] = w3_ref[0].astype(BF16)
        wb2[...] = w2_ref[0].astype(BF16)

    fast = valid & first & last & (blk >= 3) & (blk <= n_blocks - 3)

    @pl.when(fast)
    def _():
        gather_wait(slot)
        scatter_wait(slot)
        y = expert_rows(slot, after_up=lambda: gather(tok_ahead_ref, slot_ahead),
                        after_down=lambda: scatter(dst_prev_ref, slot_prev))
        store_rows(slot, y, False)

    @pl.when(valid & jnp.logical_not(fast))
    def _():
        @pl.when(first)
        def _():
            @pl.when(blk == 0)
            def _():
                gather(tok_ref, 0)
                if n_blocks > 1:
                    gather(tok_next_ref, 1)

            @pl.when(blk + 2 < n_blocks)
            def _():
                gather(tok_ahead_ref, slot_ahead)

            @pl.when(blk >= 1)
            def _():
                scatter(dst_prev_ref, slot_prev)

            gather_wait(slot)

            @pl.when(blk >= 3)
            def _():
                scatter_wait(slot)

        y = expert_rows(slot) * (erow_ref[0] == expert).astype(F32)

        @pl.when(first)
        def _():
            store_rows(slot, y, False)

        @pl.when(jnp.logical_not(first))
        def _():
            store_rows(slot, y, True)

        @pl.when(last & (blk == n_blocks - 1))
        def _():
            scatter(dst_ref, slot)

            @pl.when(blk >= 2)
            def _():
                scatter_wait(slot_prev2)

            @pl.when(blk >= 1)
            def _():
                scatter_wait(slot_prev)

            scatter_wait(slot)


def _experts(iexp, iblk, iflag, nitems, row_tok, row_dst, erow, f_rows, w1, w3, w2):
    n_items = iexp.shape[0]
    n_blocks = row_tok.shape[0]
    d, de = w1.shape[1], w1.shape[2]
    parts = d // 128
    rows = EXPERT_BLOCK
    last = n_blocks - 1
    cur = lambda i, ie, ib, fl, nt: (ib[i], 0, 0)
    nxt = lambda i, ie, ib, fl, nt: (jnp.minimum(ib[i] + 1, last), 0, 0)
    ahd = lambda i, ie, ib, fl, nt: (jnp.minimum(ib[i] + 2, last), 0, 0)
    prv = lambda i, ie, ib, fl, nt: (jnp.maximum(ib[i] - 1, 0), 0, 0)
    wsel = lambda i, ie, ib, fl, nt: (ie[i], 0, 0)
    grid_spec = pltpu.PrefetchScalarGridSpec(
        num_scalar_prefetch=4,
        grid=(n_items,),
        in_specs=[
            pl.BlockSpec((1, 1, rows), cur, memory_space=pltpu.SMEM),
            pl.BlockSpec((1, 1, rows), nxt, memory_space=pltpu.SMEM),
            pl.BlockSpec((1, 1, rows), ahd, memory_space=pltpu.SMEM),
            pl.BlockSpec((1, 1, rows), prv, memory_space=pltpu.SMEM),
            pl.BlockSpec((1, 1, rows), cur, memory_space=pltpu.SMEM),
            pl.BlockSpec((1, rows, 1), cur),
            pl.BlockSpec(memory_space=pl.ANY),
            pl.BlockSpec((1, d, de), wsel),
            pl.BlockSpec((1, d, de), wsel),
            pl.BlockSpec((1, de, d), wsel),
        ],
        out_specs=pl.BlockSpec(memory_space=pl.ANY),
        scratch_shapes=[
            pltpu.VMEM((EXPERT_RING, rows * parts, 128), F32),
            pltpu.VMEM((EXPERT_RING, rows * parts, 128), F32),
            pltpu.VMEM((d, de), BF16),
            pltpu.VMEM((d, de), BF16),
            pltpu.VMEM((de, d), BF16),
            pltpu.SemaphoreType.DMA((EXPERT_RING,)),
            pltpu.SemaphoreType.DMA((EXPERT_RING,)),
        ],
    )
    return pl.pallas_call(
        _expert_kernel,
        grid_spec=grid_spec,
        out_shape=jax.ShapeDtypeStruct((n_blocks * rows * parts, 128), F32),
        compiler_params=_cparams(("arbitrary",)),
        name="experts",
    )(iexp, iblk, iflag, nitems, row_tok, row_tok, row_tok, row_dst, row_dst, erow, f_rows, w1, w3, w2)


ASSIGN_BITS = 20


def _routing_tables(idx, n_tokens, parts):
    n_assign = n_tokens * TOP_K
    blk = EXPERT_BLOCK
    assert n_assign % blk == 0 and n_assign <= (1 << ASSIGN_BITS)
    n_blocks = n_assign // blk
    n_items = n_blocks + N_EXPERTS
    flat_e = idx.reshape(-1).astype(jnp.int32)
    key = jnp.sort((flat_e << ASSIGN_BITS) | jnp.arange(n_assign, dtype=jnp.int32))
    e_sorted = key >> ASSIGN_BITS
    order = key & ((1 << ASSIGN_BITS) - 1)
    tok = order // TOP_K
    row_dst = ((order % TOP_K) * n_tokens + tok) * parts
    tok = tok * parts

    experts = jnp.arange(N_EXPERTS, dtype=jnp.int32)
    counts = jnp.sum((flat_e.reshape(-1, 1, blk) == experts[None, :, None]).astype(jnp.int32), axis=(0, 2))
    ends = jnp.cumsum(counts)
    starts = ends - counts
    first_blk = starts // blk
    n_be = jnp.where(ends > starts, (ends - 1) // blk - first_blk + 1, 0)
    item_end = jnp.cumsum(n_be)
    item_off = item_end - n_be
    total = item_end[-1]
    i = jnp.arange(n_items, dtype=jnp.int32)
    iexp = jnp.minimum(jnp.sum((item_end[None, :] <= i[:, None]).astype(jnp.int32), axis=1), N_EXPERTS - 1)
    pick = (iexp[:, None] == experts[None, :]).astype(jnp.int32)
    base = jnp.sum(pick * (first_blk - item_off)[None, :], axis=1)
    iblk = jnp.where(i < total, base + i, n_blocks - 1).astype(jnp.int32)
    prev_blk = jnp.concatenate([jnp.full((1,), -1, jnp.int32), iblk[:-1]])
    next_blk = jnp.concatenate([iblk[1:], jnp.full((1,), -1, jnp.int32)])
    is_first = iblk != prev_blk
    is_last = (iblk != next_blk) | (i == total - 1)
    new_expert = iexp != jnp.concatenate([jnp.full((1,), -1, jnp.int32), iexp[:-1]])
    iflag = is_first.astype(jnp.int32) + 2 * is_last.astype(jnp.int32) + 4 * new_expert.astype(jnp.int32)
    return (iexp, iblk, iflag, total.astype(jnp.int32).reshape(1), tok.reshape(n_blocks, 1, blk),
            row_dst.reshape(n_blocks, 1, blk), e_sorted.reshape(n_blocks, blk, 1))


def _combine_kernel(xs_ref, mod_ref, w_ref, *refs, d):
    y_refs, o_ref = refs[:TOP_K], refs[TOP_K]
    w = w_ref[0]
    tn = w.shape[0]
    parts = d // 128
    wk = [jnp.broadcast_to(w[:, k:k + 1], (tn, 128)) for k in range(TOP_K)]
    for p in range(parts):
        tot = y_refs[0][pl.ds(p, tn, stride=parts), :] * wk[0]
        for k in range(1, TOP_K):
            tot = tot + y_refs[k][pl.ds(p, tn, stride=parts), :] * wk[k]
        sl = slice(p * 128, (p + 1) * 128)
        o_ref[0, :, sl] = xs_ref[0, :, sl] + mod_ref[0][:, 5 * d + p * 128:5 * d + (p + 1) * 128] * tot


def _combine(xs, mod, w, y, n_ctx_tiles):
    b, n, d = xs.shape
    tn = TOK_TILE
    nb = b
    tiles = n // tn
    per_slot = b * tiles

    def mod_idx(bi, t):
        return (jnp.where(t < n_ctx_tiles, nb, bi), 0, 0)

    tok = lambda bi, t: (bi, t, 0)
    y_specs = [pl.BlockSpec((tn * (d // 128), 128),
                            functools.partial(lambda bi, t, k: (k * per_slot + bi * tiles + t, 0), k=k))
               for k in range(TOP_K)]
    return pl.pallas_call(
        functools.partial(_combine_kernel, d=d),
        grid=(b, tiles),
        in_specs=[pl.BlockSpec((1, tn, d), tok), pl.BlockSpec((1, 1, N_MOD * d), mod_idx),
                  pl.BlockSpec((1, tn, TOP_K), tok)] + y_specs,
        out_specs=pl.BlockSpec((1, tn, d), tok),
        out_shape=jax.ShapeDtypeStruct((b, n, d), F32),
        compiler_params=_cparams(("parallel", "parallel")),
        name="combine",
    )(xs, mod, w, *([y] * TOP_K))


def _final_kernel(x_ref, g_ref, o_ref):
    o_ref[0] = _rms_rows(x_ref[0], g_ref[...])


def _final_norm(x, g, n_ctx_tiles):
    b, n, d = x.shape
    tn = TOK_TILE
    n_lat_tiles = n // tn - n_ctx_tiles
    return pl.pallas_call(
        _final_kernel,
        grid=(b, n_lat_tiles),
        in_specs=[pl.BlockSpec((1, tn, d), lambda bi, t: (bi, t + n_ctx_tiles, 0)),
                  pl.BlockSpec((1, d), lambda bi, t: (0, 0))],
        out_specs=pl.BlockSpec((1, tn, d), lambda bi, t: (bi, t, 0)),
        out_shape=jax.ShapeDtypeStruct((b, n_lat_tiles * tn, d), F32),
        compiler_params=_cparams(("parallel", "parallel")),
        name="final_norm",
    )(x, g)


def _rotary_column_tables():
    a_q, a_kv = A_HEADS * HEAD_DIM, A_KV * HEAD_DIM
    b_q, b_kv = B_HEADS * HEAD_DIM, B_KV * HEAD_DIM
    c_qk, c_v = 2 * C_HEADS * HEAD_DIM, C_HEADS * 2 * HEAD_DIM
    sizes = (a_q, a_kv, a_kv, b_q, b_kv, b_kv, c_qk, c_qk, c_v)
    off = np.concatenate([[0], np.cumsum(sizes)])
    seg = lambda i: np.arange(off[i], off[i + 1])
    perm_r = np.concatenate([seg(0), seg(1), seg(3), seg(4), seg(6), seg(7)])
    perm_v = np.concatenate([seg(2), seg(5), seg(8)])
    col = np.arange(R_W)
    i = col % HEAD_DIM
    partner_local = np.where((i % (2 * ROPE_FREQS)) < ROPE_FREQS, i + ROPE_FREQS, i - ROPE_FREQS)
    partner = col - i + partner_local
    return perm_r, perm_v, partner


def _rope_tables(n_ctx, n_lat):
    t = jnp.arange(n_lat, dtype=jnp.int32)
    row_pos = (t // GRID_W).astype(F32)
    col_pos = (t % GRID_W).astype(F32)
    inv_freq = jnp.power(ROPE_THETA, -jnp.arange(ROPE_FREQS, dtype=F32) / ROPE_FREQS)
    ang_r = row_pos[:, None] * inv_freq
    ang_c = col_pos[:, None] * inv_freq
    cos64 = jnp.concatenate([jnp.cos(ang_r), jnp.cos(ang_r), jnp.cos(ang_c), jnp.cos(ang_c)], axis=1)
    sin64 = jnp.concatenate([-jnp.sin(ang_r), jnp.sin(ang_r), -jnp.sin(ang_c), jnp.sin(ang_c)], axis=1)
    cos64 = jnp.concatenate([jnp.ones((n_ctx, HEAD_DIM), F32), cos64], axis=0)
    sin64 = jnp.concatenate([jnp.zeros((n_ctx, HEAD_DIM), F32), sin64], axis=0)
    return jnp.tile(cos64, (1, 2)), jnp.tile(sin64, (1, 2))


def _heads_major(t, n_heads):
    b, n, w = t.shape
    return t.reshape(b, n, n_heads, w // n_heads).transpose(0, 2, 1, 3)


def _heads_transposed(t, n_heads):
    b, n, w = t.shape
    return t.reshape(b, n, n_heads, w // n_heads).transpose(0, 2, 3, 1)


def _tokens_major(t):
    b, h, n, dv = t.shape
    return t.transpose(0, 2, 1, 3).reshape(b, n, h * dv)


def kernel(x, c, ctx, c_ctx, w_mod, b_mod, g_norm1, w_qkv, g_qnorm_a, g_knorm_a, sink_b, lam_q1, lam_k1, lam_q2, lam_k2, g_subln_c, w_br_a, w_br_b, w_br_c, w_gate, b_gate, w_out, g_norm2, w_router, e_bias, w1, w3, w2, ws1, ws3, ws2, g_final):
    bsz, n_lat, d = x.shape
    n_ctx = ctx.shape[1]
    depth = w_mod.shape[0]
    n_tok = n_ctx + n_lat
    assert n_ctx % TOK_TILE == 0 and n_lat % KEY_CHUNK == 0 and n_ctx % 128 == 0
    n_ctx_tiles = n_ctx // TOK_TILE
    n_all = bsz * n_tok

    rows = -(-(bsz + 1) // 8) * 8
    cvec = jnp.concatenate([c, c_ctx[None, :], jnp.zeros((rows - bsz - 1, d), F32)], axis=0)
    mod_all = _modulation(cvec, w_mod, b_mod)

    perm_r, perm_v, partner = _rotary_column_tables()
    cos_t, sin_t = _rope_tables(n_ctx, n_lat)
    head_of = np.arange(R_A) // HEAD_DIM
    ones_blk = jnp.asarray((head_of[:, None] == head_of[None, :]).astype(np.float32), BF16)
    scale = HEAD_DIM ** -0.5 * LOG2E
    unit = jnp.ones((HEAD_DIM,), F32)

    xs = jnp.concatenate([ctx, x], axis=1)
    for l in range(depth):
        lam_init = 0.8 - 0.6 * math.exp(-0.3 * l)
        lam = (jnp.exp(jnp.dot(lam_q1[l], lam_k1[l])) - jnp.exp(jnp.dot(lam_q2[l], lam_k2[l]))).astype(F32) + lam_init
        mod = mod_all[l].reshape(rows, 1, N_MOD * d)
        g1 = g_norm1[l].reshape(1, d)

        w_r = w_qkv[l][:, perm_r]
        wqkv = jnp.concatenate([w_r, w_qkv[l][:, perm_v]], axis=1).astype(BF16)
        wsw = w_r[:, partner].astype(BF16)
        grow = jnp.concatenate([jnp.tile(g_qnorm_a[l] * scale, A_HEADS), jnp.tile(g_knorm_a[l], A_KV),
                                jnp.tile(unit * scale, B_HEADS), jnp.tile(unit, B_KV),
                                jnp.tile(unit * scale, 2 * C_HEADS), jnp.tile(unit, 2 * C_HEADS)])
        gsrow = grow[partner]
        r_all, v_all = _pre_attention(xs, mod, g1, wqkv, wsw, ones_blk, grow.reshape(1, R_W),
                                      gsrow.reshape(1, R_W), cos_t, sin_t, n_ctx_tiles)

        o = 0
        qa = _heads_major(r_all[..., o:o + A_HEADS * HEAD_DIM], A_HEADS); o += A_HEADS * HEAD_DIM
        kta = _heads_transposed(r_all[..., o:o + A_KV * HEAD_DIM], A_KV); o += A_KV * HEAD_DIM
        qb = _heads_major(r_all[..., o:o + B_HEADS * HEAD_DIM], B_HEADS); o += B_HEADS * HEAD_DIM
        ktb = _heads_transposed(r_all[..., o:o + B_KV * HEAD_DIM], B_KV); o += B_KV * HEAD_DIM
        qc = _heads_major(r_all[..., o:o + 2 * C_HEADS * HEAD_DIM], 2 * C_HEADS); o += 2 * C_HEADS * HEAD_DIM
        ktc = _heads_transposed(r_all[..., o:o + 2 * C_HEADS * HEAD_DIM], 2 * C_HEADS)
        o = 0
        va = _heads_major(v_all[..., o:o + A_KV * HEAD_DIM], A_KV); o += A_KV * HEAD_DIM
        vb = _heads_major(v_all[..., o:o + B_KV * HEAD_DIM], B_KV); o += B_KV * HEAD_DIM
        vc = _heads_major(v_all[..., o:], C_HEADS)

        scal = jnp.concatenate([sink_b[l].astype(F32) * LOG2E, lam.reshape(1), jnp.full((1,), 1.0 - lam_init, F32)])
        scal_c = jnp.concatenate([jnp.zeros((DIFF_HEADS_PER_STEP,), F32), lam.reshape(1),
                                  jnp.full((1,), 1.0 - lam_init, F32)])
        gsub = g_subln_c[l].reshape(1, 2 * HEAD_DIM)
        g64 = jnp.ones((1, HEAD_DIM), F32)
        ya = _attention(scal, g64, qa, kta, va, heads=A_HEADS // A_KV, hpv=A_HEADS // A_KV, shared_k=True,
                        mode="global", n_ctx=n_ctx)
        yb = _attention(scal, g64, qb, ktb, vb, heads=B_HEADS // B_KV, hpv=B_HEADS // B_KV, shared_k=True,
                        mode="window", n_ctx=n_ctx)
        yc = _attention(scal_c, gsub, qc, ktc, vc, heads=DIFF_HEADS_PER_STEP, hpv=2, shared_k=False,
                        mode="diff", n_ctx=n_ctx)

        x1 = _merge(xs, mod, g1, _tokens_major(ya), _tokens_major(yb), _tokens_major(yc),
                    w_gate[l].astype(BF16), b_gate[l].reshape(1, -1), w_br_a[l].astype(BF16),
                    w_br_b[l].astype(BF16), w_br_c[l].astype(BF16), w_out[l].astype(BF16), n_ctx_tiles)

        f, idx_t, w_t, x_sh = _route(x1, mod, g_norm2[l].reshape(1, d), w_router[l].T,
                                     e_bias[l].reshape(N_EXPERTS, 1), ws1[l].astype(BF16),
                                     ws3[l].astype(BF16), ws2[l].astype(BF16), n_ctx_tiles)

        idx = idx_t.transpose(0, 2, 1).reshape(n_all, TOP_K)
        iexp, iblk, iflag, nitems, row_tok, row_dst, erow = _routing_tables(idx, n_all, d // 128)
        y = _experts(iexp, iblk, iflag, nitems, row_tok, row_dst, erow, f.reshape(n_all * (d // 128), 128),
                     w1[l], w3[l], w2[l])
        xs = _combine(x_sh, mod, w_t.transpose(0, 2, 1), y, n_ctx_tiles)

    return _final_norm(xs, g_final.reshape(1, d), n_ctx_tiles)
```

```python
import functools
import math

import numpy as np
import jax
import jax.numpy as jnp
from jax import lax
from jax.experimental import pallas as pl
from jax.experimental.pallas import tpu as pltpu

F32 = jnp.float32
BF16 = jnp.bfloat16

HEAD_DIM = 64
ROPE_FREQS = HEAD_DIM // 4
ROPE_THETA = 10000.0
GRID_W = 64
WINDOW = 128
A_HEADS, A_KV = 8, 2
B_HEADS, B_KV = 8, 2
C_HEADS = 4
N_EXPERTS = 128
TOP_K = 8
N_GROUPS = 8
TOPK_GROUPS = 4
GROUP_SIZE = N_EXPERTS // N_GROUPS
ROUTE_SCALE = 2.5
EXPERT_BLOCK = 128
EXPERT_RING = 3
N_MOD = 6
EPS = 1e-6
NEG = -1e30

R_A = A_HEADS * HEAD_DIM + A_KV * HEAD_DIM
R_B = B_HEADS * HEAD_DIM + B_KV * HEAD_DIM
R_C = 4 * C_HEADS * HEAD_DIM
R_W = R_A + R_B + R_C
V_W = A_KV * HEAD_DIM + B_KV * HEAD_DIM + C_HEADS * 2 * HEAD_DIM

TOK_TILE = 256
ATTN_TQ = 256
DIFF_HEADS_PER_STEP = 4
KEY_CHUNK = 512
MAX_KEY_CHUNK = {"global": 8448, "diff": 8448, "window": 128}
LOG2E = math.log2(math.e)
VMEM_LIMIT = 56 * 1024 * 1024


def _cparams(sem, **kw):
    return pltpu.CompilerParams(dimension_semantics=sem, vmem_limit_bytes=VMEM_LIMIT, **kw)


def _dot(a, b):
    return jnp.dot(a, b, preferred_element_type=F32)


def _sigmoid(x):
    return 1.0 / (1.0 + jnp.exp(-x))


def _silu(x):
    return x * _sigmoid(x)


def _rms_rows(x, g):
    return x * lax.rsqrt(jnp.mean(x * x, axis=-1, keepdims=True) + EPS) * g


def _mod_kernel(c_ref, w_ref, b_ref, o_ref):
    cs = _silu(c_ref[...])
    o_ref[0] = jnp.dot(cs, w_ref[0], preferred_element_type=F32,
                       precision=lax.Precision.HIGHEST) + b_ref[0]


def _modulation(cvec, w_mod, b_mod):
    depth, d, n = w_mod.shape
    rows = cvec.shape[0]
    bn = 1536
    return pl.pallas_call(
        _mod_kernel,
        grid=(depth, n // bn),
        in_specs=[
            pl.BlockSpec((rows, d), lambda l, j: (0, 0)),
            pl.BlockSpec((1, d, bn), lambda l, j: (l, 0, j)),
            pl.BlockSpec((1, 1, bn), lambda l, j: (l, 0, j)),
        ],
        out_specs=pl.BlockSpec((1, rows, bn), lambda l, j: (l, 0, j)),
        out_shape=jax.ShapeDtypeStruct((depth, rows, n), F32),
        compiler_params=_cparams(("parallel", "parallel")),
        name="modulation",
    )(cvec, w_mod, b_mod.reshape(depth, 1, n))


def _pre_kernel(x_ref, mod_ref, g1_ref, wqkv_ref, ones_ref, grow_ref, gsrow_ref,
                cos_ref, sin_ref, r_out, v_out, *, d):
    x = x_ref[0]
    mod = mod_ref[0]
    a = (_rms_rows(x, g1_ref[...]) * (1.0 + mod[:, d:2 * d]) + mod[:, 0:d]).astype(BF16)
    p = _dot(a, wqkv_ref[...])
    pa = p[:, :R_A]
    sq = pa * pa
    hi = sq.astype(BF16)
    lo = (sq - hi.astype(F32)).astype(BF16)
    ssq = _dot(hi, ones_ref[...]) + _dot(lo, ones_ref[...])
    rinv = lax.rsqrt(ssq * (1.0 / HEAD_DIM) + EPS)
    cos = cos_ref[...]
    sin = sin_ref[...]
    n_tiles = R_W // 128
    lane = lax.broadcasted_iota(jnp.int32, (p.shape[0], 128), 1)
    low_half = (lane % (2 * ROPE_FREQS)) < ROPE_FREQS
    for j in range(n_tiles):
        sl = slice(j * 128, (j + 1) * 128)
        pj = p[:, sl]
        ps = jnp.where(low_half, pltpu.roll(pj, 128 - ROPE_FREQS, 1), pltpu.roll(pj, ROPE_FREQS, 1))
        o = pj * (grow_ref[:, sl] * cos) + ps * (gsrow_ref[:, sl] * sin)
        if (j + 1) * 128 <= R_A:
            o = o * rinv[:, sl]
        r_out[0, :, sl] = o.astype(BF16)
    v_out[0] = p[:, R_W:].astype(BF16)


def _pre_attention(x, mod, g1, wqkv, ones_blk, grow, gsrow, cos_t, sin_t, n_ctx_tiles):
    b, n, d = x.shape
    tn = TOK_TILE
    nb = b

    def mod_idx(bi, t):
        return (jnp.where(t < n_ctx_tiles, nb, bi), 0, 0)

    return pl.pallas_call(
        functools.partial(_pre_kernel, d=d),
        grid=(b, n // tn),
        in_specs=[
            pl.BlockSpec((1, tn, d), lambda bi, t: (bi, t, 0)),
            pl.BlockSpec((1, 1, N_MOD * d), mod_idx),
            pl.BlockSpec((1, d), lambda bi, t: (0, 0)),
            pl.BlockSpec((d, R_W + V_W), lambda bi, t: (0, 0)),
            pl.BlockSpec((R_A, R_A), lambda bi, t: (0, 0)),
            pl.BlockSpec((1, R_W), lambda bi, t: (0, 0)),
            pl.BlockSpec((1, R_W), lambda bi, t: (0, 0)),
            pl.BlockSpec((tn, 128), lambda bi, t: (t, 0)),
            pl.BlockSpec((tn, 128), lambda bi, t: (t, 0)),
        ],
        out_specs=[
            pl.BlockSpec((1, tn, R_W), lambda bi, t: (bi, t, 0)),
            pl.BlockSpec((1, tn, V_W), lambda bi, t: (bi, t, 0)),
        ],
        out_shape=[
            jax.ShapeDtypeStruct((b, n, R_W), BF16),
            jax.ShapeDtypeStruct((b, n, V_W), BF16),
        ],
        compiler_params=_cparams(("parallel", "parallel")),
        name="pre_attention",
    )(x, mod, g1, wqkv, ones_blk, grow, gsrow, cos_t, sin_t)


def _attn_kernel(sc_ref, gsub_ref, q_ref, kt_ref, v_ref, o_ref, m_sc, acc_sc, *, heads, hpv,
                 shared_k, mode, n_ctx, n_tok, tq, ck, dv):
    u = pl.program_id(1)
    qi = pl.program_id(2)
    is_lat = qi >= n_ctx // tq
    dvx = acc_sc.shape[-1]

    for g in range(heads):
        if mode == "window":
            m_sc[g] = jnp.full((tq, 1), sc_ref[u * heads + g], F32)
            lane = lax.broadcasted_iota(jnp.int32, (tq, dvx), 1)
            acc_sc[g] = jnp.where(lane == dv, 1.0, 0.0).astype(F32)
        else:
            m_sc[g] = jnp.full((tq, 1), NEG, F32)
            acc_sc[g] = jnp.zeros((tq, dvx), F32)

    def probs(g, s, mask):
        if mask is not None:
            s = jnp.where(mask, s, NEG)
        m = m_sc[g]
        m_new = jnp.maximum(m, jnp.max(s, axis=-1, keepdims=True))
        m_sc[g] = m_new
        return jnp.exp2(s - m_new).astype(BF16), jnp.exp2(m - m_new)

    def update_all(kt_of, v_of, mask=None):
        s_next = _dot(q_ref[0, 0], kt_of(0))
        for g in range(heads):
            s = s_next
            if g + 1 < heads:
                s_next = _dot(q_ref[0, g + 1], kt_of(g + 1))
            p, alpha = probs(g, s, mask)
            acc_sc[g] = alpha * acc_sc[g] + _dot(p, v_of(g))

    def step(k0, size):
        update_all(lambda g: kt_ref[0, 0 if shared_k else g, :, pl.ds(k0, size)],
                   lambda g: v_ref[0, g // hpv, pl.ds(k0, size), :])

    if mode == "window":
        span = tq + 2 * WINDOW
        start = pl.multiple_of(jnp.clip(qi * tq - WINDOW, 0, n_tok - span), 128)
        qpos = qi * tq + lax.broadcasted_iota(jnp.int32, (tq, n_ctx + span), 0)
        col = lax.broadcasted_iota(jnp.int32, (tq, n_ctx + span), 1)
        kpos = start + col - n_ctx
        mask = (col < n_ctx) | (is_lat & (kpos >= n_ctx) & (jnp.abs(kpos - qpos) <= WINDOW))
        v = jnp.concatenate([v_ref[0, 0, 0:n_ctx, :], v_ref[0, 0, pl.ds(start, span), :]], axis=0)

        def kt_of(g):
            gk = 0 if shared_k else g
            return jnp.concatenate([kt_ref[0, gk, :, 0:n_ctx], kt_ref[0, gk, :, pl.ds(start, span)]], axis=1)

        update_all(kt_of, lambda g: v, mask)
    else:
        @pl.when(jnp.logical_not(is_lat))
        def _():
            step(0, n_ctx)

        def body(ci, carry):
            step(pl.multiple_of(ci * ck, 128), ck)
            return carry

        lax.fori_loop(0, jnp.where(is_lat, n_tok // ck, 0), body, 0)

    def result(g):
        acc = acc_sc[g]
        return acc[:, 0:dv] / acc[:, dv:dv + 1]

    if mode == "diff":
        lam = sc_ref[heads]
        post = sc_ref[heads + 1]
        for j in range(heads // 2):
            y = result(2 * j) - lam * result(2 * j + 1)
            o_ref[0, j] = (_rms_rows(y, gsub_ref[...]) * post).astype(o_ref.dtype)
    else:
        for g in range(heads):
            o_ref[0, g] = result(g).astype(o_ref.dtype)


def _attention(scalars, gsub, q, kt, v, *, heads, hpv, shared_k, mode, n_ctx):
    b, hq, n, hd = q.shape
    units = hq // heads
    dv = v.shape[-1]
    dvx = (dv // 128 + 1) * 128
    tq = ATTN_TQ
    gk = 1 if shared_k else heads
    vh = heads // hpv
    out_heads = heads // 2 if mode == "diff" else heads
    ck = max(c for c in range(128, MAX_KEY_CHUNK[mode] + 1, 128) if n % c == 0)
    ones_col = (lax.broadcasted_iota(jnp.int32, v.shape[:-1] + (dvx - dv,), v.ndim - 1) == 0).astype(v.dtype)
    vx = jnp.concatenate([v, ones_col], axis=-1)
    kern = functools.partial(_attn_kernel, heads=heads, hpv=hpv, shared_k=shared_k, mode=mode,
                             n_ctx=n_ctx, n_tok=n, tq=tq, ck=ck, dv=dv)
    return pl.pallas_call(
        kern,
        grid=(b, units, n // tq),
        in_specs=[
            pl.BlockSpec(memory_space=pltpu.SMEM),
            pl.BlockSpec((1, dv), lambda bi, u, t: (0, 0)),
            pl.BlockSpec((1, heads, tq, hd), lambda bi, u, t: (bi, u, t, 0)),
            pl.BlockSpec((1, gk, hd, n), lambda bi, u, t: (bi, u, 0, 0)),
            pl.BlockSpec((1, vh, n, dvx), lambda bi, u, t: (bi, u, 0, 0)),
        ],
        out_specs=pl.BlockSpec((1, out_heads, tq, dv), lambda bi, u, t: (bi, u, t, 0)),
        out_shape=jax.ShapeDtypeStruct((b, units * out_heads, n, dv), BF16),
        scratch_shapes=[
            pltpu.VMEM((heads, tq, 1), F32),
            pltpu.VMEM((heads, tq, dvx), F32),
        ],
        compiler_params=_cparams(("parallel", "parallel", "arbitrary")),
        name="attn_" + mode,
    )(scalars, gsub, q, kt, vx)


def _merge_kernel(x_ref, mod_ref, g1_ref, ya_ref, yb_ref, yc_ref, wg_ref, bg_ref, wa_ref, wb_ref,
                  wc_ref, wo_ref, o_ref, *, d):
    x = x_ref[0]
    mod = mod_ref[0]
    a = (_rms_rows(x, g1_ref[...]) * (1.0 + mod[:, d:2 * d]) + mod[:, 0:d]).astype(BF16)
    gate = _sigmoid(_dot(a, wg_ref[...]) + bg_ref[...])
    m = (gate[:, 0:d] * _dot(ya_ref[0], wa_ref[...])
         + gate[:, d:2 * d] * _dot(yb_ref[0], wb_ref[...])
         + gate[:, 2 * d:3 * d] * _dot(yc_ref[0], wc_ref[...]))
    mix = _dot(m.astype(BF16), wo_ref[...])
    o_ref[0] = x + mod[:, 2 * d:3 * d] * mix


def _merge(x, mod, g1, ya, yb, yc, wg, bg, wa, wb, wc, wo, n_ctx_tiles):
    b, n, d = x.shape
    tn = TOK_TILE
    nb = b
    yw = ya.shape[-1]

    def mod_idx(bi, t):
        return (jnp.where(t < n_ctx_tiles, nb, bi), 0, 0)

    tok = lambda bi, t: (bi, t, 0)
    const = lambda bi, t: (0, 0)
    return pl.pallas_call(
        functools.partial(_merge_kernel, d=d),
        grid=(b, n // tn),
        in_specs=[
            pl.BlockSpec((1, tn, d), tok),
            pl.BlockSpec((1, 1, N_MOD * d), mod_idx),
            pl.BlockSpec((1, d), const),
            pl.BlockSpec((1, tn, yw), tok),
            pl.BlockSpec((1, tn, yw), tok),
            pl.BlockSpec((1, tn, yw), tok),
            pl.BlockSpec((d, 3 * d), const),
            pl.BlockSpec((1, 3 * d), const),
            pl.BlockSpec((yw, d), const),
            pl.BlockSpec((yw, d), const),
            pl.BlockSpec((yw, d), const),
            pl.BlockSpec((d, d), const),
        ],
        out_specs=pl.BlockSpec((1, tn, d), tok),
        out_shape=jax.ShapeDtypeStruct((b, n, d), F32),
        compiler_params=_cparams(("parallel", "parallel")),
        name="merge",
    )(x, mod, g1, ya, yb, yc, wg, bg, wa, wb, wc, wo)


def _first_index(hit, idx, big):
    return jnp.min(jnp.where(hit, idx, big), axis=0, keepdims=True)


def _route_kernel(x_ref, mod_ref, g2_ref, wrt_ref, eb_ref, ws1_ref, ws3_ref, ws2_ref,
                  f_ref, idx_ref, w_ref, xs_ref, cnt_ref, *, d):
    x = x_ref[0]
    mod = mod_ref[0]
    f = _rms_rows(x, g2_ref[...]) * (1.0 + mod[:, 4 * d:5 * d]) + mod[:, 3 * d:4 * d]
    tn = f.shape[0]
    for s in range(d // 128):
        f_ref[0, pl.ds(s, tn, stride=d // 128), :] = f[:, s * 128:(s + 1) * 128]
    logits = lax.dot_general(wrt_ref[...], f, (((1,), (1,)), ((), ())),
                             preferred_element_type=F32, precision=lax.Precision.HIGHEST)
    scores = _sigmoid(logits)
    choice = scores + eb_ref[...]
    eidx = lax.broadcasted_iota(jnp.int32, (N_EXPERTS, tn), 0)
    lidx = lax.broadcasted_iota(jnp.int32, (GROUP_SIZE, tn), 0)
    gscore = []
    for g in range(N_GROUPS):
        cg = choice[g * GROUP_SIZE:(g + 1) * GROUP_SIZE, :]
        m1 = jnp.max(cg, axis=0, keepdims=True)
        first = _first_index(cg == m1, lidx, GROUP_SIZE)
        m2 = jnp.max(jnp.where(lidx == first, NEG, cg), axis=0, keepdims=True)
        gscore.append(m1 + m2)
    gs = jnp.concatenate(gscore, axis=0)
    gidx = lax.broadcasted_iota(jnp.int32, (N_GROUPS, tn), 0)
    gsel = jnp.zeros((N_GROUPS, tn), jnp.bool_)
    for _ in range(TOPK_GROUPS):
        gm = jnp.max(gs, axis=0, keepdims=True)
        first = _first_index(gs == gm, gidx, N_GROUPS)
        hit = gidx == first
        gsel = gsel | hit
        gs = jnp.where(hit, NEG, gs)
    gself = gsel.astype(F32)
    emask = jnp.concatenate(
        [jnp.broadcast_to(gself[g:g + 1, :], (GROUP_SIZE, tn)) for g in range(N_GROUPS)], axis=0)
    cur = jnp.where(emask > 0.5, choice, NEG)
    ids, ws = [], []
    chosen = jnp.zeros((N_EXPERTS, tn), F32)
    for _ in range(TOP_K):
        m = jnp.max(cur, axis=0, keepdims=True)
        first = _first_index(cur == m, eidx, N_EXPERTS)
        hit = eidx == first
        ids.append(first)
        ws.append(jnp.sum(jnp.where(hit, scores, 0.0), axis=0, keepdims=True))
        cur = jnp.where(hit, NEG, cur)
        chosen = chosen + hit.astype(F32)
    wsel = jnp.concatenate(ws, axis=0)
    idx_ref[0] = jnp.concatenate(ids, axis=0)
    cnt_ref[0, 0] = jnp.sum(chosen, axis=1, keepdims=True).astype(jnp.int32)
    w_ref[0] = wsel / jnp.sum(wsel, axis=0, keepdims=True) * ROUTE_SCALE
    fb = f.astype(BF16)
    h = _silu(_dot(fb, ws1_ref[...])) * _dot(fb, ws3_ref[...])
    xs_ref[0] = x + mod[:, 5 * d:6 * d] * _dot(h.astype(BF16), ws2_ref[...])


def _route(x, mod, g2, wrt, eb, ws1, ws3, ws2, n_ctx_tiles):
    b, n, d = x.shape
    tn = TOK_TILE
    nb = b
    ds = ws1.shape[-1]

    def mod_idx(bi, t):
        return (jnp.where(t < n_ctx_tiles, nb, bi), 0, 0)

    tok = lambda bi, t: (bi, t, 0)
    lane_tok = lambda bi, t: (bi, 0, t)
    const = lambda bi, t: (0, 0)
    return pl.pallas_call(
        functools.partial(_route_kernel, d=d),
        grid=(b, n // tn),
        in_specs=[
            pl.BlockSpec((1, tn, d), tok),
            pl.BlockSpec((1, 1, N_MOD * d), mod_idx),
            pl.BlockSpec((1, d), const),
            pl.BlockSpec((N_EXPERTS, d), const),
            pl.BlockSpec((N_EXPERTS, 1), const),
            pl.BlockSpec((d, ds), const),
            pl.BlockSpec((d, ds), const),
            pl.BlockSpec((ds, d), const),
        ],
        out_specs=[
            pl.BlockSpec((1, tn * (d // 128), 128), tok),
            pl.BlockSpec((1, TOP_K, tn), lane_tok),
            pl.BlockSpec((1, TOP_K, tn), lane_tok),
            pl.BlockSpec((1, tn, d), tok),
            pl.BlockSpec((1, 1, N_EXPERTS, 1), lambda bi, t: (bi, t, 0, 0)),
        ],
        out_shape=[
            jax.ShapeDtypeStruct((b, n * (d // 128), 128), F32),
            jax.ShapeDtypeStruct((b, TOP_K, n), jnp.int32),
            jax.ShapeDtypeStruct((b, TOP_K, n), F32),
            jax.ShapeDtypeStruct((b, n, d), F32),
            jax.ShapeDtypeStruct((b, n // tn, N_EXPERTS, 1), jnp.int32),
        ],
        compiler_params=_cparams(("parallel", "parallel")),
        name="route_shared",
    )(x, mod, g2, wrt, eb, ws1, ws3, ws2)


def _expert_kernel(iexp_ref, iblk_ref, iflag_ref, nitems_ref, tok_ref, tok_next_ref, tok_ahead_ref,
                   dst_prev_ref, dst_ref, erow_ref, f_hbm, w1_ref, w3_ref, w2_ref, y_hbm, xbuf, ybuf,
                   wb1, wb3, wb2, gsem, ssem):
    i = pl.program_id(0)
    nitems = nitems_ref[0]
    blk = iblk_ref[i]
    expert = iexp_ref[i]
    first = (iflag_ref[i] & 1) != 0
    last = (iflag_ref[i] & 2) != 0
    ring = xbuf.shape[0]
    slot = blk % ring
    slot_prev = (blk + ring - 1) % ring
    slot_prev2 = (blk + ring - 2) % ring
    slot_ahead = (blk + 2) % ring
    rows = EXPERT_BLOCK
    parts = xbuf.shape[1] // rows
    n_blocks = y_hbm.shape[0] // (rows * parts)

    def gather(idx_ref, s):
        for j in range(rows):
            src = pl.multiple_of(idx_ref[0, 0, j], parts)
            pltpu.make_async_copy(f_hbm.at[pl.ds(src, parts)], xbuf.at[s, pl.ds(j * parts, parts)],
                                  gsem.at[s]).start(priority=j % 2)

    def gather_wait(s):
        pltpu.make_async_copy(f_hbm.at[pl.ds(0, rows * parts)], xbuf.at[s], gsem.at[s]).wait()

    def scatter(idx_ref, s):
        for j in range(rows):
            dst = pl.multiple_of(idx_ref[0, 0, j], parts)
            pltpu.make_async_copy(ybuf.at[s, pl.ds(j * parts, parts)], y_hbm.at[pl.ds(dst, parts)],
                                  ssem.at[s]).start(priority=j % 2)

    def scatter_wait(s):
        pltpu.make_async_copy(ybuf.at[s], y_hbm.at[pl.ds(0, rows * parts)], ssem.at[s]).wait()

    def expert_rows(s, after_up=None, after_down=None):
        xb = jnp.concatenate([xbuf[s, pl.ds(p, rows, stride=parts), :] for p in range(parts)],
                             axis=1).astype(BF16)
        h1 = _dot(xb, wb1[...])
        h3 = _dot(xb, wb3[...])
        if after_up is not None:
            after_up()
        y = _dot((_silu(h1) * h3).astype(BF16), wb2[...])
        if after_down is not None:
            after_down()
        return y

    def store_rows(s, y, accumulate):
        for p in range(parts):
            part = y[:, p * 128:(p + 1) * 128]
            if accumulate:
                part = part + ybuf[s, pl.ds(p, rows, stride=parts), :]
            ybuf[s, pl.ds(p, rows, stride=parts), :] = part

    valid = i < nitems

    @pl.when(valid & ((iflag_ref[i] & 4) != 0))
    def _():
        wb1[...] = w1_ref[0].astype(BF16)
        wb3[...] = w3_ref[0].astype(BF16)
        wb2[...] = w2_ref[0].astype(BF16)

    fast = valid & first & last & (blk >= 3) & (blk <= n_blocks - 3)

    @pl.when(fast)
    def _():
        gather_wait(slot)
        scatter_wait(slot)
        y = expert_rows(slot, after_up=lambda: gather(tok_ahead_ref, slot_ahead),
                        after_down=lambda: scatter(dst_prev_ref, slot_prev))
        store_rows(slot, y, False)

    @pl.when(valid & jnp.logical_not(fast))
    def _():
        @pl.when(first)
        def _():
            @pl.when(blk == 0)
            def _():
                gather(tok_ref, 0)
                if n_blocks > 1:
                    gather(tok_next_ref, 1)

            @pl.when(blk + 2 < n_blocks)
            def _():
                gather(tok_ahead_ref, slot_ahead)

            @pl.when(blk >= 1)
            def _():
                scatter(dst_prev_ref, slot_prev)

            gather_wait(slot)

            @pl.when(blk >= 3)
            def _():
                scatter_wait(slot)

        y = expert_rows(slot) * (erow_ref[0] == expert).astype(F32)

        @pl.when(first)
        def _():
            store_rows(slot, y, False)

        @pl.when(jnp.logical_not(first))
        def _():
            store_rows(slot, y, True)

        @pl.when(last & (blk == n_blocks - 1))
        def _():
            scatter(dst_ref, slot)

            @pl.when(blk >= 2)
            def _():
                scatter_wait(slot_prev2)

            @pl.when(blk >= 1)
            def _():
                scatter_wait(slot_prev)

            scatter_wait(slot)


def _experts(iexp, iblk, iflag, nitems, row_tok, row_dst, erow, f_rows, w1, w3, w2):
    n_items = iexp.shape[0]
    n_blocks = row_tok.shape[0]
    d, de = w1.shape[1], w1.shape[2]
    parts = d // 128
    rows = EXPERT_BLOCK
    last = n_blocks - 1
    cur = lambda i, ie, ib, fl, nt: (ib[i], 0, 0)
    nxt = lambda i, ie, ib, fl, nt: (jnp.minimum(ib[i] + 1, last), 0, 0)
    ahd = lambda i, ie, ib, fl, nt: (jnp.minimum(ib[i] + 2, last), 0, 0)
    prv = lambda i, ie, ib, fl, nt: (jnp.maximum(ib[i] - 1, 0), 0, 0)
    wsel = lambda i, ie, ib, fl, nt: (ie[i], 0, 0)
    grid_spec = pltpu.PrefetchScalarGridSpec(
        num_scalar_prefetch=4,
        grid=(n_items,),
        in_specs=[
            pl.BlockSpec((1, 1, rows), cur, memory_space=pltpu.SMEM),
            pl.BlockSpec((1, 1, rows), nxt, memory_space=pltpu.SMEM),
            pl.BlockSpec((1, 1, rows), ahd, memory_space=pltpu.SMEM),
            pl.BlockSpec((1, 1, rows), prv, memory_space=pltpu.SMEM),
            pl.BlockSpec((1, 1, rows), cur, memory_space=pltpu.SMEM),
            pl.BlockSpec((1, rows, 1), cur),
            pl.BlockSpec(memory_space=pl.ANY),
            pl.BlockSpec((1, d, de), wsel),
            pl.BlockSpec((1, d, de), wsel),
            pl.BlockSpec((1, de, d), wsel),
        ],
        out_specs=pl.BlockSpec(memory_space=pl.ANY),
        scratch_shapes=[
            pltpu.VMEM((EXPERT_RING, rows * parts, 128), F32),
            pltpu.VMEM((EXPERT_RING, rows * parts, 128), F32),
            pltpu.VMEM((d, de), BF16),
            pltpu.VMEM((d, de), BF16),
            pltpu.VMEM((de, d), BF16),
            pltpu.SemaphoreType.DMA((EXPERT_RING,)),
            pltpu.SemaphoreType.DMA((EXPERT_RING,)),
        ],
    )
    return pl.pallas_call(
        _expert_kernel,
        grid_spec=grid_spec,
        out_shape=jax.ShapeDtypeStruct((n_blocks * rows * parts, 128), F32),
        compiler_params=_cparams(("arbitrary",)),
        name="experts",
    )(iexp, iblk, iflag, nitems, row_tok, row_tok, row_tok, row_dst, row_dst, erow, f_rows, w1, w3, w2)


ASSIGN_BITS = 20


def _routing_tables(idx, counts, n_tokens, parts):
    n_assign = n_tokens * TOP_K
    blk = EXPERT_BLOCK
    assert n_assign % blk == 0 and n_assign <= (1 << ASSIGN_BITS)
    n_blocks = n_assign // blk
    n_items = n_blocks + N_EXPERTS
    flat_e = idx.reshape(-1).astype(jnp.int32)
    key = jnp.sort((flat_e << ASSIGN_BITS) | jnp.arange(n_assign, dtype=jnp.int32))
    e_sorted = key >> ASSIGN_BITS
    order = key & ((1 << ASSIGN_BITS) - 1)
    tok = order // TOP_K
    row_dst = ((order % TOP_K) * n_tokens + tok) * parts
    tok = tok * parts

    experts = jnp.arange(N_EXPERTS, dtype=jnp.int32)
    ends = jnp.cumsum(counts)
    starts = ends - counts
    first_blk = starts // blk
    n_be = jnp.where(ends > starts, (ends - 1) // blk - first_blk + 1, 0)
    item_end = jnp.cumsum(n_be)
    item_off = item_end - n_be
    total = item_end[-1]
    i = jnp.arange(n_items, dtype=jnp.int32)
    iexp = jnp.minimum(jnp.sum((item_end[None, :] <= i[:, None]).astype(jnp.int32), axis=1), N_EXPERTS - 1)
    pick = (iexp[:, None] == experts[None, :]).astype(jnp.int32)
    base = jnp.sum(pick * (first_blk - item_off)[None, :], axis=1)
    iblk = jnp.where(i < total, base + i, n_blocks - 1).astype(jnp.int32)
    prev_blk = jnp.concatenate([jnp.full((1,), -1, jnp.int32), iblk[:-1]])
    next_blk = jnp.concatenate([iblk[1:], jnp.full((1,), -1, jnp.int32)])
    is_first = iblk != prev_blk
    is_last = (iblk != next_blk) | (i == total - 1)
    new_expert = iexp != jnp.concatenate([jnp.full((1,), -1, jnp.int32), iexp[:-1]])
    iflag = is_first.astype(jnp.int32) + 2 * is_last.astype(jnp.int32) + 4 * new_expert.astype(jnp.int32)
    return (iexp, iblk, iflag, total.astype(jnp.int32).reshape(1), tok.reshape(n_blocks, 1, blk),
            row_dst.reshape(n_blocks, 1, blk), e_sorted.reshape(n_blocks, blk, 1))


def _combine_kernel(xs_ref, mod_ref, w_ref, *refs, d):
    y_refs, o_ref = refs[:TOP_K], refs[TOP_K]
    w = w_ref[0]
    tn = w.shape[0]
    parts = d // 128
    wk = [jnp.broadcast_to(w[:, k:k + 1], (tn, 128)) for k in range(TOP_K)]
    for p in range(parts):
        tot = y_refs[0][pl.ds(p, tn, stride=parts), :] * wk[0]
        for k in range(1, TOP_K):
            tot = tot + y_refs[k][pl.ds(p, tn, stride=parts), :] * wk[k]
        sl = slice(p * 128, (p + 1) * 128)
        o_ref[0, :, sl] = xs_ref[0, :, sl] + mod_ref[0][:, 5 * d + p * 128:5 * d + (p + 1) * 128] * tot


def _combine(xs, mod, w, y, n_ctx_tiles):
    b, n, d = xs.shape
    tn = TOK_TILE
    nb = b
    tiles = n // tn
    per_slot = b * tiles

    def mod_idx(bi, t):
        return (jnp.where(t < n_ctx_tiles, nb, bi), 0, 0)

    tok = lambda bi, t: (bi, t, 0)
    y_specs = [pl.BlockSpec((tn * (d // 128), 128),
                            functools.partial(lambda bi, t, k: (k * per_slot + bi * tiles + t, 0), k=k))
               for k in range(TOP_K)]
    return pl.pallas_call(
        functools.partial(_combine_kernel, d=d),
        grid=(b, tiles),
        in_specs=[pl.BlockSpec((1, tn, d), tok), pl.BlockSpec((1, 1, N_MOD * d), mod_idx),
                  pl.BlockSpec((1, tn, TOP_K), tok)] + y_specs,
        out_specs=pl.BlockSpec((1, tn, d), tok),
        out_shape=jax.ShapeDtypeStruct((b, n, d), F32),
        compiler_params=_cparams(("parallel", "parallel")),
        name="combine",
    )(xs, mod, w, *([y] * TOP_K))


def _final_kernel(x_ref, g_ref, o_ref):
    o_ref[0] = _rms_rows(x_ref[0], g_ref[...])


def _final_norm(x, g, n_ctx_tiles):
    b, n, d = x.shape
    tn = TOK_TILE
    n_lat_tiles = n // tn - n_ctx_tiles
    return pl.pallas_call(
        _final_kernel,
        grid=(b, n_lat_tiles),
        in_specs=[pl.BlockSpec((1, tn, d), lambda bi, t: (bi, t + n_ctx_tiles, 0)),
                  pl.BlockSpec((1, d), lambda bi, t: (0, 0))],
        out_specs=pl.BlockSpec((1, tn, d), lambda bi, t: (bi, t, 0)),
        out_shape=jax.ShapeDtypeStruct((b, n_lat_tiles * tn, d), F32),
        compiler_params=_cparams(("parallel", "parallel")),
        name="final_norm",
    )(x, g)


def _regroup_qkv_columns(w):
    a_q, a_kv = A_HEADS * HEAD_DIM, A_KV * HEAD_DIM
    b_q, b_kv = B_HEADS * HEAD_DIM, B_KV * HEAD_DIM
    c_qk, c_v = 2 * C_HEADS * HEAD_DIM, C_HEADS * 2 * HEAD_DIM
    off = np.concatenate([[0], np.cumsum((a_q, a_kv, a_kv, b_q, b_kv, b_kv, c_qk, c_qk, c_v))])
    seg = lambda i: w[:, int(off[i]):int(off[i + 1])]
    return jnp.concatenate([seg(0), seg(1), seg(3), seg(4), seg(6), seg(7), seg(2), seg(5), seg(8)], axis=1)


def _rotary_partner(row):
    blocks = row.reshape(-1, 2, ROPE_FREQS)
    return blocks[:, ::-1, :].reshape(-1)


def _rope_tables(n_ctx, n_lat):
    t = jnp.arange(n_lat, dtype=jnp.int32)
    row_pos = (t // GRID_W).astype(F32)
    col_pos = (t % GRID_W).astype(F32)
    inv_freq = jnp.power(ROPE_THETA, -jnp.arange(ROPE_FREQS, dtype=F32) / ROPE_FREQS)
    ang_r = row_pos[:, None] * inv_freq
    ang_c = col_pos[:, None] * inv_freq
    cos64 = jnp.concatenate([jnp.cos(ang_r), jnp.cos(ang_r), jnp.cos(ang_c), jnp.cos(ang_c)], axis=1)
    sin64 = jnp.concatenate([-jnp.sin(ang_r), jnp.sin(ang_r), -jnp.sin(ang_c), jnp.sin(ang_c)], axis=1)
    cos64 = jnp.concatenate([jnp.ones((n_ctx, HEAD_DIM), F32), cos64], axis=0)
    sin64 = jnp.concatenate([jnp.zeros((n_ctx, HEAD_DIM), F32), sin64], axis=0)
    return jnp.tile(cos64, (1, 2)), jnp.tile(sin64, (1, 2))


def _heads_major(t, n_heads):
    b, n, w = t.shape
    return t.reshape(b, n, n_heads, w // n_heads).transpose(0, 2, 1, 3)


def _heads_transposed(t, n_heads):
    b, n, w = t.shape
    return t.reshape(b, n, n_heads, w // n_heads).transpose(0, 2, 3, 1)


def _tokens_major(t):
    b, h, n, dv = t.shape
    return t.transpose(0, 2, 1, 3).reshape(b, n, h * dv)


def kernel(x, c, ctx, c_ctx, w_mod, b_mod, g_norm1, w_qkv, g_qnorm_a, g_knorm_a, sink_b, lam_q1, lam_k1, lam_q2, lam_k2, g_subln_c, w_br_a, w_br_b, w_br_c, w_gate, b_gate, w_out, g_norm2, w_router, e_bias, w1, w3, w2, ws1, ws3, ws2, g_final):
    bsz, n_lat, d = x.shape
    n_ctx = ctx.shape[1]
    depth = w_mod.shape[0]
    n_tok = n_ctx + n_lat
    assert n_ctx % TOK_TILE == 0 and n_lat % KEY_CHUNK == 0 and n_ctx % 128 == 0
    n_ctx_tiles = n_ctx // TOK_TILE
    n_all = bsz * n_tok

    rows = -(-(bsz + 1) // 8) * 8
    cvec = jnp.concatenate([c, c_ctx[None, :], jnp.zeros((rows - bsz - 1, d), F32)], axis=0)
    mod_all = _modulation(cvec, w_mod, b_mod)

    cos_t, sin_t = _rope_tables(n_ctx, n_lat)
    head_of = np.arange(R_A) // HEAD_DIM
    ones_blk = jnp.asarray((head_of[:, None] == head_of[None, :]).astype(np.float32), BF16)
    scale = HEAD_DIM ** -0.5 * LOG2E
    unit = jnp.ones((HEAD_DIM,), F32)

    xs = jnp.concatenate([ctx, x], axis=1)
    for l in range(depth):
        lam_init = 0.8 - 0.6 * math.exp(-0.3 * l)
        lam = (jnp.exp(jnp.dot(lam_q1[l], lam_k1[l])) - jnp.exp(jnp.dot(lam_q2[l], lam_k2[l]))).astype(F32) + lam_init
        mod = mod_all[l].reshape(rows, 1, N_MOD * d)
        g1 = g_norm1[l].reshape(1, d)

        wqkv = _regroup_qkv_columns(w_qkv[l]).astype(BF16)
        grow = jnp.concatenate([jnp.tile(g_qnorm_a[l] * scale, A_HEADS), jnp.tile(g_knorm_a[l], A_KV),
                                jnp.tile(unit * scale, B_HEADS), jnp.tile(unit, B_KV),
                                jnp.tile(unit * scale, 2 * C_HEADS), jnp.tile(unit, 2 * C_HEADS)])
        gsrow = _rotary_partner(grow)
        r_all, v_all = _pre_attention(xs, mod, g1, wqkv, ones_blk, grow.reshape(1, R_W),
                                      gsrow.reshape(1, R_W), cos_t, sin_t, n_ctx_tiles)

        o = 0
        qa = _heads_major(r_all[..., o:o + A_HEADS * HEAD_DIM], A_HEADS); o += A_HEADS * HEAD_DIM
        kta = _heads_transposed(r_all[..., o:o + A_KV * HEAD_DIM], A_KV); o += A_KV * HEAD_DIM
        qb = _heads_major(r_all[..., o:o + B_HEADS * HEAD_DIM], B_HEADS); o += B_HEADS * HEAD_DIM
        ktb = _heads_transposed(r_all[..., o:o + B_KV * HEAD_DIM], B_KV); o += B_KV * HEAD_DIM
        qc = _heads_major(r_all[..., o:o + 2 * C_HEADS * HEAD_DIM], 2 * C_HEADS); o += 2 * C_HEADS * HEAD_DIM
        ktc = _heads_transposed(r_all[..., o:o + 2 * C_HEADS * HEAD_DIM], 2 * C_HEADS)
        o = 0
        va = _heads_major(v_all[..., o:o + A_KV * HEAD_DIM], A_KV); o += A_KV * HEAD_DIM
        vb = _heads_major(v_all[..., o:o + B_KV * HEAD_DIM], B_KV); o += B_KV * HEAD_DIM
        vc = _heads_major(v_all[..., o:], C_HEADS)

        scal = jnp.concatenate([sink_b[l].astype(F32) * LOG2E, lam.reshape(1), jnp.full((1,), 1.0 - lam_init, F32)])
        scal_c = jnp.concatenate([jnp.zeros((DIFF_HEADS_PER_STEP,), F32), lam.reshape(1),
                                  jnp.full((1,), 1.0 - lam_init, F32)])
        gsub = g_subln_c[l].reshape(1, 2 * HEAD_DIM)
        g64 = jnp.ones((1, HEAD_DIM), F32)
        ya = _attention(scal, g64, qa, kta, va, heads=A_HEADS // A_KV, hpv=A_HEADS // A_KV, shared_k=True,
                        mode="global", n_ctx=n_ctx)
        yb = _attention(scal, g64, qb, ktb, vb, heads=B_HEADS // B_KV, hpv=B_HEADS // B_KV, shared_k=True,
                        mode="window", n_ctx=n_ctx)
        yc = _attention(scal_c, gsub, qc, ktc, vc, heads=DIFF_HEADS_PER_STEP, hpv=2, shared_k=False,
                        mode="diff", n_ctx=n_ctx)

        x1 = _merge(xs, mod, g1, _tokens_major(ya), _tokens_major(yb), _tokens_major(yc),
                    w_gate[l].astype(BF16), b_gate[l].reshape(1, -1), w_br_a[l].astype(BF16),
                    w_br_b[l].astype(BF16), w_br_c[l].astype(BF16), w_out[l].astype(BF16), n_ctx_tiles)

        f, idx_t, w_t, x_sh, tile_counts = _route(x1, mod, g_norm2[l].reshape(1, d), w_router[l].T,
                                     e_bias[l].reshape(N_EXPERTS, 1), ws1[l].astype(BF16),
                                     ws3[l].astype(BF16), ws2[l].astype(BF16), n_ctx_tiles)

        idx = idx_t.transpose(0, 2, 1).reshape(n_all, TOP_K)
        counts = jnp.sum(tile_counts, axis=(0, 1)).reshape(N_EXPERTS)
        iexp, iblk, iflag, nitems, row_tok, row_dst, erow = _routing_tables(idx, counts, n_all, d // 128)
        y = _experts(iexp, iblk, iflag, nitems, row_tok, row_dst, erow, f.reshape(n_all * (d // 128), 128),
                     w1[l], w3[l], w2[l])
        xs = _combine(x_sh, mod, w_t.transpose(0, 2, 1), y, n_ctx_tiles)

    return _final_norm(xs, g_final.reshape(1, d), n_ctx_tiles)
```

```python
import functools
import math

import numpy as np
import jax
import jax.numpy as jnp
from jax import lax
from jax.experimental import pallas as pl
from jax.experimental.pallas import tpu as pltpu

F32 = jnp.float32
BF16 = jnp.bfloat16

HEAD_DIM = 64
ROPE_FREQS = HEAD_DIM // 4
ROPE_THETA = 10000.0
GRID_W = 64
WINDOW = 128
A_HEADS, A_KV = 8, 2
B_HEADS, B_KV = 8, 2
C_HEADS = 4
N_EXPERTS = 128
TOP_K = 8
N_GROUPS = 8
TOPK_GROUPS = 4
GROUP_SIZE = N_EXPERTS // N_GROUPS
ROUTE_SCALE = 2.5
EXPERT_BLOCK = 128
EXPERT_RING = 3
N_MOD = 6
EPS = 1e-6
NEG = -1e30

QA_W, QB_W, QC_W = A_HEADS * HEAD_DIM, B_HEADS * HEAD_DIM, 2 * C_HEADS * HEAD_DIM
KA_W, KB_W, KC_W = A_KV * HEAD_DIM, B_KV * HEAD_DIM, 2 * C_HEADS * HEAD_DIM
Q_W = QA_W + QB_W + QC_W
K_W = KA_W + KB_W + KC_W
R_W = Q_W + K_W
R_A = QA_W + KA_W
V_W = A_KV * HEAD_DIM + B_KV * HEAD_DIM + C_HEADS * 2 * HEAD_DIM
K_HEADS = K_W // HEAD_DIM

TOK_TILE = 256
ATTN_TQ = 256
DIFF_HEADS_PER_STEP = 4
KEY_CHUNK = 512
MAX_KEY_CHUNK = {"global": 8448, "diff": 8448, "window": 128}
LOG2E = math.log2(math.e)
VMEM_LIMIT = 56 * 1024 * 1024


def _cparams(sem, **kw):
    return pltpu.CompilerParams(dimension_semantics=sem, vmem_limit_bytes=VMEM_LIMIT, **kw)


def _dot(a, b):
    return jnp.dot(a, b, preferred_element_type=F32)


def _sigmoid(x):
    return 1.0 / (1.0 + jnp.exp(-x))


def _silu(x):
    return x * _sigmoid(x)


def _rms_rows(x, g):
    return x * lax.rsqrt(jnp.mean(x * x, axis=-1, keepdims=True) + EPS) * g


def _mod_kernel(c_ref, w_ref, b_ref, o_ref):
    cs = _silu(c_ref[...])
    o_ref[0] = jnp.dot(cs, w_ref[0], preferred_element_type=F32,
                       precision=lax.Precision.HIGHEST) + b_ref[0]


def _modulation(cvec, w_mod, b_mod):
    depth, d, n = w_mod.shape
    rows = cvec.shape[0]
    bn = 1536
    return pl.pallas_call(
        _mod_kernel,
        grid=(depth, n // bn),
        in_specs=[
            pl.BlockSpec((rows, d), lambda l, j: (0, 0)),
            pl.BlockSpec((1, d, bn), lambda l, j: (l, 0, j)),
            pl.BlockSpec((1, 1, bn), lambda l, j: (l, 0, j)),
        ],
        out_specs=pl.BlockSpec((1, rows, bn), lambda l, j: (l, 0, j)),
        out_shape=jax.ShapeDtypeStruct((depth, rows, n), F32),
        compiler_params=_cparams(("parallel", "parallel")),
        name="modulation",
    )(cvec, w_mod, b_mod.reshape(depth, 1, n))


def _pre_kernel(x_ref, mod_ref, g1_ref, wqkv_ref, ones_ref, grow_ref, gsrow_ref,
                cos_ref, sin_ref, q_out, kt_out, vab_out, vc_out, *, d):
    x = x_ref[0]
    mod = mod_ref[0]
    tn = x.shape[0]
    a = (_rms_rows(x, g1_ref[...]) * (1.0 + mod[:, d:2 * d]) + mod[:, 0:d]).astype(BF16)
    p = _dot(a, wqkv_ref[...])
    pa = jnp.concatenate([p[:, 0:QA_W], p[:, Q_W:Q_W + KA_W]], axis=1)
    sq = pa * pa
    hi = sq.astype(BF16)
    lo = (sq - hi.astype(F32)).astype(BF16)
    ssq = _dot(hi, ones_ref[...]) + _dot(lo, ones_ref[...])
    rinv = lax.rsqrt(ssq * (1.0 / HEAD_DIM) + EPS)
    cos = cos_ref[...]
    sin = sin_ref[...]
    lane = lax.broadcasted_iota(jnp.int32, (tn, 128), 1)
    low_half = (lane % (2 * ROPE_FREQS)) < ROPE_FREQS
    first_head = lane < HEAD_DIM
    for j in range(R_W // 128):
        sl = slice(j * 128, (j + 1) * 128)
        pj = p[:, sl]
        ps = jnp.where(low_half, pltpu.roll(pj, 128 - ROPE_FREQS, 1), pltpu.roll(pj, ROPE_FREQS, 1))
        o = pj * (grow_ref[:, sl] * cos) + ps * (gsrow_ref[:, sl] * sin)
        if (j + 1) * 128 <= QA_W:
            o = o * rinv[:, sl]
        elif j * 128 == Q_W:
            o = o * rinv[:, QA_W:QA_W + KA_W]
        if j * 128 < Q_W:
            q_out[0, :, sl] = o.astype(BF16)
        else:
            ot = jnp.transpose(o).astype(BF16)
            kh = (j * 128 - Q_W) // HEAD_DIM
            kt_out[0, kh] = ot[0:HEAD_DIM]
            kt_out[0, kh + 1] = ot[HEAD_DIM:2 * HEAD_DIM]
    ones_at_64 = (lane == HEAD_DIM).astype(F32)
    for j in range(2):
        vj = p[:, R_W + j * 128:R_W + (j + 1) * 128]
        vab_out[0, 2 * j] = jnp.where(first_head, vj, ones_at_64).astype(BF16)
        vab_out[0, 2 * j + 1] = jnp.where(first_head, pltpu.roll(vj, HEAD_DIM, 1), ones_at_64).astype(BF16)
    ones_at_0 = (lane == 0).astype(BF16)
    for j in range(C_HEADS):
        vj = p[:, R_W + (2 + j) * 128:R_W + (3 + j) * 128]
        vc_out[0, j] = jnp.concatenate([vj.astype(BF16), ones_at_0], axis=1)


def _pre_attention(x, mod, g1, wqkv, ones_blk, grow, gsrow, cos_t, sin_t, n_ctx_tiles):
    b, n, d = x.shape
    tn = TOK_TILE
    nb = b

    def mod_idx(bi, t):
        return (jnp.where(t < n_ctx_tiles, nb, bi), 0, 0)

    return pl.pallas_call(
        functools.partial(_pre_kernel, d=d),
        grid=(b, n // tn),
        in_specs=[
            pl.BlockSpec((1, tn, d), lambda bi, t: (bi, t, 0)),
            pl.BlockSpec((1, 1, N_MOD * d), mod_idx),
            pl.BlockSpec((1, d), lambda bi, t: (0, 0)),
            pl.BlockSpec((d, R_W + V_W), lambda bi, t: (0, 0)),
            pl.BlockSpec((R_A, R_A), lambda bi, t: (0, 0)),
            pl.BlockSpec((1, R_W), lambda bi, t: (0, 0)),
            pl.BlockSpec((1, R_W), lambda bi, t: (0, 0)),
            pl.BlockSpec((tn, 128), lambda bi, t: (t, 0)),
            pl.BlockSpec((tn, 128), lambda bi, t: (t, 0)),
        ],
        out_specs=[
            pl.BlockSpec((1, tn, Q_W), lambda bi, t: (bi, t, 0)),
            pl.BlockSpec((1, K_HEADS, HEAD_DIM, tn), lambda bi, t: (bi, 0, 0, t)),
            pl.BlockSpec((1, A_KV + B_KV, tn, 128), lambda bi, t: (bi, 0, t, 0)),
            pl.BlockSpec((1, C_HEADS, tn, 256), lambda bi, t: (bi, 0, t, 0)),
        ],
        out_shape=[
            jax.ShapeDtypeStruct((b, n, Q_W), BF16),
            jax.ShapeDtypeStruct((b, K_HEADS, HEAD_DIM, n), BF16),
            jax.ShapeDtypeStruct((b, A_KV + B_KV, n, 128), BF16),
            jax.ShapeDtypeStruct((b, C_HEADS, n, 256), BF16),
        ],
        compiler_params=_cparams(("parallel", "parallel")),
        name="pre_attention",
    )(x, mod, g1, wqkv, ones_blk, grow, gsrow, cos_t, sin_t)


def _attn_kernel(sc_ref, gsub_ref, q_ref, kt_ref, v_ref, o_ref, m_sc, acc_sc, *, heads, hpv,
                 shared_k, mode, n_ctx, n_tok, tq, ck, dv):
    u = pl.program_id(1)
    qi = pl.program_id(2)
    is_lat = qi >= n_ctx // tq
    dvx = acc_sc.shape[-1]

    for g in range(heads):
        if mode == "window":
            m_sc[g] = jnp.full((tq, 1), sc_ref[u * heads + g], F32)
            lane = lax.broadcasted_iota(jnp.int32, (tq, dvx), 1)
            acc_sc[g] = jnp.where(lane == dv, 1.0, 0.0).astype(F32)
        else:
            m_sc[g] = jnp.full((tq, 1), NEG, F32)
            acc_sc[g] = jnp.zeros((tq, dvx), F32)

    def probs(g, s, mask):
        if mask is not None:
            s = jnp.where(mask, s, NEG)
        m = m_sc[g]
        m_new = jnp.maximum(m, jnp.max(s, axis=-1, keepdims=True))
        m_sc[g] = m_new
        return jnp.exp2(s - m_new).astype(BF16), jnp.exp2(m - m_new)

    def q_of(g):
        return q_ref[0, :, g * HEAD_DIM:(g + 1) * HEAD_DIM]

    def update_all(kt_of, v_of, mask=None):
        s_next = _dot(q_of(0), kt_of(0))
        for g in range(heads):
            s = s_next
            if g + 1 < heads:
                s_next = _dot(q_of(g + 1), kt_of(g + 1))
            p, alpha = probs(g, s, mask)
            acc_sc[g] = alpha * acc_sc[g] + _dot(p, v_of(g))

    def step(k0, size):
        update_all(lambda g: kt_ref[0, 0 if shared_k else g, :, pl.ds(k0, size)],
                   lambda g: v_ref[0, g // hpv, pl.ds(k0, size), :])

    if mode == "window":
        span = tq + 2 * WINDOW
        start = pl.multiple_of(jnp.clip(qi * tq - WINDOW, 0, n_tok - span), 128)
        qpos = qi * tq + lax.broadcasted_iota(jnp.int32, (tq, n_ctx + span), 0)
        col = lax.broadcasted_iota(jnp.int32, (tq, n_ctx + span), 1)
        kpos = start + col - n_ctx
        mask = (col < n_ctx) | (is_lat & (kpos >= n_ctx) & (jnp.abs(kpos - qpos) <= WINDOW))
        v = jnp.concatenate([v_ref[0, 0, 0:n_ctx, :], v_ref[0, 0, pl.ds(start, span), :]], axis=0)

        def kt_of(g):
            gk = 0 if shared_k else g
            return jnp.concatenate([kt_ref[0, gk, :, 0:n_ctx], kt_ref[0, gk, :, pl.ds(start, span)]], axis=1)

        update_all(kt_of, lambda g: v, mask)
    else:
        @pl.when(jnp.logical_not(is_lat))
        def _():
            step(0, n_ctx)

        def body(ci, carry):
            step(pl.multiple_of(ci * ck, 128), ck)
            return carry

        lax.fori_loop(0, jnp.where(is_lat, n_tok // ck, 0), body, 0)

    def result(g):
        acc = acc_sc[g]
        return acc[:, 0:128] / acc[:, dv:dv + 1]

    if mode == "diff":
        lam = sc_ref[heads]
        post = sc_ref[heads + 1]
        for j in range(heads // 2):
            y = result(2 * j) - lam * result(2 * j + 1)
            o_ref[0, :, j * dv:(j + 1) * dv] = (_rms_rows(y, gsub_ref[...]) * post).astype(o_ref.dtype)
    else:
        lane = lax.broadcasted_iota(jnp.int32, (tq, 128), 1)
        for j in range(heads // 2):
            pair = jnp.where(lane < dv, result(2 * j), pltpu.roll(result(2 * j + 1), dv, 1))
            o_ref[0, :, j * 128:(j + 1) * 128] = pair.astype(o_ref.dtype)


def _attention(scalars, gsub, q, kt, vx, *, heads, hpv, shared_k, mode, n_ctx, units, q_unit0, k_unit0,
               v_unit0):
    b, n, _ = q.shape
    hd = HEAD_DIM
    dvx = vx.shape[-1]
    dv = gsub.shape[-1]
    assert dv < dvx
    tq = ATTN_TQ
    gk = 1 if shared_k else heads
    vh = heads // hpv
    out_w = heads * hd if mode != "diff" else (heads // 2) * dv
    ck = max(c for c in range(128, MAX_KEY_CHUNK[mode] + 1, 128) if n % c == 0)
    kern = functools.partial(_attn_kernel, heads=heads, hpv=hpv, shared_k=shared_k, mode=mode,
                             n_ctx=n_ctx, n_tok=n, tq=tq, ck=ck, dv=dv)
    return pl.pallas_call(
        kern,
        grid=(b, units, n // tq),
        in_specs=[
            pl.BlockSpec(memory_space=pltpu.SMEM),
            pl.BlockSpec((1, dv), lambda bi, u, t: (0, 0)),
            pl.BlockSpec((1, tq, heads * hd), lambda bi, u, t: (bi, t, q_unit0 + u)),
            pl.BlockSpec((1, gk, hd, n), lambda bi, u, t: (bi, k_unit0 + u, 0, 0)),
            pl.BlockSpec((1, vh, n, dvx), lambda bi, u, t: (bi, v_unit0 + u, 0, 0)),
        ],
        out_specs=pl.BlockSpec((1, tq, out_w), lambda bi, u, t: (bi, t, u)),
        out_shape=jax.ShapeDtypeStruct((b, n, units * out_w), BF16),
        scratch_shapes=[
            pltpu.VMEM((heads, tq, 1), F32),
            pltpu.VMEM((heads, tq, dvx), F32),
        ],
        compiler_params=_cparams(("parallel", "parallel", "arbitrary")),
        name="attn_" + mode,
    )(scalars, gsub, q, kt, vx)


def _merge_kernel(x_ref, mod_ref, g1_ref, ya_ref, yb_ref, yc_ref, wg_ref, bg_ref, wa_ref, wb_ref,
                  wc_ref, wo_ref, o_ref, *, d):
    x = x_ref[0]
    mod = mod_ref[0]
    a = (_rms_rows(x, g1_ref[...]) * (1.0 + mod[:, d:2 * d]) + mod[:, 0:d]).astype(BF16)
    gate = _sigmoid(_dot(a, wg_ref[...]) + bg_ref[...])
    m = (gate[:, 0:d] * _dot(ya_ref[0], wa_ref[...])
         + gate[:, d:2 * d] * _dot(yb_ref[0], wb_ref[...])
         + gate[:, 2 * d:3 * d] * _dot(yc_ref[0], wc_ref[...]))
    mix = _dot(m.astype(BF16), wo_ref[...])
    o_ref[0] = x + mod[:, 2 * d:3 * d] * mix


def _merge(x, mod, g1, ya, yb, yc, wg, bg, wa, wb, wc, wo, n_ctx_tiles):
    b, n, d = x.shape
    tn = TOK_TILE
    nb = b
    yw = ya.shape[-1]

    def mod_idx(bi, t):
        return (jnp.where(t < n_ctx_tiles, nb, bi), 0, 0)

    tok = lambda bi, t: (bi, t, 0)
    const = lambda bi, t: (0, 0)
    return pl.pallas_call(
        functools.partial(_merge_kernel, d=d),
        grid=(b, n // tn),
        in_specs=[
            pl.BlockSpec((1, tn, d), tok),
            pl.BlockSpec((1, 1, N_MOD * d), mod_idx),
            pl.BlockSpec((1, d), const),
            pl.BlockSpec((1, tn, yw), tok),
            pl.BlockSpec((1, tn, yw), tok),
            pl.BlockSpec((1, tn, yw), tok),
            pl.BlockSpec((d, 3 * d), const),
            pl.BlockSpec((1, 3 * d), const),
            pl.BlockSpec((yw, d), const),
            pl.BlockSpec((yw, d), const),
            pl.BlockSpec((yw, d), const),
            pl.BlockSpec((d, d), const),
        ],
        out_specs=pl.BlockSpec((1, tn, d), tok),
        out_shape=jax.ShapeDtypeStruct((b, n, d), F32),
        compiler_params=_cparams(("parallel", "parallel")),
        name="merge",
    )(x, mod, g1, ya, yb, yc, wg, bg, wa, wb, wc, wo)


def _first_index(hit, idx, big):
    return jnp.min(jnp.where(hit, idx, big), axis=0, keepdims=True)


def _route_kernel(x_ref, mod_ref, g2_ref, wrt_ref, eb_ref, ws1_ref, ws3_ref, ws2_ref,
                  f_ref, idx_ref, w_ref, xs_ref, cnt_ref, *, d):
    x = x_ref[0]
    mod = mod_ref[0]
    f = _rms_rows(x, g2_ref[...]) * (1.0 + mod[:, 4 * d:5 * d]) + mod[:, 3 * d:4 * d]
    tn = f.shape[0]
    for s in range(d // 128):
        f_ref[0, pl.ds(s, tn, stride=d // 128), :] = f[:, s * 128:(s + 1) * 128]
    logits = lax.dot_general(wrt_ref[...], f, (((1,), (1,)), ((), ())),
                             preferred_element_type=F32, precision=lax.Precision.HIGHEST)
    scores = _sigmoid(logits)
    choice = scores + eb_ref[...]
    eidx = lax.broadcasted_iota(jnp.int32, (N_EXPERTS, tn), 0)
    lidx = lax.broadcasted_iota(jnp.int32, (GROUP_SIZE, tn), 0)
    gscore = []
    for g in range(N_GROUPS):
        cg = choice[g * GROUP_SIZE:(g + 1) * GROUP_SIZE, :]
        m1 = jnp.max(cg, axis=0, keepdims=True)
        first = _first_index(cg == m1, lidx, GROUP_SIZE)
        m2 = jnp.max(jnp.where(lidx == first, NEG, cg), axis=0, keepdims=True)
        gscore.append(m1 + m2)
    gs = jnp.concatenate(gscore, axis=0)
    gidx = lax.broadcasted_iota(jnp.int32, (N_GROUPS, tn), 0)
    gsel = jnp.zeros((N_GROUPS, tn), jnp.bool_)
    for _ in range(TOPK_GROUPS):
        gm = jnp.max(gs, axis=0, keepdims=True)
        first = _first_index(gs == gm, gidx, N_GROUPS)
        hit = gidx == first
        gsel = gsel | hit
        gs = jnp.where(hit, NEG, gs)
    gself = gsel.astype(F32)
    emask = jnp.concatenate(
        [jnp.broadcast_to(gself[g:g + 1, :], (GROUP_SIZE, tn)) for g in range(N_GROUPS)], axis=0)
    cur = jnp.where(emask > 0.5, choice, NEG)
    ids, ws = [], []
    chosen = jnp.zeros((N_EXPERTS, tn), F32)
    for _ in range(TOP_K):
        m = jnp.max(cur, axis=0, keepdims=True)
        first = _first_index(cur == m, eidx, N_EXPERTS)
        hit = eidx == first
        ids.append(first)
        ws.append(jnp.sum(jnp.where(hit, scores, 0.0), axis=0, keepdims=True))
        cur = jnp.where(hit, NEG, cur)
        chosen = chosen + hit.astype(F32)
    wsel = jnp.concatenate(ws, axis=0)
    idx_ref[0] = jnp.concatenate(ids, axis=0)
    cnt_ref[0, 0] = jnp.sum(chosen, axis=1, keepdims=True).astype(jnp.int32)
    w_ref[0] = wsel / jnp.sum(wsel, axis=0, keepdims=True) * ROUTE_SCALE
    fb = f.astype(BF16)
    h = _silu(_dot(fb, ws1_ref[...])) * _dot(fb, ws3_ref[...])
    xs_ref[0] = x + mod[:, 5 * d:6 * d] * _dot(h.astype(BF16), ws2_ref[...])


def _route(x, mod, g2, wrt, eb, ws1, ws3, ws2, n_ctx_tiles):
    b, n, d = x.shape
    tn = TOK_TILE
    nb = b
    ds = ws1.shape[-1]

    def mod_idx(bi, t):
        return (jnp.where(t < n_ctx_tiles, nb, bi), 0, 0)

    tok = lambda bi, t: (bi, t, 0)
    lane_tok = lambda bi, t: (bi, 0, t)
    const = lambda bi, t: (0, 0)
    return pl.pallas_call(
        functools.partial(_route_kernel, d=d),
        grid=(b, n // tn),
        in_specs=[
            pl.BlockSpec((1, tn, d), tok),
            pl.BlockSpec((1, 1, N_MOD * d), mod_idx),
            pl.BlockSpec((1, d), const),
            pl.BlockSpec((N_EXPERTS, d), const),
            pl.BlockSpec((N_EXPERTS, 1), const),
            pl.BlockSpec((d, ds), const),
            pl.BlockSpec((d, ds), const),
            pl.BlockSpec((ds, d), const),
        ],
        out_specs=[
            pl.BlockSpec((1, tn * (d // 128), 128), tok),
            pl.BlockSpec((1, TOP_K, tn), lane_tok),
            pl.BlockSpec((1, TOP_K, tn), lane_tok),
            pl.BlockSpec((1, tn, d), tok),
            pl.BlockSpec((1, 1, N_EXPERTS, 1), lambda bi, t: (bi, t, 0, 0)),
        ],
        out_shape=[
            jax.ShapeDtypeStruct((b, n * (d // 128), 128), F32),
            jax.ShapeDtypeStruct((b, TOP_K, n), jnp.int32),
            jax.ShapeDtypeStruct((b, TOP_K, n), F32),
            jax.ShapeDtypeStruct((b, n, d), F32),
            jax.ShapeDtypeStruct((b, n // tn, N_EXPERTS, 1), jnp.int32),
        ],
        compiler_params=_cparams(("parallel", "parallel")),
        name="route_shared",
    )(x, mod, g2, wrt, eb, ws1, ws3, ws2)


def _expert_kernel(iexp_ref, iblk_ref, iflag_ref, nitems_ref, tok_ref, tok_next_ref, tok_ahead_ref,
                   dst_prev_ref, dst_ref, erow_ref, f_hbm, w1_ref, w3_ref, w2_ref, y_hbm, xbuf, ybuf,
                   wb1, wb3, wb2, gsem, ssem):
    i = pl.program_id(0)
    nitems = nitems_ref[0]
    blk = iblk_ref[i]
    expert = iexp_ref[i]
    first = (iflag_ref[i] & 1) != 0
    last = (iflag_ref[i] & 2) != 0
    ring = xbuf.shape[0]
    slot = blk % ring
    slot_prev = (blk + ring - 1) % ring
    slot_prev2 = (blk + ring - 2) % ring
    slot_ahead = (blk + 2) % ring
    rows = EXPERT_BLOCK
    parts = xbuf.shape[1] // rows
    n_blocks = y_hbm.shape[0] // (rows * parts)

    def gather(idx_ref, s):
        for j in range(rows):
            src = pl.multiple_of(idx_ref[0, 0, j], parts)
            pltpu.make_async_copy(f_hbm.at[pl.ds(src, parts)], xbuf.at[s, pl.ds(j * parts, parts)],
                                  gsem.at[s]).start(priority=j % 2)

    def gather_wait(s):
        pltpu.make_async_copy(f_hbm.at[pl.ds(0, rows * parts)], xbuf.at[s], gsem.at[s]).wait()

    def scatter(idx_ref, s):
        for j in range(rows):
            dst = pl.multiple_of(idx_ref[0, 0, j], parts)
            pltpu.make_async_copy(ybuf.at[s, pl.ds(j * parts, parts)], y_hbm.at[pl.ds(dst, parts)],
                                  ssem.at[s]).start(priority=j % 2)

    def scatter_wait(s):
        pltpu.make_async_copy(ybuf.at[s], y_hbm.at[pl.ds(0, rows * parts)], ssem.at[s]).wait()

    def expert_rows(s, after_up=None, after_down=None):
        xb = jnp.concatenate([xbuf[s, pl.ds(p, rows, stride=parts), :] for p in range(parts)],
                             axis=1).astype(BF16)
        h1 = _dot(xb, wb1[...])
        h3 = _dot(xb, wb3[...])
        if after_up is not None:
            after_up()
        y = _dot((_silu(h1) * h3).astype(BF16), wb2[...])
        if after_down is not None:
            after_down()
        return y

    def store_rows(s, y, accumulate):
        for p in range(parts):
            part = y[:, p * 128:(p + 1) * 128]
            if accumulate:
                part = part + ybuf[s, pl.ds(p, rows, stride=parts), :]
            ybuf[s, pl.ds(p, rows, stride=parts), :] = part

    valid = i < nitems

    @pl.when(valid & ((iflag_ref[i] & 4) != 0))
    def _():
        wb1[...] = w1_ref[0].astype(BF16)
        wb3[...] = w3_ref[0].astype(BF16)
        wb2[...] = w2_ref[0].astype(BF16)

    fast = valid & first & last & (blk >= 3) & (blk <= n_blocks - 3)

    @pl.when(fast)
    def _():
        gather_wait(slot)
        scatter_wait(slot)
        y = expert_rows(slot, after_up=lambda: gather(tok_ahead_ref, slot_ahead),
                        after_down=lambda: scatter(dst_prev_ref, slot_prev))
        store_rows(slot, y, False)

    @pl.when(valid & jnp.logical_not(fast))
    def _():
        @pl.when(first)
        def _():
            @pl.when(blk == 0)
            def _():
                gather(tok_ref, 0)
                if n_blocks > 1:
                    gather(tok_next_ref, 1)

            @pl.when(blk + 2 < n_blocks)
            def _():
                gather(tok_ahead_ref, slot_ahead)

            @pl.when(blk >= 1)
            def _():
                scatter(dst_prev_ref, slot_prev)

            gather_wait(slot)

            @pl.when(blk >= 3)
            def _():
                scatter_wait(slot)

        y = expert_rows(slot) * (erow_ref[0] == expert).astype(F32)

        @pl.when(first)
        def _():
            store_rows(slot, y, False)

        @pl.when(jnp.logical_not(first))
        def _():
            store_rows(slot, y, True)

        @pl.when(last & (blk == n_blocks - 1))
        def _():
            scatter(dst_ref, slot)

            @pl.when(blk >= 2)
            def _():
                scatter_wait(slot_prev2)

            @pl.when(blk >= 1)
            def _():
                scatter_wait(slot_prev)

            scatter_wait(slot)


def _experts(iexp, iblk, iflag, nitems, row_tok, row_dst, erow, f_rows, w1, w3, w2):
    n_items = iexp.shape[0]
    n_blocks = row_tok.shape[0]
    d, de = w1.shape[1], w1.shape[2]
    parts = d // 128
    rows = EXPERT_BLOCK
    last = n_blocks - 1
    cur = lambda i, ie, ib, fl, nt: (ib[i], 0, 0)
    nxt = lambda i, ie, ib, fl, nt: (jnp.minimum(ib[i] + 1, last), 0, 0)
    ahd = lambda i, ie, ib, fl, nt: (jnp.minimum(ib[i] + 2, last), 0, 0)
    prv = lambda i, ie, ib, fl, nt: (jnp.maximum(ib[i] - 1, 0), 0, 0)
    wsel = lambda i, ie, ib, fl, nt: (ie[i], 0, 0)
    grid_spec = pltpu.PrefetchScalarGridSpec(
        num_scalar_prefetch=4,
        grid=(n_items,),
        in_specs=[
            pl.BlockSpec((1, 1, rows), cur, memory_space=pltpu.SMEM),
            pl.BlockSpec((1, 1, rows), nxt, memory_space=pltpu.SMEM),
            pl.BlockSpec((1, 1, rows), ahd, memory_space=pltpu.SMEM),
            pl.BlockSpec((1, 1, rows), prv, memory_space=pltpu.SMEM),
            pl.BlockSpec((1, 1, rows), cur, memory_space=pltpu.SMEM),
            pl.BlockSpec((1, rows, 1), cur),
            pl.BlockSpec(memory_space=pl.ANY),
            pl.BlockSpec((1, d, de), wsel),
            pl.BlockSpec((1, d, de), wsel),
            pl.BlockSpec((1, de, d), wsel),
        ],
        out_specs=pl.BlockSpec(memory_space=pl.ANY),
        scratch_shapes=[
            pltpu.VMEM((EXPERT_RING, rows * parts, 128), F32),
            pltpu.VMEM((EXPERT_RING, rows * parts, 128), F32),
            pltpu.VMEM((d, de), BF16),
            pltpu.VMEM((d, de), BF16),
            pltpu.VMEM((de, d), BF16),
            pltpu.SemaphoreType.DMA((EXPERT_RING,)),
            pltpu.SemaphoreType.DMA((EXPERT_RING,)),
        ],
    )
    return pl.pallas_call(
        _expert_kernel,
        grid_spec=grid_spec,
        out_shape=jax.ShapeDtypeStruct((n_blocks * rows * parts, 128), F32),
        compiler_params=_cparams(("arbitrary",)),
        name="experts",
    )(iexp, iblk, iflag, nitems, row_tok, row_tok, row_tok, row_dst, row_dst, erow, f_rows, w1, w3, w2)


ASSIGN_BITS = 20


def _routing_tables(idx, counts, n_tokens, parts):
    n_assign = n_tokens * TOP_K
    blk = EXPERT_BLOCK
    assert n_assign % blk == 0 and n_assign <= (1 << ASSIGN_BITS)
    n_blocks = n_assign // blk
    n_items = n_blocks + N_EXPERTS
    flat_e = idx.reshape(-1).astype(jnp.int32)
    key = jnp.sort((flat_e << ASSIGN_BITS) | jnp.arange(n_assign, dtype=jnp.int32))
    e_sorted = key >> ASSIGN_BITS
    order = key & ((1 << ASSIGN_BITS) - 1)
    tok = order // TOP_K
    row_dst = ((order % TOP_K) * n_tokens + tok) * parts
    tok = tok * parts

    experts = jnp.arange(N_EXPERTS, dtype=jnp.int32)
    ends = jnp.cumsum(counts)
    starts = ends - counts
    first_blk = starts // blk
    n_be = jnp.where(ends > starts, (ends - 1) // blk - first_blk + 1, 0)
    item_end = jnp.cumsum(n_be)
    item_off = item_end - n_be
    total = item_end[-1]
    i = jnp.arange(n_items, dtype=jnp.int32)
    iexp = jnp.minimum(jnp.sum((item_end[None, :] <= i[:, None]).astype(jnp.int32), axis=1), N_EXPERTS - 1)
    pick = (iexp[:, None] == experts[None, :]).astype(jnp.int32)
    base = jnp.sum(pick * (first_blk - item_off)[None, :], axis=1)
    iblk = jnp.where(i < total, base + i, n_blocks - 1).astype(jnp.int32)
    prev_blk = jnp.concatenate([jnp.full((1,), -1, jnp.int32), iblk[:-1]])
    next_blk = jnp.concatenate([iblk[1:], jnp.full((1,), -1, jnp.int32)])
    is_first = iblk != prev_blk
    is_last = (iblk != next_blk) | (i == total - 1)
    new_expert = iexp != jnp.concatenate([jnp.full((1,), -1, jnp.int32), iexp[:-1]])
    iflag = is_first.astype(jnp.int32) + 2 * is_last.astype(jnp.int32) + 4 * new_expert.astype(jnp.int32)
    return (iexp, iblk, iflag, total.astype(jnp.int32).reshape(1), tok.reshape(n_blocks, 1, blk),
            row_dst.reshape(n_blocks, 1, blk), e_sorted.reshape(n_blocks, blk, 1))


def _combine_kernel(xs_ref, mod_ref, w_ref, *refs, d):
    y_refs, o_ref = refs[:TOP_K], refs[TOP_K]
    w = w_ref[0]
    tn = w.shape[0]
    parts = d // 128
    wk = [jnp.broadcast_to(w[:, k:k + 1], (tn, 128)) for k in range(TOP_K)]
    for p in range(parts):
        tot = y_refs[0][pl.ds(p, tn, stride=parts), :] * wk[0]
        for k in range(1, TOP_K):
            tot = tot + y_refs[k][pl.ds(p, tn, stride=parts), :] * wk[k]
        sl = slice(p * 128, (p + 1) * 128)
        o_ref[0, :, sl] = xs_ref[0, :, sl] + mod_ref[0][:, 5 * d + p * 128:5 * d + (p + 1) * 128] * tot


def _combine(xs, mod, w, y, n_ctx_tiles):
    b, n, d = xs.shape
    tn = TOK_TILE
    nb = b
    tiles = n // tn
    per_slot = b * tiles

    def mod_idx(bi, t):
        return (jnp.where(t < n_ctx_tiles, nb, bi), 0, 0)

    tok = lambda bi, t: (bi, t, 0)
    y_specs = [pl.BlockSpec((tn * (d // 128), 128),
                            functools.partial(lambda bi, t, k: (k * per_slot + bi * tiles + t, 0), k=k))
               for k in range(TOP_K)]
    return pl.pallas_call(
        functools.partial(_combine_kernel, d=d),
        grid=(b, tiles),
        in_specs=[pl.BlockSpec((1, tn, d), tok), pl.BlockSpec((1, 1, N_MOD * d), mod_idx),
                  pl.BlockSpec((1, tn, TOP_K), tok)] + y_specs,
        out_specs=pl.BlockSpec((1, tn, d), tok),
        out_shape=jax.ShapeDtypeStruct((b, n, d), F32),
        compiler_params=_cparams(("parallel", "parallel")),
        name="combine",
    )(xs, mod, w, *([y] * TOP_K))


def _final_kernel(x_ref, g_ref, o_ref):
    o_ref[0] = _rms_rows(x_ref[0], g_ref[...])


def _final_norm(x, g, n_ctx_tiles):
    b, n, d = x.shape
    tn = TOK_TILE
    n_lat_tiles = n // tn - n_ctx_tiles
    return pl.pallas_call(
        _final_kernel,
        grid=(b, n_lat_tiles),
        in_specs=[pl.BlockSpec((1, tn, d), lambda bi, t: (bi, t + n_ctx_tiles, 0)),
                  pl.BlockSpec((1, d), lambda bi, t: (0, 0))],
        out_specs=pl.BlockSpec((1, tn, d), lambda bi, t: (bi, t, 0)),
        out_shape=jax.ShapeDtypeStruct((b, n_lat_tiles * tn, d), F32),
        compiler_params=_cparams(("parallel", "parallel")),
        name="final_norm",
    )(x, g)


def _regroup_qkv_columns(w):
    a_q, a_kv = A_HEADS * HEAD_DIM, A_KV * HEAD_DIM
    b_q, b_kv = B_HEADS * HEAD_DIM, B_KV * HEAD_DIM
    c_qk, c_v = 2 * C_HEADS * HEAD_DIM, C_HEADS * 2 * HEAD_DIM
    off = np.concatenate([[0], np.cumsum((a_q, a_kv, a_kv, b_q, b_kv, b_kv, c_qk, c_qk, c_v))])
    seg = lambda i: w[:, int(off[i]):int(off[i + 1])]
    return jnp.concatenate([seg(0), seg(3), seg(6), seg(1), seg(4), seg(7), seg(2), seg(5), seg(8)], axis=1)


def _rotary_partner(row):
    blocks = row.reshape(-1, 2, ROPE_FREQS)
    return blocks[:, ::-1, :].reshape(-1)


def _rope_tables(n_ctx, n_lat):
    t = jnp.arange(n_lat, dtype=jnp.int32)
    row_pos = (t // GRID_W).astype(F32)
    col_pos = (t % GRID_W).astype(F32)
    inv_freq = jnp.power(ROPE_THETA, -jnp.arange(ROPE_FREQS, dtype=F32) / ROPE_FREQS)
    ang_r = row_pos[:, None] * inv_freq
    ang_c = col_pos[:, None] * inv_freq
    cos64 = jnp.concatenate([jnp.cos(ang_r), jnp.cos(ang_r), jnp.cos(ang_c), jnp.cos(ang_c)], axis=1)
    sin64 = jnp.concatenate([-jnp.sin(ang_r), jnp.sin(ang_r), -jnp.sin(ang_c), jnp.sin(ang_c)], axis=1)
    cos64 = jnp.concatenate([jnp.ones((n_ctx, HEAD_DIM), F32), cos64], axis=0)
    sin64 = jnp.concatenate([jnp.zeros((n_ctx, HEAD_DIM), F32), sin64], axis=0)
    return jnp.tile(cos64, (1, 2)), jnp.tile(sin64, (1, 2))


def kernel(x, c, ctx, c_ctx, w_mod, b_mod, g_norm1, w_qkv, g_qnorm_a, g_knorm_a, sink_b, lam_q1, lam_k1, lam_q2, lam_k2, g_subln_c, w_br_a, w_br_b, w_br_c, w_gate, b_gate, w_out, g_norm2, w_router, e_bias, w1, w3, w2, ws1, ws3, ws2, g_final):
    bsz, n_lat, d = x.shape
    n_ctx = ctx.shape[1]
    depth = w_mod.shape[0]
    n_tok = n_ctx + n_lat
    assert n_ctx % TOK_TILE == 0 and n_lat % KEY_CHUNK == 0 and n_ctx % 128 == 0
    n_ctx_tiles = n_ctx // TOK_TILE
    n_all = bsz * n_tok

    rows = -(-(bsz + 1) // 8) * 8
    cvec = jnp.concatenate([c, c_ctx[None, :], jnp.zeros((rows - bsz - 1, d), F32)], axis=0)
    mod_all = _modulation(cvec, w_mod, b_mod)

    cos_t, sin_t = _rope_tables(n_ctx, n_lat)
    head_of = np.arange(R_A) // HEAD_DIM
    ones_blk = jnp.asarray((head_of[:, None] == head_of[None, :]).astype(np.float32), BF16)
    scale = HEAD_DIM ** -0.5 * LOG2E
    unit = jnp.ones((HEAD_DIM,), F32)

    xs = jnp.concatenate([ctx, x], axis=1)
    for l in range(depth):
        lam_init = 0.8 - 0.6 * math.exp(-0.3 * l)
        lam = (jnp.exp(jnp.dot(lam_q1[l], lam_k1[l])) - jnp.exp(jnp.dot(lam_q2[l], lam_k2[l]))).astype(F32) + lam_init
        mod = mod_all[l].reshape(rows, 1, N_MOD * d)
        g1 = g_norm1[l].reshape(1, d)

        wqkv = _regroup_qkv_columns(w_qkv[l]).astype(BF16)
        grow = jnp.concatenate([jnp.tile(g_qnorm_a[l] * scale, A_HEADS), jnp.tile(unit * scale, B_HEADS),
                                jnp.tile(unit * scale, 2 * C_HEADS), jnp.tile(g_knorm_a[l], A_KV),
                                jnp.tile(unit, B_KV), jnp.tile(unit, 2 * C_HEADS)])
        gsrow = _rotary_partner(grow)
        q_all, kt_all, vx_ab, vx_c = _pre_attention(xs, mod, g1, wqkv, ones_blk, grow.reshape(1, R_W),
                                                    gsrow.reshape(1, R_W), cos_t, sin_t, n_ctx_tiles)

        scal = jnp.concatenate([sink_b[l].astype(F32) * LOG2E, lam.reshape(1), jnp.full((1,), 1.0 - lam_init, F32)])
        scal_c = jnp.concatenate([jnp.zeros((DIFF_HEADS_PER_STEP,), F32), lam.reshape(1),
                                  jnp.full((1,), 1.0 - lam_init, F32)])
        gsub = g_subln_c[l].reshape(1, 2 * HEAD_DIM)
        g64 = jnp.ones((1, HEAD_DIM), F32)
        ya = _attention(scal, g64, q_all, kt_all, vx_ab, heads=A_HEADS // A_KV, hpv=A_HEADS // A_KV,
                        shared_k=True, mode="global", n_ctx=n_ctx, units=A_KV, q_unit0=0, k_unit0=0, v_unit0=0)
        yb = _attention(scal, g64, q_all, kt_all, vx_ab, heads=B_HEADS // B_KV, hpv=B_HEADS // B_KV,
                        shared_k=True, mode="window", n_ctx=n_ctx, units=B_KV, q_unit0=QA_W // 256,
                        k_unit0=A_KV, v_unit0=A_KV)
        yc = _attention(scal_c, gsub, q_all, kt_all, vx_c, heads=DIFF_HEADS_PER_STEP, hpv=2, shared_k=False,
                        mode="diff", n_ctx=n_ctx, units=2 * C_HEADS // DIFF_HEADS_PER_STEP,
                        q_unit0=(QA_W + QB_W) // 256, k_unit0=(A_KV + B_KV) // DIFF_HEADS_PER_STEP, v_unit0=0)

        x1 = _merge(xs, mod, g1, ya, yb, yc,
                    w_gate[l].astype(BF16), b_gate[l].reshape(1, -1), w_br_a[l].astype(BF16),
                    w_br_b[l].astype(BF16), w_br_c[l].astype(BF16), w_out[l].astype(BF16), n_ctx_tiles)

        f, idx_t, w_t, x_sh, tile_counts = _route(x1, mod, g_norm2[l].reshape(1, d), w_router[l].T,
                                     e_bias[l].reshape(N_EXPERTS, 1), ws1[l].astype(BF16),
                                     ws3[l].astype(BF16), ws2[l].astype(BF16), n_ctx_tiles)

        idx = idx_t.transpose(0, 2, 1).reshape(n_all, TOP_K)
        counts = jnp.sum(tile_counts, axis=(0, 1)).reshape(N_EXPERTS)
        iexp, iblk, iflag, nitems, row_tok, row_dst, erow = _routing_tables(idx, counts, n_all, d // 128)
        y = _experts(iexp, iblk, iflag, nitems, row_tok, row_dst, erow, f.reshape(n_all * (d // 128), 128),
                     w1[l], w3[l], w2[l])
        xs = _combine(x_sh, mod, w_t.transpose(0, 2, 1), y, n_ctx_tiles)

    return _final_norm(xs, g_final.reshape(1, d), n_ctx_tiles)
```

```python
import functools
import math

import numpy as np
import jax
import jax.numpy as jnp
from jax import lax
from jax.experimental import pallas as pl
from jax.experimental.pallas import tpu as pltpu

F32 = jnp.float32
BF16 = jnp.bfloat16

HEAD_DIM = 64
ROPE_FREQS = HEAD_DIM // 4
ROPE_THETA = 10000.0
GRID_W = 64
WINDOW = 128
A_HEADS, A_KV = 8, 2
B_HEADS, B_KV = 8, 2
C_HEADS = 4
N_EXPERTS = 128
TOP_K = 8
N_GROUPS = 8
TOPK_GROUPS = 4
GROUP_SIZE = N_EXPERTS // N_GROUPS
ROUTE_SCALE = 2.5
EXPERT_BLOCK = 128
EXPERT_RING = 3
N_MOD = 6
EPS = 1e-6
NEG = -1e30

QA_W, QB_W, QC_W = A_HEADS * HEAD_DIM, B_HEADS * HEAD_DIM, 2 * C_HEADS * HEAD_DIM
KA_W, KB_W, KC_W = A_KV * HEAD_DIM, B_KV * HEAD_DIM, 2 * C_HEADS * HEAD_DIM
Q_W = QA_W + QB_W + QC_W
K_W = KA_W + KB_W + KC_W
R_W = Q_W + K_W
R_A = QA_W + KA_W
V_W = A_KV * HEAD_DIM + B_KV * HEAD_DIM + C_HEADS * 2 * HEAD_DIM
K_HEADS = K_W // HEAD_DIM

TOK_TILE = 256
ATTN_TQ = 256
DIFF_HEADS_PER_STEP = 4
KEY_CHUNK = 512
MAX_KEY_CHUNK = {"global": 8448, "diff": 8448, "window": 128}
LOG2E = math.log2(math.e)
VMEM_LIMIT = 56 * 1024 * 1024


def _cparams(sem, **kw):
    return pltpu.CompilerParams(dimension_semantics=sem, vmem_limit_bytes=VMEM_LIMIT, **kw)


def _dot(a, b):
    return jnp.dot(a, b, preferred_element_type=F32)


def _sigmoid(x):
    return 1.0 / (1.0 + jnp.exp(-x))


def _silu(x):
    return x * _sigmoid(x)


def _rms_rows(x, g):
    return x * lax.rsqrt(jnp.mean(x * x, axis=-1, keepdims=True) + EPS) * g


def _mod_kernel(c_ref, w_ref, b_ref, o_ref):
    cs = _silu(c_ref[...])
    o_ref[0] = jnp.dot(cs, w_ref[0], preferred_element_type=F32,
                       precision=lax.Precision.HIGHEST) + b_ref[0]


def _modulation(cvec, w_mod, b_mod):
    depth, d, n = w_mod.shape
    rows = cvec.shape[0]
    bn = 1536
    return pl.pallas_call(
        _mod_kernel,
        grid=(depth, n // bn),
        in_specs=[
            pl.BlockSpec((rows, d), lambda l, j: (0, 0)),
            pl.BlockSpec((1, d, bn), lambda l, j: (l, 0, j)),
            pl.BlockSpec((1, 1, bn), lambda l, j: (l, 0, j)),
        ],
        out_specs=pl.BlockSpec((1, rows, bn), lambda l, j: (l, 0, j)),
        out_shape=jax.ShapeDtypeStruct((depth, rows, n), F32),
        compiler_params=_cparams(("parallel", "parallel")),
        name="modulation",
    )(cvec, w_mod, b_mod.reshape(depth, 1, n))


def _pre_kernel(x_ref, mod_ref, g1_ref, wqkv_ref, ones_ref, grow_ref, gsrow_ref,
                cos_ref, sin_ref, q_out, kt_out, vab_out, vc_out, *, d):
    x = x_ref[0]
    mod = mod_ref[0]
    tn = x.shape[0]
    a = (_rms_rows(x, g1_ref[...]) * (1.0 + mod[:, d:2 * d]) + mod[:, 0:d]).astype(BF16)
    p = _dot(a, wqkv_ref[...])
    pa = jnp.concatenate([p[:, 0:QA_W], p[:, Q_W:Q_W + KA_W]], axis=1)
    sq = pa * pa
    hi = sq.astype(BF16)
    lo = (sq - hi.astype(F32)).astype(BF16)
    ssq = _dot(hi, ones_ref[...]) + _dot(lo, ones_ref[...])
    rinv = lax.rsqrt(ssq * (1.0 / HEAD_DIM) + EPS)
    cos = cos_ref[...]
    sin = sin_ref[...]
    lane = lax.broadcasted_iota(jnp.int32, (tn, 128), 1)
    low_half = (lane % (2 * ROPE_FREQS)) < ROPE_FREQS
    first_head = lane < HEAD_DIM
    for j in range(R_W // 128):
        sl = slice(j * 128, (j + 1) * 128)
        pj = p[:, sl]
        ps = jnp.where(low_half, pltpu.roll(pj, 128 - ROPE_FREQS, 1), pltpu.roll(pj, ROPE_FREQS, 1))
        o = pj * (grow_ref[:, sl] * cos) + ps * (gsrow_ref[:, sl] * sin)
        if (j + 1) * 128 <= QA_W:
            o = o * rinv[:, sl]
        elif j * 128 == Q_W:
            o = o * rinv[:, QA_W:QA_W + KA_W]
        if j * 128 < Q_W:
            q_out[0, :, sl] = o.astype(BF16)
        else:
            ot = jnp.transpose(o).astype(BF16)
            kh = (j * 128 - Q_W) // HEAD_DIM
            kt_out[0, kh] = ot[0:HEAD_DIM]
            kt_out[0, kh + 1] = ot[HEAD_DIM:2 * HEAD_DIM]
    ones_at_64 = (lane == HEAD_DIM).astype(F32)
    for j in range(2):
        vj = p[:, R_W + j * 128:R_W + (j + 1) * 128]
        vab_out[0, 2 * j] = jnp.where(first_head, vj, ones_at_64).astype(BF16)
        vab_out[0, 2 * j + 1] = jnp.where(first_head, pltpu.roll(vj, HEAD_DIM, 1), ones_at_64).astype(BF16)
    ones_at_0 = (lane == 0).astype(BF16)
    for j in range(C_HEADS):
        vj = p[:, R_W + (2 + j) * 128:R_W + (3 + j) * 128]
        vc_out[0, j] = jnp.concatenate([vj.astype(BF16), ones_at_0], axis=1)


def _pre_attention(x, mod, g1, wqkv, ones_blk, grow, gsrow, cos_t, sin_t, n_ctx_tiles):
    b, n, d = x.shape
    tn = TOK_TILE
    nb = b

    def mod_idx(bi, t):
        return (jnp.where(t < n_ctx_tiles, nb, bi), 0, 0)

    return pl.pallas_call(
        functools.partial(_pre_kernel, d=d),
        grid=(b, n // tn),
        in_specs=[
            pl.BlockSpec((1, tn, d), lambda bi, t: (bi, t, 0)),
            pl.BlockSpec((1, 1, N_MOD * d), mod_idx),
            pl.BlockSpec((1, d), lambda bi, t: (0, 0)),
            pl.BlockSpec((d, R_W + V_W), lambda bi, t: (0, 0)),
            pl.BlockSpec((R_A, R_A), lambda bi, t: (0, 0)),
            pl.BlockSpec((1, R_W), lambda bi, t: (0, 0)),
            pl.BlockSpec((1, R_W), lambda bi, t: (0, 0)),
            pl.BlockSpec((tn, 128), lambda bi, t: (t, 0)),
            pl.BlockSpec((tn, 128), lambda bi, t: (t, 0)),
        ],
        out_specs=[
            pl.BlockSpec((1, tn, Q_W), lambda bi, t: (bi, t, 0)),
            pl.BlockSpec((1, K_HEADS, HEAD_DIM, tn), lambda bi, t: (bi, 0, 0, t)),
            pl.BlockSpec((1, A_KV + B_KV, tn, 128), lambda bi, t: (bi, 0, t, 0)),
            pl.BlockSpec((1, C_HEADS, tn, 256), lambda bi, t: (bi, 0, t, 0)),
        ],
        out_shape=[
            jax.ShapeDtypeStruct((b, n, Q_W), BF16),
            jax.ShapeDtypeStruct((b, K_HEADS, HEAD_DIM, n), BF16),
            jax.ShapeDtypeStruct((b, A_KV + B_KV, n, 128), BF16),
            jax.ShapeDtypeStruct((b, C_HEADS, n, 256), BF16),
        ],
        compiler_params=_cparams(("parallel", "parallel")),
        name="pre_attention",
    )(x, mod, g1, wqkv, ones_blk, grow, gsrow, cos_t, sin_t)


def _attn_kernel(sc_ref, gsub_ref, q_ref, kt_ref, v_ref, o_ref, m_sc, acc_sc, *, heads, hpv,
                 shared_k, mode, n_ctx, n_tok, tq, ck, dv):
    u = pl.program_id(1)
    qi = pl.program_id(2)
    is_lat = qi >= n_ctx // tq
    dvx = acc_sc.shape[-1]

    for g in range(heads):
        if mode == "window":
            m_sc[g] = jnp.full((tq, 1), sc_ref[u * heads + g], F32)
            lane = lax.broadcasted_iota(jnp.int32, (tq, dvx), 1)
            acc_sc[g] = jnp.where(lane == dv, 1.0, 0.0).astype(F32)
        else:
            m_sc[g] = jnp.full((tq, 1), NEG, F32)
            acc_sc[g] = jnp.zeros((tq, dvx), F32)

    def probs(g, s, mask):
        if mask is not None:
            s = jnp.where(mask, s, NEG)
        m = m_sc[g]
        m_new = jnp.maximum(m, jnp.max(s, axis=-1, keepdims=True))
        m_sc[g] = m_new
        return jnp.exp2(s - m_new).astype(BF16), jnp.exp2(m - m_new)

    def q_of(g):
        return q_ref[0, :, g * HEAD_DIM:(g + 1) * HEAD_DIM]

    def update_all(kt_of, v_of, mask=None):
        s_next = _dot(q_of(0), kt_of(0))
        for g in range(heads):
            s = s_next
            if g + 1 < heads:
                s_next = _dot(q_of(g + 1), kt_of(g + 1))
            p, alpha = probs(g, s, mask)
            acc_sc[g] = alpha * acc_sc[g] + _dot(p, v_of(g))

    def step(k0, size):
        update_all(lambda g: kt_ref[0, 0 if shared_k else g, :, pl.ds(k0, size)],
                   lambda g: v_ref[0, g // hpv, pl.ds(k0, size), :])

    if mode == "window":
        span = tq + 2 * WINDOW
        start = pl.multiple_of(jnp.clip(qi * tq - WINDOW, 0, n_tok - span), 128)
        qpos = qi * tq + lax.broadcasted_iota(jnp.int32, (tq, n_ctx + span), 0)
        col = lax.broadcasted_iota(jnp.int32, (tq, n_ctx + span), 1)
        kpos = start + col - n_ctx
        mask = (col < n_ctx) | (is_lat & (kpos >= n_ctx) & (jnp.abs(kpos - qpos) <= WINDOW))
        v = jnp.concatenate([v_ref[0, 0, 0:n_ctx, :], v_ref[0, 0, pl.ds(start, span), :]], axis=0)

        def kt_of(g):
            gk = 0 if shared_k else g
            return jnp.concatenate([kt_ref[0, gk, :, 0:n_ctx], kt_ref[0, gk, :, pl.ds(start, span)]], axis=1)

        update_all(kt_of, lambda g: v, mask)
    else:
        @pl.when(jnp.logical_not(is_lat))
        def _():
            step(0, n_ctx)

        def body(ci, carry):
            step(pl.multiple_of(ci * ck, 128), ck)
            return carry

        lax.fori_loop(0, jnp.where(is_lat, n_tok // ck, 0), body, 0)

    def result(g):
        acc = acc_sc[g]
        return acc[:, 0:128] / acc[:, dv:dv + 1]

    if mode == "diff":
        lam = sc_ref[heads]
        post = sc_ref[heads + 1]
        for j in range(heads // 2):
            y = result(2 * j) - lam * result(2 * j + 1)
            o_ref[0, :, j * dv:(j + 1) * dv] = (_rms_rows(y, gsub_ref[...]) * post).astype(o_ref.dtype)
    else:
        lane = lax.broadcasted_iota(jnp.int32, (tq, 128), 1)
        for j in range(heads // 2):
            pair = jnp.where(lane < dv, result(2 * j), pltpu.roll(result(2 * j + 1), dv, 1))
            o_ref[0, :, j * 128:(j + 1) * 128] = pair.astype(o_ref.dtype)


def _attention(scalars, gsub, q, kt, vx, *, heads, hpv, shared_k, mode, n_ctx, units, q_unit0, k_unit0,
               v_unit0):
    b, n, _ = q.shape
    hd = HEAD_DIM
    dvx = vx.shape[-1]
    dv = gsub.shape[-1]
    assert dv < dvx
    tq = ATTN_TQ
    gk = 1 if shared_k else heads
    vh = heads // hpv
    out_w = heads * hd if mode != "diff" else (heads // 2) * dv
    ck = max(c for c in range(128, MAX_KEY_CHUNK[mode] + 1, 128) if n % c == 0)
    kern = functools.partial(_attn_kernel, heads=heads, hpv=hpv, shared_k=shared_k, mode=mode,
                             n_ctx=n_ctx, n_tok=n, tq=tq, ck=ck, dv=dv)
    return pl.pallas_call(
        kern,
        grid=(b, units, n // tq),
        in_specs=[
            pl.BlockSpec(memory_space=pltpu.SMEM),
            pl.BlockSpec((1, dv), lambda bi, u, t: (0, 0)),
            pl.BlockSpec((1, tq, heads * hd), lambda bi, u, t: (bi, t, q_unit0 + u)),
            pl.BlockSpec((1, gk, hd, n), lambda bi, u, t: (bi, k_unit0 + u, 0, 0)),
            pl.BlockSpec((1, vh, n, dvx), lambda bi, u, t: (bi, v_unit0 + u, 0, 0)),
        ],
        out_specs=pl.BlockSpec((1, tq, out_w), lambda bi, u, t: (bi, t, u)),
        out_shape=jax.ShapeDtypeStruct((b, n, units * out_w), BF16),
        scratch_shapes=[
            pltpu.VMEM((heads, tq, 1), F32),
            pltpu.VMEM((heads, tq, dvx), F32),
        ],
        compiler_params=_cparams(("parallel", "parallel", "arbitrary")),
        name="attn_" + mode,
    )(scalars, gsub, q, kt, vx)


def _merge_kernel(x_ref, mod_ref, g1_ref, ya_ref, yb_ref, yc_ref, wg_ref, bg_ref, wa_ref, wb_ref,
                  wc_ref, wo_ref, o_ref, *, d):
    x = x_ref[0]
    mod = mod_ref[0]
    a = (_rms_rows(x, g1_ref[...]) * (1.0 + mod[:, d:2 * d]) + mod[:, 0:d]).astype(BF16)
    gate = _sigmoid(_dot(a, wg_ref[...]) + bg_ref[...])
    m = (gate[:, 0:d] * _dot(ya_ref[0], wa_ref[...])
         + gate[:, d:2 * d] * _dot(yb_ref[0], wb_ref[...])
         + gate[:, 2 * d:3 * d] * _dot(yc_ref[0], wc_ref[...]))
    mix = _dot(m.astype(BF16), wo_ref[...])
    o_ref[0] = x + mod[:, 2 * d:3 * d] * mix


def _merge(x, mod, g1, ya, yb, yc, wg, bg, wa, wb, wc, wo, n_ctx_tiles):
    b, n, d = x.shape
    tn = TOK_TILE
    nb = b
    yw = ya.shape[-1]

    def mod_idx(bi, t):
        return (jnp.where(t < n_ctx_tiles, nb, bi), 0, 0)

    tok = lambda bi, t: (bi, t, 0)
    const = lambda bi, t: (0, 0)
    return pl.pallas_call(
        functools.partial(_merge_kernel, d=d),
        grid=(b, n // tn),
        in_specs=[
            pl.BlockSpec((1, tn, d), tok),
            pl.BlockSpec((1, 1, N_MOD * d), mod_idx),
            pl.BlockSpec((1, d), const),
            pl.BlockSpec((1, tn, yw), tok),
            pl.BlockSpec((1, tn, yw), tok),
            pl.BlockSpec((1, tn, yw), tok),
            pl.BlockSpec((d, 3 * d), const),
            pl.BlockSpec((1, 3 * d), const),
            pl.BlockSpec((yw, d), const),
            pl.BlockSpec((yw, d), const),
            pl.BlockSpec((yw, d), const),
            pl.BlockSpec((d, d), const),
        ],
        out_specs=pl.BlockSpec((1, tn, d), tok),
        out_shape=jax.ShapeDtypeStruct((b, n, d), F32),
        compiler_params=_cparams(("parallel", "parallel")),
        name="merge",
    )(x, mod, g1, ya, yb, yc, wg, bg, wa, wb, wc, wo)


def _first_index(hit, idx, big):
    return jnp.min(jnp.where(hit, idx, big), axis=0, keepdims=True)


def _route_kernel(x_ref, mod_ref, g2_ref, wrt_ref, eb_ref, ws1_ref, ws3_ref, ws2_ref,
                  f_ref, idx_ref, w_ref, xs_ref, cnt_ref, *, d):
    x = x_ref[0]
    mod = mod_ref[0]
    f = _rms_rows(x, g2_ref[...]) * (1.0 + mod[:, 4 * d:5 * d]) + mod[:, 3 * d:4 * d]
    tn = f.shape[0]
    for s in range(d // 128):
        f_ref[0, pl.ds(s, tn, stride=d // 128), :] = f[:, s * 128:(s + 1) * 128]
    logits = lax.dot_general(wrt_ref[...], f, (((1,), (1,)), ((), ())),
                             preferred_element_type=F32, precision=lax.Precision.HIGHEST)
    scores = _sigmoid(logits)
    choice = scores + eb_ref[...]
    eidx = lax.broadcasted_iota(jnp.int32, (N_EXPERTS, tn), 0)
    lidx = lax.broadcasted_iota(jnp.int32, (GROUP_SIZE, tn), 0)
    gscore = []
    for g in range(N_GROUPS):
        cg = choice[g * GROUP_SIZE:(g + 1) * GROUP_SIZE, :]
        m1 = jnp.max(cg, axis=0, keepdims=True)
        first = _first_index(cg == m1, lidx, GROUP_SIZE)
        m2 = jnp.max(jnp.where(lidx == first, NEG, cg), axis=0, keepdims=True)
        gscore.append(m1 + m2)
    gs = jnp.concatenate(gscore, axis=0)
    gidx = lax.broadcasted_iota(jnp.int32, (N_GROUPS, tn), 0)
    gsel = jnp.zeros((N_GROUPS, tn), jnp.bool_)
    for _ in range(TOPK_GROUPS):
        gm = jnp.max(gs, axis=0, keepdims=True)
        first = _first_index(gs == gm, gidx, N_GROUPS)
        hit = gidx == first
        gsel = gsel | hit
        gs = jnp.where(hit, NEG, gs)
    gself = gsel.astype(F32)
    emask = jnp.concatenate(
        [jnp.broadcast_to(gself[g:g + 1, :], (GROUP_SIZE, tn)) for g in range(N_GROUPS)], axis=0)
    cur = jnp.where(emask > 0.5, choice, NEG)
    ids, ws = [], []
    chosen = jnp.zeros((N_EXPERTS, tn), F32)
    for _ in range(TOP_K):
        m = jnp.max(cur, axis=0, keepdims=True)
        first = _first_index(cur == m, eidx, N_EXPERTS)
        hit = eidx == first
        ids.append(first)
        ws.append(jnp.sum(jnp.where(hit, scores, 0.0), axis=0, keepdims=True))
        cur = jnp.where(hit, NEG, cur)
        chosen = chosen + hit.astype(F32)
    wsel = jnp.concatenate(ws, axis=0)
    idx_ref[0] = jnp.concatenate(ids, axis=0)
    cnt_ref[0, 0] = jnp.sum(chosen, axis=1, keepdims=True).astype(jnp.int32)
    w_ref[0] = wsel / jnp.sum(wsel, axis=0, keepdims=True) * ROUTE_SCALE
    fb = f.astype(BF16)
    h = _silu(_dot(fb, ws1_ref[...])) * _dot(fb, ws3_ref[...])
    xs_ref[0] = x + mod[:, 5 * d:6 * d] * _dot(h.astype(BF16), ws2_ref[...])


def _route(x, mod, g2, wrt, eb, ws1, ws3, ws2, n_ctx_tiles):
    b, n, d = x.shape
    tn = TOK_TILE
    nb = b
    ds = ws1.shape[-1]

    def mod_idx(bi, t):
        return (jnp.where(t < n_ctx_tiles, nb, bi), 0, 0)

    tok = lambda bi, t: (bi, t, 0)
    lane_tok = lambda bi, t: (bi, 0, t)
    const = lambda bi, t: (0, 0)
    return pl.pallas_call(
        functools.partial(_route_kernel, d=d),
        grid=(b, n // tn),
        in_specs=[
            pl.BlockSpec((1, tn, d), tok),
            pl.BlockSpec((1, 1, N_MOD * d), mod_idx),
            pl.BlockSpec((1, d), const),
            pl.BlockSpec((N_EXPERTS, d), const),
            pl.BlockSpec((N_EXPERTS, 1), const),
            pl.BlockSpec((d, ds), const),
            pl.BlockSpec((d, ds), const),
            pl.BlockSpec((ds, d), const),
        ],
        out_specs=[
            pl.BlockSpec((1, tn * (d // 128), 128), tok),
            pl.BlockSpec((1, TOP_K, tn), lane_tok),
            pl.BlockSpec((1, TOP_K, tn), lane_tok),
            pl.BlockSpec((1, tn, d), tok),
            pl.BlockSpec((1, 1, N_EXPERTS, 1), lambda bi, t: (bi, t, 0, 0)),
        ],
        out_shape=[
            jax.ShapeDtypeStruct((b, n * (d // 128), 128), F32),
            jax.ShapeDtypeStruct((b, TOP_K, n), jnp.int32),
            jax.ShapeDtypeStruct((b, TOP_K, n), F32),
            jax.ShapeDtypeStruct((b, n, d), F32),
            jax.ShapeDtypeStruct((b, n // tn, N_EXPERTS, 1), jnp.int32),
        ],
        compiler_params=_cparams(("parallel", "parallel")),
        name="route_shared",
    )(x, mod, g2, wrt, eb, ws1, ws3, ws2)


def _expert_kernel(iexp_ref, iblk_ref, iflag_ref, nitems_ref, tok_ref, tok_next_ref, tok_ahead_ref,
                   dst_prev_ref, dst_ref, erow_ref, f_hbm, w1_ref, w3_ref, w2_ref, y_hbm, xbuf, ybuf,
                   wb1, wb3, wb2, gsem, ssem):
    i = pl.program_id(0)
    nitems = nitems_ref[0]
    blk = iblk_ref[i]
    expert = iexp_ref[i]
    first = (iflag_ref[i] & 1) != 0
    last = (iflag_ref[i] & 2) != 0
    ring = xbuf.shape[0]
    slot = blk % ring
    slot_prev = (blk + ring - 1) % ring
    slot_prev2 = (blk + ring - 2) % ring
    slot_ahead = (blk + 2) % ring
    rows = EXPERT_BLOCK
    parts = xbuf.shape[1] // rows
    yparts = ybuf.shape[1] // rows
    n_blocks = y_hbm.shape[0] // (rows * yparts)

    def gather(idx_ref, s):
        for j in range(rows):
            src = pl.multiple_of(idx_ref[0, 0, j], parts)
            pltpu.make_async_copy(f_hbm.at[pl.ds(src, parts)], xbuf.at[s, pl.ds(j * parts, parts)],
                                  gsem.at[s]).start(priority=j % 2)

    def gather_wait(s):
        pltpu.make_async_copy(f_hbm.at[pl.ds(0, rows * parts)], xbuf.at[s], gsem.at[s]).wait()

    def scatter(idx_ref, s):
        for j in range(rows):
            dst = pl.multiple_of(idx_ref[0, 0, j], yparts)
            pltpu.make_async_copy(ybuf.at[s, pl.ds(j * yparts, yparts)], y_hbm.at[pl.ds(dst, yparts)],
                                  ssem.at[s]).start(priority=j % 2)

    def scatter_wait(s):
        pltpu.make_async_copy(ybuf.at[s], y_hbm.at[pl.ds(0, rows * yparts)], ssem.at[s]).wait()

    def expert_rows(s, after_up=None, after_down=None):
        xb = jnp.concatenate([xbuf[s, pl.ds(p, rows, stride=parts), :] for p in range(parts)],
                             axis=1).astype(BF16)
        h1 = _dot(xb, wb1[...])
        h3 = _dot(xb, wb3[...])
        if after_up is not None:
            after_up()
        y = _dot((_silu(h1) * h3).astype(BF16), wb2[...])
        if after_down is not None:
            after_down()
        return y

    def bf16_bits(v):
        return lax.bitcast_convert_type(v.astype(BF16).astype(F32), jnp.uint32)

    def store_rows(s, y, own_rows=None):
        for p in range(yparts):
            lo = bf16_bits(y[:, p * 128:(p + 1) * 128]) >> 16
            hi = bf16_bits(y[:, (yparts + p) * 128:(yparts + p + 1) * 128]) & jnp.uint32(0xFFFF0000)
            part = hi | lo
            if own_rows is not None:
                part = jnp.where(own_rows, part, ybuf[s, pl.ds(p, rows, stride=yparts), :])
            ybuf[s, pl.ds(p, rows, stride=yparts), :] = part

    valid = i < nitems

    @pl.when(valid & ((iflag_ref[i] & 4) != 0))
    def _():
        wb1[...] = w1_ref[0, 0].astype(BF16)
        wb3[...] = w3_ref[0, 0].astype(BF16)
        wb2[...] = w2_ref[0, 0].astype(BF16)

    fast = valid & first & last & (blk >= 3) & (blk <= n_blocks - 3)

    @pl.when(fast)
    def _():
        gather_wait(slot)
        scatter_wait(slot)
        y = expert_rows(slot, after_up=lambda: gather(tok_ahead_ref, slot_ahead),
                        after_down=lambda: scatter(dst_prev_ref, slot_prev))
        store_rows(slot, y)

    @pl.when(valid & jnp.logical_not(fast))
    def _():
        @pl.when(first)
        def _():
            @pl.when(blk == 0)
            def _():
                gather(tok_ref, 0)
                if n_blocks > 1:
                    gather(tok_next_ref, 1)

            @pl.when(blk + 2 < n_blocks)
            def _():
                gather(tok_ahead_ref, slot_ahead)

            @pl.when(blk >= 1)
            def _():
                scatter(dst_prev_ref, slot_prev)

            gather_wait(slot)

            @pl.when(blk >= 3)
            def _():
                scatter_wait(slot)

        y = expert_rows(slot)

        @pl.when(first)
        def _():
            store_rows(slot, y)

        @pl.when(jnp.logical_not(first))
        def _():
            store_rows(slot, y, own_rows=erow_ref[0] == expert)

        @pl.when(last & (blk == n_blocks - 1))
        def _():
            scatter(dst_ref, slot)

            @pl.when(blk >= 2)
            def _():
                scatter_wait(slot_prev2)

            @pl.when(blk >= 1)
            def _():
                scatter_wait(slot_prev)

            scatter_wait(slot)


def _experts(iexp, iblk, iflag, nitems, row_tok, row_dst, erow, f_rows, w1, w3, w2, layer):
    n_items = iexp.shape[0]
    n_blocks = row_tok.shape[0]
    d, de = w1.shape[2], w1.shape[3]
    parts = d // 128
    yparts = parts // 2
    rows = EXPERT_BLOCK
    last = n_blocks - 1
    cur = lambda i, ie, ib, fl, nt: (ib[i], 0, 0)
    nxt = lambda i, ie, ib, fl, nt: (jnp.minimum(ib[i] + 1, last), 0, 0)
    ahd = lambda i, ie, ib, fl, nt: (jnp.minimum(ib[i] + 2, last), 0, 0)
    prv = lambda i, ie, ib, fl, nt: (jnp.maximum(ib[i] - 1, 0), 0, 0)
    wsel = lambda i, ie, ib, fl, nt: (layer, ie[i], 0, 0)
    grid_spec = pltpu.PrefetchScalarGridSpec(
        num_scalar_prefetch=4,
        grid=(n_items,),
        in_specs=[
            pl.BlockSpec((1, 1, rows), cur, memory_space=pltpu.SMEM),
            pl.BlockSpec((1, 1, rows), nxt, memory_space=pltpu.SMEM),
            pl.BlockSpec((1, 1, rows), ahd, memory_space=pltpu.SMEM),
            pl.BlockSpec((1, 1, rows), prv, memory_space=pltpu.SMEM),
            pl.BlockSpec((1, 1, rows), cur, memory_space=pltpu.SMEM),
            pl.BlockSpec((1, rows, 1), cur),
            pl.BlockSpec(memory_space=pl.ANY),
            pl.BlockSpec((1, 1, d, de), wsel),
            pl.BlockSpec((1, 1, d, de), wsel),
            pl.BlockSpec((1, 1, de, d), wsel),
        ],
        out_specs=pl.BlockSpec(memory_space=pl.ANY),
        scratch_shapes=[
            pltpu.VMEM((EXPERT_RING, rows * parts, 128), F32),
            pltpu.VMEM((EXPERT_RING, rows * yparts, 128), jnp.uint32),
            pltpu.VMEM((d, de), BF16),
            pltpu.VMEM((d, de), BF16),
            pltpu.VMEM((de, d), BF16),
            pltpu.SemaphoreType.DMA((EXPERT_RING,)),
            pltpu.SemaphoreType.DMA((EXPERT_RING,)),
        ],
    )
    return pl.pallas_call(
        _expert_kernel,
        grid_spec=grid_spec,
        out_shape=jax.ShapeDtypeStruct((n_blocks * rows * yparts, 128), jnp.uint32),
        compiler_params=_cparams(("arbitrary",)),
        name="experts",
    )(iexp, iblk, iflag, nitems, row_tok, row_tok, row_tok, row_dst, row_dst, erow, f_rows, w1, w3, w2)


ASSIGN_BITS = 20


def _routing_tables(idx, counts, n_tokens, parts):
    n_assign = n_tokens * TOP_K
    blk = EXPERT_BLOCK
    assert n_assign % blk == 0 and n_assign <= (1 << ASSIGN_BITS)
    n_blocks = n_assign // blk
    n_items = n_blocks + N_EXPERTS
    flat_e = idx.reshape(-1).astype(jnp.int32)
    key = jnp.sort((flat_e << ASSIGN_BITS) | jnp.arange(n_assign, dtype=jnp.int32))
    e_sorted = key >> ASSIGN_BITS
    order = key & ((1 << ASSIGN_BITS) - 1)
    tok = order // TOP_K
    row_dst = ((order % TOP_K) * n_tokens + tok) * (parts // 2)
    tok = tok * parts

    experts = jnp.arange(N_EXPERTS, dtype=jnp.int32)
    ends = jnp.cumsum(counts)
    starts = ends - counts
    first_blk = starts // blk
    n_be = jnp.where(ends > starts, (ends - 1) // blk - first_blk + 1, 0)
    item_end = jnp.cumsum(n_be)
    item_off = item_end - n_be
    total = item_end[-1]
    i = jnp.arange(n_items, dtype=jnp.int32)
    iexp = jnp.minimum(jnp.sum((item_end[None, :] <= i[:, None]).astype(jnp.int32), axis=1), N_EXPERTS - 1)
    pick = (iexp[:, None] == experts[None, :]).astype(jnp.int32)
    base = jnp.sum(pick * (first_blk - item_off)[None, :], axis=1)
    iblk = jnp.where(i < total, base + i, n_blocks - 1).astype(jnp.int32)
    prev_blk = jnp.concatenate([jnp.full((1,), -1, jnp.int32), iblk[:-1]])
    next_blk = jnp.concatenate([iblk[1:], jnp.full((1,), -1, jnp.int32)])
    is_first = iblk != prev_blk
    is_last = (iblk != next_blk) | (i == total - 1)
    new_expert = iexp != jnp.concatenate([jnp.full((1,), -1, jnp.int32), iexp[:-1]])
    iflag = is_first.astype(jnp.int32) + 2 * is_last.astype(jnp.int32) + 4 * new_expert.astype(jnp.int32)
    return (iexp, iblk, iflag, total.astype(jnp.int32).reshape(1), tok.reshape(n_blocks, 1, blk),
            row_dst.reshape(n_blocks, 1, blk), e_sorted.reshape(n_blocks, blk, 1))


def _combine_kernel(xs_ref, mod_ref, w_ref, *refs, d):
    y_refs, o_ref = refs[:TOP_K], refs[TOP_K]
    w = w_ref[0]
    tn = w.shape[0]
    yparts = d // 256
    wk = [jnp.broadcast_to(w[:, k:k + 1], (tn, 128)) for k in range(TOP_K)]
    for p in range(yparts):
        lo = jnp.zeros((tn, 128), F32)
        hi = jnp.zeros((tn, 128), F32)
        for k in range(TOP_K):
            word = y_refs[k][pl.ds(p, tn, stride=yparts), :]
            lo = lo + lax.bitcast_convert_type(word << 16, F32) * wk[k]
            hi = hi + lax.bitcast_convert_type(word & jnp.uint32(0xFFFF0000), F32) * wk[k]
        for q, tot in ((p, lo), (yparts + p, hi)):
            sl = slice(q * 128, (q + 1) * 128)
            o_ref[0, :, sl] = xs_ref[0, :, sl] + mod_ref[0][:, 5 * d + q * 128:5 * d + (q + 1) * 128] * tot


def _combine(xs, mod, w, y, n_ctx_tiles):
    b, n, d = xs.shape
    tn = TOK_TILE
    nb = b
    tiles = n // tn
    per_slot = b * tiles

    def mod_idx(bi, t):
        return (jnp.where(t < n_ctx_tiles, nb, bi), 0, 0)

    tok = lambda bi, t: (bi, t, 0)
    y_specs = [pl.BlockSpec((tn * (d // 256), 128),
                            functools.partial(lambda bi, t, k: (k * per_slot + bi * tiles + t, 0), k=k))
               for k in range(TOP_K)]
    return pl.pallas_call(
        functools.partial(_combine_kernel, d=d),
        grid=(b, tiles),
        in_specs=[pl.BlockSpec((1, tn, d), tok), pl.BlockSpec((1, 1, N_MOD * d), mod_idx),
                  pl.BlockSpec((1, tn, TOP_K), tok)] + y_specs,
        out_specs=pl.BlockSpec((1, tn, d), tok),
        out_shape=jax.ShapeDtypeStruct((b, n, d), F32),
        compiler_params=_cparams(("parallel", "parallel")),
        name="combine",
    )(xs, mod, w, *([y] * TOP_K))


def _final_kernel(x_ref, g_ref, o_ref):
    o_ref[0] = _rms_rows(x_ref[0], g_ref[...])


def _final_norm(x, g, n_ctx_tiles):
    b, n, d = x.shape
    tn = TOK_TILE
    n_lat_tiles = n // tn - n_ctx_tiles
    return pl.pallas_call(
        _final_kernel,
        grid=(b, n_lat_tiles),
        in_specs=[pl.BlockSpec((1, tn, d), lambda bi, t: (bi, t + n_ctx_tiles, 0)),
                  pl.BlockSpec((1, d), lambda bi, t: (0, 0))],
        out_specs=pl.BlockSpec((1, tn, d), lambda bi, t: (bi, t, 0)),
        out_shape=jax.ShapeDtypeStruct((b, n_lat_tiles * tn, d), F32),
        compiler_params=_cparams(("parallel", "parallel")),
        name="final_norm",
    )(x, g)


def _regroup_qkv_columns(w):
    a_q, a_kv = A_HEADS * HEAD_DIM, A_KV * HEAD_DIM
    b_q, b_kv = B_HEADS * HEAD_DIM, B_KV * HEAD_DIM
    c_qk, c_v = 2 * C_HEADS * HEAD_DIM, C_HEADS * 2 * HEAD_DIM
    off = np.concatenate([[0], np.cumsum((a_q, a_kv, a_kv, b_q, b_kv, b_kv, c_qk, c_qk, c_v))])
    seg = lambda i: w[:, int(off[i]):int(off[i + 1])]
    return jnp.concatenate([seg(0), seg(3), seg(6), seg(1), seg(4), seg(7), seg(2), seg(5), seg(8)], axis=1)


def _rotary_partner(row):
    blocks = row.reshape(-1, 2, ROPE_FREQS)
    return blocks[:, ::-1, :].reshape(-1)


def _rope_tables(n_ctx, n_lat):
    t = jnp.arange(n_lat, dtype=jnp.int32)
    row_pos = (t // GRID_W).astype(F32)
    col_pos = (t % GRID_W).astype(F32)
    inv_freq = jnp.power(ROPE_THETA, -jnp.arange(ROPE_FREQS, dtype=F32) / ROPE_FREQS)
    ang_r = row_pos[:, None] * inv_freq
    ang_c = col_pos[:, None] * inv_freq
    cos64 = jnp.concatenate([jnp.cos(ang_r), jnp.cos(ang_r), jnp.cos(ang_c), jnp.cos(ang_c)], axis=1)
    sin64 = jnp.concatenate([-jnp.sin(ang_r), jnp.sin(ang_r), -jnp.sin(ang_c), jnp.sin(ang_c)], axis=1)
    cos64 = jnp.concatenate([jnp.ones((n_ctx, HEAD_DIM), F32), cos64], axis=0)
    sin64 = jnp.concatenate([jnp.zeros((n_ctx, HEAD_DIM), F32), sin64], axis=0)
    return jnp.tile(cos64, (1, 2)), jnp.tile(sin64, (1, 2))


def kernel(x, c, ctx, c_ctx, w_mod, b_mod, g_norm1, w_qkv, g_qnorm_a, g_knorm_a, sink_b, lam_q1, lam_k1, lam_q2, lam_k2, g_subln_c, w_br_a, w_br_b, w_br_c, w_gate, b_gate, w_out, g_norm2, w_router, e_bias, w1, w3, w2, ws1, ws3, ws2, g_final):
    bsz, n_lat, d = x.shape
    n_ctx = ctx.shape[1]
    depth = w_mod.shape[0]
    n_tok = n_ctx + n_lat
    assert n_ctx % TOK_TILE == 0 and n_lat % KEY_CHUNK == 0 and n_ctx % 128 == 0
    n_ctx_tiles = n_ctx // TOK_TILE
    n_all = bsz * n_tok

    rows = -(-(bsz + 1) // 8) * 8
    cvec = jnp.concatenate([c, c_ctx[None, :], jnp.zeros((rows - bsz - 1, d), F32)], axis=0)
    mod_all = _modulation(cvec, w_mod, b_mod)

    cos_t, sin_t = _rope_tables(n_ctx, n_lat)
    head_of = np.arange(R_A) // HEAD_DIM
    ones_blk = jnp.asarray((head_of[:, None] == head_of[None, :]).astype(np.float32), BF16)
    scale = HEAD_DIM ** -0.5 * LOG2E
    unit = jnp.ones((HEAD_DIM,), F32)

    xs = jnp.concatenate([ctx, x], axis=1)
    for l in range(depth):
        lam_init = 0.8 - 0.6 * math.exp(-0.3 * l)
        lam = (jnp.exp(jnp.dot(lam_q1[l], lam_k1[l])) - jnp.exp(jnp.dot(lam_q2[l], lam_k2[l]))).astype(F32) + lam_init
        mod = mod_all[l].reshape(rows, 1, N_MOD * d)
        g1 = g_norm1[l].reshape(1, d)

        wqkv = _regroup_qkv_columns(w_qkv[l]).astype(BF16)
        grow = jnp.concatenate([jnp.tile(g_qnorm_a[l] * scale, A_HEADS), jnp.tile(unit * scale, B_HEADS),
                                jnp.tile(unit * scale, 2 * C_HEADS), jnp.tile(g_knorm_a[l], A_KV),
                                jnp.tile(unit, B_KV), jnp.tile(unit, 2 * C_HEADS)])
        gsrow = _rotary_partner(grow)
        q_all, kt_all, vx_ab, vx_c = _pre_attention(xs, mod, g1, wqkv, ones_blk, grow.reshape(1, R_W),
                                                    gsrow.reshape(1, R_W), cos_t, sin_t, n_ctx_tiles)

        scal = jnp.concatenate([sink_b[l].astype(F32) * LOG2E, lam.reshape(1), jnp.full((1,), 1.0 - lam_init, F32)])
        scal_c = jnp.concatenate([jnp.zeros((DIFF_HEADS_PER_STEP,), F32), lam.reshape(1),
                                  jnp.full((1,), 1.0 - lam_init, F32)])
        gsub = g_subln_c[l].reshape(1, 2 * HEAD_DIM)
        g64 = jnp.ones((1, HEAD_DIM), F32)
        ya = _attention(scal, g64, q_all, kt_all, vx_ab, heads=A_HEADS // A_KV, hpv=A_HEADS // A_KV,
                        shared_k=True, mode="global", n_ctx=n_ctx, units=A_KV, q_unit0=0, k_unit0=0, v_unit0=0)
        yb = _attention(scal, g64, q_all, kt_all, vx_ab, heads=B_HEADS // B_KV, hpv=B_HEADS // B_KV,
                        shared_k=True, mode="window", n_ctx=n_ctx, units=B_KV, q_unit0=QA_W // 256,
                        k_unit0=A_KV, v_unit0=A_KV)
        yc = _attention(scal_c, gsub, q_all, kt_all, vx_c, heads=DIFF_HEADS_PER_STEP, hpv=2, shared_k=False,
                        mode="diff", n_ctx=n_ctx, units=2 * C_HEADS // DIFF_HEADS_PER_STEP,
                        q_unit0=(QA_W + QB_W) // 256, k_unit0=(A_KV + B_KV) // DIFF_HEADS_PER_STEP, v_unit0=0)

        x1 = _merge(xs, mod, g1, ya, yb, yc,
                    w_gate[l].astype(BF16), b_gate[l].reshape(1, -1), w_br_a[l].astype(BF16),
                    w_br_b[l].astype(BF16), w_br_c[l].astype(BF16), w_out[l].astype(BF16), n_ctx_tiles)

        f, idx_t, w_t, x_sh, tile_counts = _route(x1, mod, g_norm2[l].reshape(1, d), w_router[l].T,
                                     e_bias[l].reshape(N_EXPERTS, 1), ws1[l].astype(BF16),
                                     ws3[l].astype(BF16), ws2[l].astype(BF16), n_ctx_tiles)

        idx = idx_t.transpose(0, 2, 1).reshape(n_all, TOP_K)
        counts = jnp.sum(tile_counts, axis=(0, 1)).reshape(N_EXPERTS)
        iexp, iblk, iflag, nitems, row_tok, row_dst, erow = _routing_tables(idx, counts, n_all, d // 128)
        y = _experts(iexp, iblk, iflag, nitems, row_tok, row_dst, erow, f.reshape(n_all * (d // 128), 128),
                     w1, w3, w2, l)
        xs = _combine(x_sh, mod, w_t.transpose(0, 2, 1), y, n_ctx_tiles)

    return _final_norm(xs, g_final.reshape(1, d), n_ctx_tiles)
```

```python
import functools
import math

import numpy as np
import jax
import jax.numpy as jnp
from jax import lax
from jax.experimental import pallas as pl
from jax.experimental.pallas import tpu as pltpu

F32 = jnp.float32
BF16 = jnp.bfloat16

HEAD_DIM = 64
ROPE_FREQS = HEAD_DIM // 4
ROPE_THETA = 10000.0
GRID_W = 64
WINDOW = 128
A_HEADS, A_KV = 8, 2
B_HEADS, B_KV = 8, 2
C_HEADS = 4
N_EXPERTS = 128
TOP_K = 8
N_GROUPS = 8
TOPK_GROUPS = 4
GROUP_SIZE = N_EXPERTS // N_GROUPS
ROUTE_SCALE = 2.5
EXPERT_BLOCK = 128
EXPERT_RING = 3
N_MOD = 6
EPS = 1e-6
NEG = -1e30

QA_W, QB_W, QC_W = A_HEADS * HEAD_DIM, B_HEADS * HEAD_DIM, 2 * C_HEADS * HEAD_DIM
KA_W, KB_W, KC_W = A_KV * HEAD_DIM, B_KV * HEAD_DIM, 2 * C_HEADS * HEAD_DIM
Q_W = QA_W + QB_W + QC_W
K_W = KA_W + KB_W + KC_W
R_W = Q_W + K_W
R_A = QA_W + KA_W
V_W = A_KV * HEAD_DIM + B_KV * HEAD_DIM + C_HEADS * 2 * HEAD_DIM
K_HEADS = K_W // HEAD_DIM

TOK_TILE = 256
ATTN_TQ = 256
DIFF_HEADS_PER_STEP = 4
KEY_CHUNK = 512
MAX_KEY_CHUNK = {"global": 8448, "diff": 8448, "window": 128}
LOG2E = math.log2(math.e)
VMEM_LIMIT = 56 * 1024 * 1024


def _cparams(sem, **kw):
    return pltpu.CompilerParams(dimension_semantics=sem, vmem_limit_bytes=VMEM_LIMIT, **kw)


def _dot(a, b):
    return jnp.dot(a, b, preferred_element_type=F32)


def _sigmoid(x):
    return 1.0 / (1.0 + jnp.exp(-x))


def _silu(x):
    return x * _sigmoid(x)


def _rms_rows(x, g):
    return x * lax.rsqrt(jnp.mean(x * x, axis=-1, keepdims=True) + EPS) * g


def _mod_kernel(c_ref, w_ref, b_ref, o_ref):
    cs = _silu(c_ref[...])
    o_ref[0] = jnp.dot(cs, w_ref[0], preferred_element_type=F32,
                       precision=lax.Precision.HIGHEST) + b_ref[0]


def _modulation(cvec, w_mod, b_mod):
    depth, d, n = w_mod.shape
    rows = cvec.shape[0]
    bn = 1536
    return pl.pallas_call(
        _mod_kernel,
        grid=(depth, n // bn),
        in_specs=[
            pl.BlockSpec((rows, d), lambda l, j: (0, 0)),
            pl.BlockSpec((1, d, bn), lambda l, j: (l, 0, j)),
            pl.BlockSpec((1, 1, bn), lambda l, j: (l, 0, j)),
        ],
        out_specs=pl.BlockSpec((1, rows, bn), lambda l, j: (l, 0, j)),
        out_shape=jax.ShapeDtypeStruct((depth, rows, n), F32),
        compiler_params=_cparams(("parallel", "parallel")),
        name="modulation",
    )(cvec, w_mod, b_mod.reshape(depth, 1, n))


def _pre_kernel(x_ref, mod_ref, g1_ref, wqkv_ref, ones_ref, grow_ref, gsrow_ref,
                cos_ref, sin_ref, q_out, kt_out, vab_out, vc_out, *, d):
    x = x_ref[0]
    mod = mod_ref[0]
    tn = x.shape[0]
    a = (_rms_rows(x, g1_ref[...]) * (1.0 + mod[:, d:2 * d]) + mod[:, 0:d]).astype(BF16)
    p = _dot(a, wqkv_ref[...])
    pa = jnp.concatenate([p[:, 0:QA_W], p[:, Q_W:Q_W + KA_W]], axis=1)
    sq = pa * pa
    hi = sq.astype(BF16)
    lo = (sq - hi.astype(F32)).astype(BF16)
    ssq = _dot(hi, ones_ref[...]) + _dot(lo, ones_ref[...])
    rinv = lax.rsqrt(ssq * (1.0 / HEAD_DIM) + EPS)
    cos = cos_ref[...]
    sin = sin_ref[...]
    lane = lax.broadcasted_iota(jnp.int32, (tn, 128), 1)
    low_half = (lane % (2 * ROPE_FREQS)) < ROPE_FREQS
    first_head = lane < HEAD_DIM
    for j in range(R_W // 128):
        sl = slice(j * 128, (j + 1) * 128)
        pj = p[:, sl]
        ps = jnp.where(low_half, pltpu.roll(pj, 128 - ROPE_FREQS, 1), pltpu.roll(pj, ROPE_FREQS, 1))
        o = pj * (grow_ref[:, sl] * cos) + ps * (gsrow_ref[:, sl] * sin)
        if (j + 1) * 128 <= QA_W:
            o = o * rinv[:, sl]
        elif j * 128 == Q_W:
            o = o * rinv[:, QA_W:QA_W + KA_W]
        if j * 128 < Q_W:
            q_out[0, :, sl] = o.astype(BF16)
        else:
            ot = jnp.transpose(o).astype(BF16)
            kh = (j * 128 - Q_W) // HEAD_DIM
            kt_out[0, kh] = ot[0:HEAD_DIM]
            kt_out[0, kh + 1] = ot[HEAD_DIM:2 * HEAD_DIM]
    ones_at_64 = (lane == HEAD_DIM).astype(F32)
    for j in range(2):
        vj = p[:, R_W + j * 128:R_W + (j + 1) * 128]
        vab_out[0, 2 * j] = jnp.where(first_head, vj, ones_at_64).astype(BF16)
        vab_out[0, 2 * j + 1] = jnp.where(first_head, pltpu.roll(vj, HEAD_DIM, 1), ones_at_64).astype(BF16)
    ones_at_0 = (lane == 0).astype(BF16)
    for j in range(C_HEADS):
        vj = p[:, R_W + (2 + j) * 128:R_W + (3 + j) * 128]
        vc_out[0, j] = jnp.concatenate([vj.astype(BF16), ones_at_0], axis=1)


def _pre_attention(x, mod, g1, wqkv, ones_blk, grow, gsrow, cos_t, sin_t, n_ctx_tiles):
    b, n, d = x.shape
    tn = TOK_TILE
    nb = b

    def mod_idx(bi, t):
        return (jnp.where(t < n_ctx_tiles, nb, bi), 0, 0)

    return pl.pallas_call(
        functools.partial(_pre_kernel, d=d),
        grid=(b, n // tn),
        in_specs=[
            pl.BlockSpec((1, tn, d), lambda bi, t: (bi, t, 0)),
            pl.BlockSpec((1, 1, N_MOD * d), mod_idx),
            pl.BlockSpec((1, d), lambda bi, t: (0, 0)),
            pl.BlockSpec((d, R_W + V_W), lambda bi, t: (0, 0)),
            pl.BlockSpec((R_A, R_A), lambda bi, t: (0, 0)),
            pl.BlockSpec((1, R_W), lambda bi, t: (0, 0)),
            pl.BlockSpec((1, R_W), lambda bi, t: (0, 0)),
            pl.BlockSpec((tn, 128), lambda bi, t: (t, 0)),
            pl.BlockSpec((tn, 128), lambda bi, t: (t, 0)),
        ],
        out_specs=[
            pl.BlockSpec((1, tn, Q_W), lambda bi, t: (bi, t, 0)),
            pl.BlockSpec((1, K_HEADS, HEAD_DIM, tn), lambda bi, t: (bi, 0, 0, t)),
            pl.BlockSpec((1, A_KV + B_KV, tn, 128), lambda bi, t: (bi, 0, t, 0)),
            pl.BlockSpec((1, C_HEADS, tn, 256), lambda bi, t: (bi, 0, t, 0)),
        ],
        out_shape=[
            jax.ShapeDtypeStruct((b, n, Q_W), BF16),
            jax.ShapeDtypeStruct((b, K_HEADS, HEAD_DIM, n), BF16),
            jax.ShapeDtypeStruct((b, A_KV + B_KV, n, 128), BF16),
            jax.ShapeDtypeStruct((b, C_HEADS, n, 256), BF16),
        ],
        compiler_params=_cparams(("parallel", "parallel")),
        name="pre_attention",
    )(x, mod, g1, wqkv, ones_blk, grow, gsrow, cos_t, sin_t)


def _attn_kernel(sc_ref, gsub_ref, q_ref, kt_ref, v_ref, o_ref, m_sc, acc_sc, *, heads, hpv,
                 hpk, mode, n_ctx, n_tok, tq, ck, dv):
    u = pl.program_id(1)
    qi = pl.program_id(2)
    is_lat = qi >= n_ctx // tq
    dvx = acc_sc.shape[-1]

    for g in range(heads):
        if mode == "window":
            m_sc[g] = jnp.full((tq, 1), sc_ref[u * heads + g], F32)
            lane = lax.broadcasted_iota(jnp.int32, (tq, dvx), 1)
            acc_sc[g] = jnp.where(lane == dv, 1.0, 0.0).astype(F32)
        else:
            m_sc[g] = jnp.full((tq, 1), NEG, F32)
            acc_sc[g] = jnp.zeros((tq, dvx), F32)

    def probs(g, s, mask):
        if mask is not None:
            s = jnp.where(mask, s, NEG)
        m = m_sc[g]
        m_new = jnp.maximum(m, jnp.max(s, axis=-1, keepdims=True))
        m_sc[g] = m_new
        return jnp.exp2(s - m_new).astype(BF16), jnp.exp2(m - m_new)

    def q_of(g):
        return q_ref[0, :, g * HEAD_DIM:(g + 1) * HEAD_DIM]

    def update_all(kt_of, v_of, mask=None):
        s_next = _dot(q_of(0), kt_of(0))
        for g in range(heads):
            s = s_next
            if g + 1 < heads:
                s_next = _dot(q_of(g + 1), kt_of(g + 1))
            p, alpha = probs(g, s, mask)
            acc_sc[g] = alpha * acc_sc[g] + _dot(p, v_of(g))

    def step(k0, size):
        update_all(lambda g: kt_ref[0, g // hpk, :, pl.ds(k0, size)],
                   lambda g: v_ref[0, g // hpv, pl.ds(k0, size), :])

    if mode == "window":
        span = tq + 2 * WINDOW
        start = pl.multiple_of(jnp.clip(qi * tq - WINDOW, 0, n_tok - span), 128)
        qpos = qi * tq + lax.broadcasted_iota(jnp.int32, (tq, n_ctx + span), 0)
        col = lax.broadcasted_iota(jnp.int32, (tq, n_ctx + span), 1)
        kpos = start + col - n_ctx
        mask = (col < n_ctx) | (is_lat & (kpos >= n_ctx) & (jnp.abs(kpos - qpos) <= WINDOW))

        def v_of(g):
            gv = g // hpv
            return jnp.concatenate([v_ref[0, gv, 0:n_ctx, :], v_ref[0, gv, pl.ds(start, span), :]], axis=0)

        def kt_of(g):
            gk = g // hpk
            return jnp.concatenate([kt_ref[0, gk, :, 0:n_ctx], kt_ref[0, gk, :, pl.ds(start, span)]], axis=1)

        update_all(kt_of, v_of, mask)
    else:
        @pl.when(jnp.logical_not(is_lat))
        def _():
            step(0, n_ctx)

        def body(ci, carry):
            step(pl.multiple_of(ci * ck, 128), ck)
            return carry

        lax.fori_loop(0, jnp.where(is_lat, n_tok // ck, 0), body, 0)

    def result(g):
        acc = acc_sc[g]
        return acc[:, 0:128] / acc[:, dv:dv + 1]

    if mode == "diff":
        lam = sc_ref[heads]
        post = sc_ref[heads + 1]
        for j in range(heads // 2):
            y = result(2 * j) - lam * result(2 * j + 1)
            o_ref[0, :, j * dv:(j + 1) * dv] = (_rms_rows(y, gsub_ref[...]) * post).astype(o_ref.dtype)
    else:
        lane = lax.broadcasted_iota(jnp.int32, (tq, 128), 1)
        for j in range(heads // 2):
            pair = jnp.where(lane < dv, result(2 * j), pltpu.roll(result(2 * j + 1), dv, 1))
            o_ref[0, :, j * 128:(j + 1) * 128] = pair.astype(o_ref.dtype)


def _attention(scalars, gsub, q, kt, vx, *, heads, hpv, hpk, mode, n_ctx, units, q_unit0, k_unit0,
               v_unit0):
    b, n, _ = q.shape
    hd = HEAD_DIM
    dvx = vx.shape[-1]
    dv = gsub.shape[-1]
    assert dv < dvx
    tq = ATTN_TQ
    gk = heads // hpk
    vh = heads // hpv
    out_w = heads * hd if mode != "diff" else (heads // 2) * dv
    ck = max(c for c in range(128, MAX_KEY_CHUNK[mode] + 1, 128) if n % c == 0)
    kern = functools.partial(_attn_kernel, heads=heads, hpv=hpv, hpk=hpk, mode=mode,
                             n_ctx=n_ctx, n_tok=n, tq=tq, ck=ck, dv=dv)
    return pl.pallas_call(
        kern,
        grid=(b, units, n // tq),
        in_specs=[
            pl.BlockSpec(memory_space=pltpu.SMEM),
            pl.BlockSpec((1, dv), lambda bi, u, t: (0, 0)),
            pl.BlockSpec((1, tq, heads * hd), lambda bi, u, t: (bi, t, q_unit0 + u)),
            pl.BlockSpec((1, gk, hd, n), lambda bi, u, t: (bi, k_unit0 + u, 0, 0)),
            pl.BlockSpec((1, vh, n, dvx), lambda bi, u, t: (bi, v_unit0 + u, 0, 0)),
        ],
        out_specs=pl.BlockSpec((1, tq, out_w), lambda bi, u, t: (bi, t, u)),
        out_shape=jax.ShapeDtypeStruct((b, n, units * out_w), BF16),
        scratch_shapes=[
            pltpu.VMEM((heads, tq, 1), F32),
            pltpu.VMEM((heads, tq, dvx), F32),
        ],
        compiler_params=_cparams(("parallel", "parallel", "arbitrary")),
        name="attn_" + mode,
    )(scalars, gsub, q, kt, vx)


def _merge_kernel(x_ref, mod_ref, g1_ref, ya_ref, yb_ref, yc_ref, wg_ref, bg_ref, wa_ref, wb_ref,
                  wc_ref, wo_ref, o_ref, *, d):
    x = x_ref[0]
    mod = mod_ref[0]
    a = (_rms_rows(x, g1_ref[...]) * (1.0 + mod[:, d:2 * d]) + mod[:, 0:d]).astype(BF16)
    gate = _sigmoid(_dot(a, wg_ref[...]) + bg_ref[...])
    m = (gate[:, 0:d] * _dot(ya_ref[0], wa_ref[...])
         + gate[:, d:2 * d] * _dot(yb_ref[0], wb_ref[...])
         + gate[:, 2 * d:3 * d] * _dot(yc_ref[0], wc_ref[...]))
    mix = _dot(m.astype(BF16), wo_ref[...])
    o_ref[0] = x + mod[:, 2 * d:3 * d] * mix


def _merge(x, mod, g1, ya, yb, yc, wg, bg, wa, wb, wc, wo, n_ctx_tiles):
    b, n, d = x.shape
    tn = TOK_TILE
    nb = b
    yw = ya.shape[-1]

    def mod_idx(bi, t):
        return (jnp.where(t < n_ctx_tiles, nb, bi), 0, 0)

    tok = lambda bi, t: (bi, t, 0)
    const = lambda bi, t: (0, 0)
    return pl.pallas_call(
        functools.partial(_merge_kernel, d=d),
        grid=(b, n // tn),
        in_specs=[
            pl.BlockSpec((1, tn, d), tok),
            pl.BlockSpec((1, 1, N_MOD * d), mod_idx),
            pl.BlockSpec((1, d), const),
            pl.BlockSpec((1, tn, yw), tok),
            pl.BlockSpec((1, tn, yw), tok),
            pl.BlockSpec((1, tn, yw), tok),
            pl.BlockSpec((d, 3 * d), const),
            pl.BlockSpec((1, 3 * d), const),
            pl.BlockSpec((yw, d), const),
            pl.BlockSpec((yw, d), const),
            pl.BlockSpec((yw, d), const),
            pl.BlockSpec((d, d), const),
        ],
        out_specs=pl.BlockSpec((1, tn, d), tok),
        out_shape=jax.ShapeDtypeStruct((b, n, d), F32),
        compiler_params=_cparams(("parallel", "parallel")),
        name="merge",
    )(x, mod, g1, ya, yb, yc, wg, bg, wa, wb, wc, wo)


def _first_index(hit, idx, big):
    return jnp.min(jnp.where(hit, idx, big), axis=0, keepdims=True)


def _route_kernel(x_ref, mod_ref, g2_ref, wrt_ref, eb_ref, ws1_ref, ws3_ref, ws2_ref,
                  f_ref, idx_ref, w_ref, xs_ref, cnt_ref, *, d):
    x = x_ref[0]
    mod = mod_ref[0]
    f = _rms_rows(x, g2_ref[...]) * (1.0 + mod[:, 4 * d:5 * d]) + mod[:, 3 * d:4 * d]
    tn = f.shape[0]
    for s in range(d // 128):
        f_ref[0, pl.ds(s, tn, stride=d // 128), :] = f[:, s * 128:(s + 1) * 128]
    logits = lax.dot_general(wrt_ref[...], f, (((1,), (1,)), ((), ())),
                             preferred_element_type=F32, precision=lax.Precision.HIGHEST)
    scores = _sigmoid(logits)
    choice = scores + eb_ref[...]
    eidx = lax.broadcasted_iota(jnp.int32, (N_EXPERTS, tn), 0)
    lidx = lax.broadcasted_iota(jnp.int32, (GROUP_SIZE, tn), 0)
    gscore = []
    for g in range(N_GROUPS):
        cg = choice[g * GROUP_SIZE:(g + 1) * GROUP_SIZE, :]
        m1 = jnp.max(cg, axis=0, keepdims=True)
        first = _first_index(cg == m1, lidx, GROUP_SIZE)
        m2 = jnp.max(jnp.where(lidx == first, NEG, cg), axis=0, keepdims=True)
        gscore.append(m1 + m2)
    gs = jnp.concatenate(gscore, axis=0)
    gidx = lax.broadcasted_iota(jnp.int32, (N_GROUPS, tn), 0)
    gsel = jnp.zeros((N_GROUPS, tn), jnp.bool_)
    for _ in range(TOPK_GROUPS):
        gm = jnp.max(gs, axis=0, keepdims=True)
        first = _first_index(gs == gm, gidx, N_GROUPS)
        hit = gidx == first
        gsel = gsel | hit
        gs = jnp.where(hit, NEG, gs)
    gself = gsel.astype(F32)
    emask = jnp.concatenate(
        [jnp.broadcast_to(gself[g:g + 1, :], (GROUP_SIZE, tn)) for g in range(N_GROUPS)], axis=0)
    cur = jnp.where(emask > 0.5, choice, NEG)
    ids, ws = [], []
    chosen = jnp.zeros((N_EXPERTS, tn), F32)
    for _ in range(TOP_K):
        m = jnp.max(cur, axis=0, keepdims=True)
        first = _first_index(cur == m, eidx, N_EXPERTS)
        hit = eidx == first
        ids.append(first)
        ws.append(jnp.sum(jnp.where(hit, scores, 0.0), axis=0, keepdims=True))
        cur = jnp.where(hit, NEG, cur)
        chosen = chosen + hit.astype(F32)
    wsel = jnp.concatenate(ws, axis=0)
    idx_ref[0] = jnp.concatenate(ids, axis=0)
    cnt_ref[0, 0] = jnp.sum(chosen, axis=1, keepdims=True).astype(jnp.int32)
    w_ref[0] = wsel / jnp.sum(wsel, axis=0, keepdims=True) * ROUTE_SCALE
    fb = f.astype(BF16)
    h = _silu(_dot(fb, ws1_ref[...])) * _dot(fb, ws3_ref[...])
    xs_ref[0] = x + mod[:, 5 * d:6 * d] * _dot(h.astype(BF16), ws2_ref[...])


def _route(x, mod, g2, wrt, eb, ws1, ws3, ws2, n_ctx_tiles):
    b, n, d = x.shape
    tn = TOK_TILE
    nb = b
    ds = ws1.shape[-1]

    def mod_idx(bi, t):
        return (jnp.where(t < n_ctx_tiles, nb, bi), 0, 0)

    tok = lambda bi, t: (bi, t, 0)
    lane_tok = lambda bi, t: (bi, 0, t)
    const = lambda bi, t: (0, 0)
    return pl.pallas_call(
        functools.partial(_route_kernel, d=d),
        grid=(b, n // tn),
        in_specs=[
            pl.BlockSpec((1, tn, d), tok),
            pl.BlockSpec((1, 1, N_MOD * d), mod_idx),
            pl.BlockSpec((1, d), const),
            pl.BlockSpec((N_EXPERTS, d), const),
            pl.BlockSpec((N_EXPERTS, 1), const),
            pl.BlockSpec((d, ds), const),
            pl.BlockSpec((d, ds), const),
            pl.BlockSpec((ds, d), const),
        ],
        out_specs=[
            pl.BlockSpec((1, tn * (d // 128), 128), tok),
            pl.BlockSpec((1, TOP_K, tn), lane_tok),
            pl.BlockSpec((1, TOP_K, tn), lane_tok),
            pl.BlockSpec((1, tn, d), tok),
            pl.BlockSpec((1, 1, N_EXPERTS, 1), lambda bi, t: (bi, t, 0, 0)),
        ],
        out_shape=[
            jax.ShapeDtypeStruct((b, n * (d // 128), 128), F32),
            jax.ShapeDtypeStruct((b, TOP_K, n), jnp.int32),
            jax.ShapeDtypeStruct((b, TOP_K, n), F32),
            jax.ShapeDtypeStruct((b, n, d), F32),
            jax.ShapeDtypeStruct((b, n // tn, N_EXPERTS, 1), jnp.int32),
        ],
        compiler_params=_cparams(("parallel", "parallel")),
        name="route_shared",
    )(x, mod, g2, wrt, eb, ws1, ws3, ws2)


def _expert_kernel(iexp_ref, iblk_ref, iflag_ref, nitems_ref, tok_ref, tok_next_ref, tok_ahead_ref,
                   dst_prev_ref, dst_ref, erow_ref, f_hbm, w1_ref, w3_ref, w2_ref, y_hbm, xbuf, ybuf,
                   wb1, wb3, wb2, gsem, ssem):
    i = pl.program_id(0)
    nitems = nitems_ref[0]
    blk = iblk_ref[i]
    expert = iexp_ref[i]
    first = (iflag_ref[i] & 1) != 0
    last = (iflag_ref[i] & 2) != 0
    ring = xbuf.shape[0]
    slot = blk % ring
    slot_prev = (blk + ring - 1) % ring
    slot_prev2 = (blk + ring - 2) % ring
    slot_ahead = (blk + 2) % ring
    rows = EXPERT_BLOCK
    parts = xbuf.shape[1] // rows
    yparts = ybuf.shape[1] // rows
    n_blocks = y_hbm.shape[0] // (rows * yparts)

    def gather(idx_ref, s):
        for j in range(rows):
            src = pl.multiple_of(idx_ref[0, 0, j], parts)
            pltpu.make_async_copy(f_hbm.at[pl.ds(src, parts)], xbuf.at[s, pl.ds(j * parts, parts)],
                                  gsem.at[s]).start(priority=j % 2)

    def gather_wait(s):
        pltpu.make_async_copy(f_hbm.at[pl.ds(0, rows * parts)], xbuf.at[s], gsem.at[s]).wait()

    def scatter(idx_ref, s):
        for j in range(rows):
            dst = pl.multiple_of(idx_ref[0, 0, j], yparts)
            pltpu.make_async_copy(ybuf.at[s, pl.ds(j * yparts, yparts)], y_hbm.at[pl.ds(dst, yparts)],
                                  ssem.at[s]).start(priority=j % 2)

    def scatter_wait(s):
        pltpu.make_async_copy(ybuf.at[s], y_hbm.at[pl.ds(0, rows * yparts)], ssem.at[s]).wait()

    def expert_rows(s, after_up=None, after_down=None):
        xb = jnp.concatenate([xbuf[s, pl.ds(p, rows, stride=parts), :] for p in range(parts)],
                             axis=1).astype(BF16)
        h1 = _dot(xb, wb1[...])
        h3 = _dot(xb, wb3[...])
        if after_up is not None:
            after_up()
        y = _dot((_silu(h1) * h3).astype(BF16), wb2[...])
        if after_down is not None:
            after_down()
        return y

    def bf16_bits(v):
        return lax.bitcast_convert_type(v.astype(BF16).astype(F32), jnp.uint32)

    def store_rows(s, y, own_rows=None):
        for p in range(yparts):
            lo = bf16_bits(y[:, p * 128:(p + 1) * 128]) >> 16
            hi = bf16_bits(y[:, (yparts + p) * 128:(yparts + p + 1) * 128]) & jnp.uint32(0xFFFF0000)
            part = hi | lo
            if own_rows is not None:
                part = jnp.where(own_rows, part, ybuf[s, pl.ds(p, rows, stride=yparts), :])
            ybuf[s, pl.ds(p, rows, stride=yparts), :] = part

    valid = i < nitems

    @pl.when(valid & ((iflag_ref[i] & 4) != 0))
    def _():
        wb1[...] = w1_ref[0, 0].astype(BF16)
        wb3[...] = w3_ref[0, 0].astype(BF16)
        wb2[...] = w2_ref[0, 0].astype(BF16)

    fast = valid & first & last & (blk >= 3) & (blk <= n_blocks - 3)

    @pl.when(fast)
    def _():
        gather_wait(slot)
        scatter_wait(slot)
        y = expert_rows(slot, after_up=lambda: gather(tok_ahead_ref, slot_ahead),
                        after_down=lambda: scatter(dst_prev_ref, slot_prev))
        store_rows(slot, y)

    @pl.when(valid & jnp.logical_not(fast))
    def _():
        @pl.when(first)
        def _():
            @pl.when(blk == 0)
            def _():
                gather(tok_ref, 0)
                if n_blocks > 1:
                    gather(tok_next_ref, 1)

            @pl.when(blk + 2 < n_blocks)
            def _():
                gather(tok_ahead_ref, slot_ahead)

            @pl.when(blk >= 1)
            def _():
                scatter(dst_prev_ref, slot_prev)

            gather_wait(slot)

            @pl.when(blk >= 3)
            def _():
                scatter_wait(slot)

        y = expert_rows(slot)

        @pl.when(first)
        def _():
            store_rows(slot, y)

        @pl.when(jnp.logical_not(first))
        def _():
            ours = (erow_ref[0] == expert) & (lax.broadcasted_iota(jnp.int32, (rows, rows), 0)
                                             == lax.broadcasted_iota(jnp.int32, (rows, rows), 1))
            store_rows(slot, y, own_rows=jnp.sum(ours.astype(F32), axis=1, keepdims=True) > 0.5)

        @pl.when(last & (blk == n_blocks - 1))
        def _():
            scatter(dst_ref, slot)

            @pl.when(blk >= 2)
            def _():
                scatter_wait(slot_prev2)

            @pl.when(blk >= 1)
            def _():
                scatter_wait(slot_prev)

            scatter_wait(slot)


def _experts(iexp, iblk, iflag, nitems, row_tok, row_dst, erow, f_rows, w1, w3, w2, layer):
    n_items = iexp.shape[0]
    n_blocks = row_tok.shape[0]
    d, de = w1.shape[2], w1.shape[3]
    parts = d // 128
    yparts = parts // 2
    rows = EXPERT_BLOCK
    last = n_blocks - 1
    cur = lambda i, ie, ib, fl, nt: (ib[i], 0, 0)
    nxt = lambda i, ie, ib, fl, nt: (jnp.minimum(ib[i] + 1, last), 0, 0)
    ahd = lambda i, ie, ib, fl, nt: (jnp.minimum(ib[i] + 2, last), 0, 0)
    prv = lambda i, ie, ib, fl, nt: (jnp.maximum(ib[i] - 1, 0), 0, 0)
    wsel = lambda i, ie, ib, fl, nt: (layer, ie[i], 0, 0)
    grid_spec = pltpu.PrefetchScalarGridSpec(
        num_scalar_prefetch=4,
        grid=(n_items,),
        in_specs=[
            pl.BlockSpec((1, 1, rows), cur, memory_space=pltpu.SMEM),
            pl.BlockSpec((1, 1, rows), nxt, memory_space=pltpu.SMEM),
            pl.BlockSpec((1, 1, rows), ahd, memory_space=pltpu.SMEM),
            pl.BlockSpec((1, 1, rows), prv, memory_space=pltpu.SMEM),
            pl.BlockSpec((1, 1, rows), cur, memory_space=pltpu.SMEM),
            pl.BlockSpec((1, 1, rows), cur),
            pl.BlockSpec(memory_space=pl.ANY),
            pl.BlockSpec((1, 1, d, de), wsel),
            pl.BlockSpec((1, 1, d, de), wsel),
            pl.BlockSpec((1, 1, de, d), wsel),
        ],
        out_specs=pl.BlockSpec(memory_space=pl.ANY),
        scratch_shapes=[
            pltpu.VMEM((EXPERT_RING, rows * parts, 128), F32),
            pltpu.VMEM((EXPERT_RING, rows * yparts, 128), jnp.uint32),
            pltpu.VMEM((d, de), BF16),
            pltpu.VMEM((d, de), BF16),
            pltpu.VMEM((de, d), BF16),
            pltpu.SemaphoreType.DMA((EXPERT_RING,)),
            pltpu.SemaphoreType.DMA((EXPERT_RING,)),
        ],
    )
    return pl.pallas_call(
        _expert_kernel,
        grid_spec=grid_spec,
        out_shape=jax.ShapeDtypeStruct((n_blocks * rows * yparts, 128), jnp.uint32),
        compiler_params=_cparams(("arbitrary",)),
        name="experts",
    )(iexp, iblk, iflag, nitems, row_tok, row_tok, row_tok, row_dst, row_dst, erow, f_rows, w1, w3, w2)


ASSIGN_BITS = 20


def _routing_tables(idx, counts, n_tokens, parts):
    n_assign = n_tokens * TOP_K
    blk = EXPERT_BLOCK
    assert n_assign % blk == 0 and n_assign <= (1 << ASSIGN_BITS)
    n_blocks = n_assign // blk
    n_items = n_blocks + N_EXPERTS
    flat_e = idx.reshape(-1).astype(jnp.int32)
    key = jnp.sort((flat_e << ASSIGN_BITS) | jnp.arange(n_assign, dtype=jnp.int32))
    e_sorted = key >> ASSIGN_BITS
    order = key & ((1 << ASSIGN_BITS) - 1)
    tok = order // TOP_K
    row_dst = ((order % TOP_K) * n_tokens + tok) * (parts // 2)
    tok = tok * parts

    experts = jnp.arange(N_EXPERTS, dtype=jnp.int32)
    ends = jnp.cumsum(counts)
    starts = ends - counts
    first_blk = starts // blk
    n_be = jnp.where(ends > starts, (ends - 1) // blk - first_blk + 1, 0)
    item_end = jnp.cumsum(n_be)
    item_off = item_end - n_be
    total = item_end[-1]
    i = jnp.arange(n_items, dtype=jnp.int32)
    iexp = jnp.minimum(jnp.sum((item_end[None, :] <= i[:, None]).astype(jnp.int32), axis=1), N_EXPERTS - 1)
    pick = (iexp[:, None] == experts[None, :]).astype(jnp.int32)
    base = jnp.sum(pick * (first_blk - item_off)[None, :], axis=1)
    iblk = jnp.where(i < total, base + i, n_blocks - 1).astype(jnp.int32)
    prev_blk = jnp.concatenate([jnp.full((1,), -1, jnp.int32), iblk[:-1]])
    next_blk = jnp.concatenate([iblk[1:], jnp.full((1,), -1, jnp.int32)])
    is_first = iblk != prev_blk
    is_last = (iblk != next_blk) | (i == total - 1)
    new_expert = iexp != jnp.concatenate([jnp.full((1,), -1, jnp.int32), iexp[:-1]])
    iflag = is_first.astype(jnp.int32) + 2 * is_last.astype(jnp.int32) + 4 * new_expert.astype(jnp.int32)
    return (iexp, iblk, iflag, total.astype(jnp.int32).reshape(1), tok.reshape(n_blocks, 1, blk),
            row_dst.reshape(n_blocks, 1, blk), e_sorted.reshape(n_blocks, 1, blk))


def _combine_kernel(xs_ref, mod_ref, w_ref, *refs, d):
    y_refs, o_ref = refs[:TOP_K], refs[TOP_K]
    w = w_ref[0]
    tn = w.shape[0]
    yparts = d // 256
    wk = [jnp.broadcast_to(w[:, k:k + 1], (tn, 128)) for k in range(TOP_K)]
    for p in range(yparts):
        lo = jnp.zeros((tn, 128), F32)
        hi = jnp.zeros((tn, 128), F32)
        for k in range(TOP_K):
            word = y_refs[k][pl.ds(p, tn, stride=yparts), :]
            lo = lo + lax.bitcast_convert_type(word << 16, F32) * wk[k]
            hi = hi + lax.bitcast_convert_type(word & jnp.uint32(0xFFFF0000), F32) * wk[k]
        for q, tot in ((p, lo), (yparts + p, hi)):
            sl = slice(q * 128, (q + 1) * 128)
            o_ref[0, :, sl] = xs_ref[0, :, sl] + mod_ref[0][:, 5 * d + q * 128:5 * d + (q + 1) * 128] * tot


def _combine(xs, mod, w, y, n_ctx_tiles):
    b, n, d = xs.shape
    tn = TOK_TILE
    nb = b
    tiles = n // tn
    per_slot = b * tiles

    def mod_idx(bi, t):
        return (jnp.where(t < n_ctx_tiles, nb, bi), 0, 0)

    tok = lambda bi, t: (bi, t, 0)
    y_specs = [pl.BlockSpec((tn * (d // 256), 128),
                            functools.partial(lambda bi, t, k: (k * per_slot + bi * tiles + t, 0), k=k))
               for k in range(TOP_K)]
    return pl.pallas_call(
        functools.partial(_combine_kernel, d=d),
        grid=(b, tiles),
        in_specs=[pl.BlockSpec((1, tn, d), tok), pl.BlockSpec((1, 1, N_MOD * d), mod_idx),
                  pl.BlockSpec((1, tn, TOP_K), tok)] + y_specs,
        out_specs=pl.BlockSpec((1, tn, d), tok),
        out_shape=jax.ShapeDtypeStruct((b, n, d), F32),
        compiler_params=_cparams(("parallel", "parallel")),
        name="combine",
    )(xs, mod, w, *([y] * TOP_K))


def _final_kernel(x_ref, g_ref, o_ref):
    o_ref[0] = _rms_rows(x_ref[0], g_ref[...])


def _final_norm(x, g, n_ctx_tiles):
    b, n, d = x.shape
    tn = TOK_TILE
    n_lat_tiles = n // tn - n_ctx_tiles
    return pl.pallas_call(
        _final_kernel,
        grid=(b, n_lat_tiles),
        in_specs=[pl.BlockSpec((1, tn, d), lambda bi, t: (bi, t + n_ctx_tiles, 0)),
                  pl.BlockSpec((1, d), lambda bi, t: (0, 0))],
        out_specs=pl.BlockSpec((1, tn, d), lambda bi, t: (bi, t, 0)),
        out_shape=jax.ShapeDtypeStruct((b, n_lat_tiles * tn, d), F32),
        compiler_params=_cparams(("parallel", "parallel")),
        name="final_norm",
    )(x, g)


def _regroup_qkv_columns(w):
    a_q, a_kv = A_HEADS * HEAD_DIM, A_KV * HEAD_DIM
    b_q, b_kv = B_HEADS * HEAD_DIM, B_KV * HEAD_DIM
    c_qk, c_v = 2 * C_HEADS * HEAD_DIM, C_HEADS * 2 * HEAD_DIM
    off = np.concatenate([[0], np.cumsum((a_q, a_kv, a_kv, b_q, b_kv, b_kv, c_qk, c_qk, c_v))])
    seg = lambda i: w[:, int(off[i]):int(off[i + 1])]
    return jnp.concatenate([seg(0), seg(3), seg(6), seg(1), seg(4), seg(7), seg(2), seg(5), seg(8)], axis=1)


def _rotary_partner(row):
    blocks = row.reshape(-1, 2, ROPE_FREQS)
    return blocks[:, ::-1, :].reshape(-1)


def _rope_tables(n_ctx, n_lat):
    t = jnp.arange(n_lat, dtype=jnp.int32)
    row_pos = (t // GRID_W).astype(F32)
    col_pos = (t % GRID_W).astype(F32)
    inv_freq = jnp.power(ROPE_THETA, -jnp.arange(ROPE_FREQS, dtype=F32) / ROPE_FREQS)
    ang_r = row_pos[:, None] * inv_freq
    ang_c = col_pos[:, None] * inv_freq
    cos64 = jnp.concatenate([jnp.cos(ang_r), jnp.cos(ang_r), jnp.cos(ang_c), jnp.cos(ang_c)], axis=1)
    sin64 = jnp.concatenate([-jnp.sin(ang_r), jnp.sin(ang_r), -jnp.sin(ang_c), jnp.sin(ang_c)], axis=1)
    cos64 = jnp.concatenate([jnp.ones((n_ctx, HEAD_DIM), F32), cos64], axis=0)
    sin64 = jnp.concatenate([jnp.zeros((n_ctx, HEAD_DIM), F32), sin64], axis=0)
    return jnp.tile(cos64, (1, 2)), jnp.tile(sin64, (1, 2))


def kernel(x, c, ctx, c_ctx, w_mod, b_mod, g_norm1, w_qkv, g_qnorm_a, g_knorm_a, sink_b, lam_q1, lam_k1, lam_q2, lam_k2, g_subln_c, w_br_a, w_br_b, w_br_c, w_gate, b_gate, w_out, g_norm2, w_router, e_bias, w1, w3, w2, ws1, ws3, ws2, g_final):
    bsz, n_lat, d = x.shape
    n_ctx = ctx.shape[1]
    depth = w_mod.shape[0]
    n_tok = n_ctx + n_lat
    assert n_ctx % TOK_TILE == 0 and n_lat % KEY_CHUNK == 0 and n_ctx % 128 == 0
    n_ctx_tiles = n_ctx // TOK_TILE
    n_all = bsz * n_tok

    rows = -(-(bsz + 1) // 8) * 8
    cvec = jnp.concatenate([c, c_ctx[None, :], jnp.zeros((rows - bsz - 1, d), F32)], axis=0)
    mod_all = _modulation(cvec, w_mod, b_mod)

    cos_t, sin_t = _rope_tables(n_ctx, n_lat)
    head_of = np.arange(R_A) // HEAD_DIM
    ones_blk = jnp.asarray((head_of[:, None] == head_of[None, :]).astype(np.float32), BF16)
    scale = HEAD_DIM ** -0.5 * LOG2E
    unit = jnp.ones((HEAD_DIM,), F32)

    xs = jnp.concatenate([ctx, x], axis=1)
    for l in range(depth):
        lam_init = 0.8 - 0.6 * math.exp(-0.3 * l)
        lam = (jnp.exp(jnp.dot(lam_q1[l], lam_k1[l])) - jnp.exp(jnp.dot(lam_q2[l], lam_k2[l]))).astype(F32) + lam_init
        mod = mod_all[l].reshape(rows, 1, N_MOD * d)
        g1 = g_norm1[l].reshape(1, d)

        wqkv = _regroup_qkv_columns(w_qkv[l]).astype(BF16)
        grow = jnp.concatenate([jnp.tile(g_qnorm_a[l] * scale, A_HEADS), jnp.tile(unit * scale, B_HEADS),
                                jnp.tile(unit * scale, 2 * C_HEADS), jnp.tile(g_knorm_a[l], A_KV),
                                jnp.tile(unit, B_KV), jnp.tile(unit, 2 * C_HEADS)])
        gsrow = _rotary_partner(grow)
        q_all, kt_all, vx_ab, vx_c = _pre_attention(xs, mod, g1, wqkv, ones_blk, grow.reshape(1, R_W),
                                                    gsrow.reshape(1, R_W), cos_t, sin_t, n_ctx_tiles)

        scal = jnp.concatenate([sink_b[l].astype(F32) * LOG2E, lam.reshape(1), jnp.full((1,), 1.0 - lam_init, F32)])
        scal_c = jnp.concatenate([jnp.zeros((DIFF_HEADS_PER_STEP,), F32), lam.reshape(1),
                                  jnp.full((1,), 1.0 - lam_init, F32)])
        gsub = g_subln_c[l].reshape(1, 2 * HEAD_DIM)
        g64 = jnp.ones((1, HEAD_DIM), F32)
        ga, gb = A_HEADS // A_KV, B_HEADS // B_KV
        ya = _attention(scal, g64, q_all, kt_all, vx_ab, heads=ga, hpv=ga, hpk=ga, mode="global",
                        n_ctx=n_ctx, units=A_KV, q_unit0=0, k_unit0=0, v_unit0=0)
        yb = _attention(scal, g64, q_all, kt_all, vx_ab, heads=B_HEADS, hpv=gb, hpk=gb, mode="window",
                        n_ctx=n_ctx, units=1, q_unit0=QA_W // QB_W, k_unit0=A_KV // B_KV, v_unit0=A_KV // B_KV)
        yc = _attention(scal_c, gsub, q_all, kt_all, vx_c, heads=DIFF_HEADS_PER_STEP, hpv=2, hpk=1,
                        mode="diff", n_ctx=n_ctx, units=2 * C_HEADS // DIFF_HEADS_PER_STEP,
                        q_unit0=(QA_W + QB_W) // 256, k_unit0=(A_KV + B_KV) // DIFF_HEADS_PER_STEP, v_unit0=0)

        x1 = _merge(xs, mod, g1, ya, yb, yc,
                    w_gate[l].astype(BF16), b_gate[l].reshape(1, -1), w_br_a[l].astype(BF16),
                    w_br_b[l].astype(BF16), w_br_c[l].astype(BF16), w_out[l].astype(BF16), n_ctx_tiles)

        f, idx_t, w_t, x_sh, tile_counts = _route(x1, mod, g_norm2[l].reshape(1, d), w_router[l].T,
                                     e_bias[l].reshape(N_EXPERTS, 1), ws1[l].astype(BF16),
                                     ws3[l].astype(BF16), ws2[l].astype(BF16), n_ctx_tiles)

        idx = idx_t.transpose(0, 2, 1).reshape(n_all, TOP_K)
        counts = jnp.sum(tile_counts, axis=(0, 1)).reshape(N_EXPERTS)
        iexp, iblk, iflag, nitems, row_tok, row_dst, erow = _routing_tables(idx, counts, n_all, d // 128)
        y = _experts(iexp, iblk, iflag, nitems, row_tok, row_dst, erow, f.reshape(n_all * (d // 128), 128),
                     w1, w3, w2, l)
        xs = _combine(x_sh, mod, w_t.transpose(0, 2, 1), y, n_ctx_tiles)

    return _final_norm(xs, g_final.reshape(1, d), n_ctx_tiles)
```

```python
import functools
import math

import numpy as np
import jax
import jax.numpy as jnp
from jax import lax
from jax.experimental import pallas as pl
from jax.experimental.pallas import tpu as pltpu

F32 = jnp.float32
BF16 = jnp.bfloat16

HEAD_DIM = 64
ROPE_FREQS = HEAD_DIM // 4
ROPE_THETA = 10000.0
GRID_W = 64
WINDOW = 128
A_HEADS, A_KV = 8, 2
B_HEADS, B_KV = 8, 2
C_HEADS = 4
N_EXPERTS = 128
TOP_K = 8
N_GROUPS = 8
TOPK_GROUPS = 4
GROUP_SIZE = N_EXPERTS // N_GROUPS
ROUTE_SCALE = 2.5
EXPERT_BLOCK = 128
EXPERT_RING = 3
N_MOD = 6
EPS = 1e-6
NEG = -1e30

QA_W, QB_W, QC_W = A_HEADS * HEAD_DIM, B_HEADS * HEAD_DIM, 2 * C_HEADS * HEAD_DIM
KA_W, KB_W, KC_W = A_KV * HEAD_DIM, B_KV * HEAD_DIM, 2 * C_HEADS * HEAD_DIM
Q_W = QA_W + QB_W + QC_W
K_W = KA_W + KB_W + KC_W
R_W = Q_W + K_W
R_A = QA_W + KA_W
V_W = A_KV * HEAD_DIM + B_KV * HEAD_DIM + C_HEADS * 2 * HEAD_DIM
K_HEADS = K_W // HEAD_DIM

TOK_TILE = 256
ATTN_TQ = 256
DIFF_HEADS_PER_STEP = 4
KEY_CHUNK = 512
MAX_KEY_CHUNK = {"global": 8448, "diff": 8448, "window": 128}
LOG2E = math.log2(math.e)
VMEM_LIMIT = 56 * 1024 * 1024


def _cparams(sem, **kw):
    return pltpu.CompilerParams(dimension_semantics=sem, vmem_limit_bytes=VMEM_LIMIT, **kw)


def _dot(a, b):
    return jnp.dot(a, b, preferred_element_type=F32)


def _sigmoid(x):
    return 1.0 / (1.0 + jnp.exp(-x))


def _silu(x):
    return x * _sigmoid(x)


def _rms_rows(x, g):
    return x * lax.rsqrt(jnp.mean(x * x, axis=-1, keepdims=True) + EPS) * g


def _mod_kernel(c_ref, w_ref, b_ref, o_ref):
    cs = _silu(c_ref[...])
    o_ref[0] = jnp.dot(cs, w_ref[0], preferred_element_type=F32,
                       precision=lax.Precision.HIGHEST) + b_ref[0]


def _modulation(cvec, w_mod, b_mod):
    depth, d, n = w_mod.shape
    rows = cvec.shape[0]
    bn = 1536
    return pl.pallas_call(
        _mod_kernel,
        grid=(depth, n // bn),
        in_specs=[
            pl.BlockSpec((rows, d), lambda l, j: (0, 0)),
            pl.BlockSpec((1, d, bn), lambda l, j: (l, 0, j)),
            pl.BlockSpec((1, 1, bn), lambda l, j: (l, 0, j)),
        ],
        out_specs=pl.BlockSpec((1, rows, bn), lambda l, j: (l, 0, j)),
        out_shape=jax.ShapeDtypeStruct((depth, rows, n), F32),
        compiler_params=_cparams(("parallel", "parallel")),
        name="modulation",
    )(cvec, w_mod, b_mod.reshape(depth, 1, n))


def _pre_kernel(x_ref, mod_ref, g1_ref, wqkv_ref, ones_ref, grow_ref, gsrow_ref,
                cos_ref, sin_ref, q_out, kt_out, vab_out, vc_out, *, d):
    x = x_ref[0]
    mod = mod_ref[0]
    tn = x.shape[0]
    a = (_rms_rows(x, g1_ref[...]) * (1.0 + mod[:, d:2 * d]) + mod[:, 0:d]).astype(BF16)
    p = _dot(a, wqkv_ref[...])
    pa = jnp.concatenate([p[:, 0:QA_W], p[:, Q_W:Q_W + KA_W]], axis=1)
    sq = pa * pa
    hi = sq.astype(BF16)
    lo = (sq - hi.astype(F32)).astype(BF16)
    ssq = _dot(hi, ones_ref[...]) + _dot(lo, ones_ref[...])
    rinv = lax.rsqrt(ssq * (1.0 / HEAD_DIM) + EPS)
    cos = cos_ref[...]
    sin = sin_ref[...]
    lane = lax.broadcasted_iota(jnp.int32, (tn, 128), 1)
    low_half = (lane % (2 * ROPE_FREQS)) < ROPE_FREQS
    first_head = lane < HEAD_DIM
    for j in range(R_W // 128):
        sl = slice(j * 128, (j + 1) * 128)
        pj = p[:, sl]
        ps = jnp.where(low_half, pltpu.roll(pj, 128 - ROPE_FREQS, 1), pltpu.roll(pj, ROPE_FREQS, 1))
        o = pj * (grow_ref[:, sl] * cos) + ps * (gsrow_ref[:, sl] * sin)
        if (j + 1) * 128 <= QA_W:
            o = o * rinv[:, sl]
        elif j * 128 == Q_W:
            o = o * rinv[:, QA_W:QA_W + KA_W]
        if j * 128 < Q_W:
            q_out[0, :, sl] = o.astype(BF16)
        else:
            ot = jnp.transpose(o).astype(BF16)
            kh = (j * 128 - Q_W) // HEAD_DIM
            kt_out[0, kh] = ot[0:HEAD_DIM]
            kt_out[0, kh + 1] = ot[HEAD_DIM:2 * HEAD_DIM]
    ones_at_64 = (lane == HEAD_DIM).astype(F32)
    for j in range(2):
        vj = p[:, R_W + j * 128:R_W + (j + 1) * 128]
        vab_out[0, 2 * j] = jnp.where(first_head, vj, ones_at_64).astype(BF16)
        vab_out[0, 2 * j + 1] = jnp.where(first_head, pltpu.roll(vj, HEAD_DIM, 1), ones_at_64).astype(BF16)
    ones_at_0 = (lane == 0).astype(BF16)
    for j in range(C_HEADS):
        vj = p[:, R_W + (2 + j) * 128:R_W + (3 + j) * 128]
        vc_out[0, j] = jnp.concatenate([vj.astype(BF16), ones_at_0], axis=1)


def _pre_attention(x, mod, g1, wqkv, ones_blk, grow, gsrow, cos_t, sin_t, n_ctx_tiles):
    b, n, d = x.shape
    tn = TOK_TILE
    nb = b

    def mod_idx(bi, t):
        return (jnp.where(t < n_ctx_tiles, nb, bi), 0, 0)

    return pl.pallas_call(
        functools.partial(_pre_kernel, d=d),
        grid=(b, n // tn),
        in_specs=[
            pl.BlockSpec((1, tn, d), lambda bi, t: (bi, t, 0)),
            pl.BlockSpec((1, 1, N_MOD * d), mod_idx),
            pl.BlockSpec((1, d), lambda bi, t: (0, 0)),
            pl.BlockSpec((d, R_W + V_W), lambda bi, t: (0, 0)),
            pl.BlockSpec((R_A, R_A), lambda bi, t: (0, 0)),
            pl.BlockSpec((1, R_W), lambda bi, t: (0, 0)),
            pl.BlockSpec((1, R_W), lambda bi, t: (0, 0)),
            pl.BlockSpec((tn, 128), lambda bi, t: (t, 0)),
            pl.BlockSpec((tn, 128), lambda bi, t: (t, 0)),
        ],
        out_specs=[
            pl.BlockSpec((1, tn, Q_W), lambda bi, t: (bi, t, 0)),
            pl.BlockSpec((1, K_HEADS, HEAD_DIM, tn), lambda bi, t: (bi, 0, 0, t)),
            pl.BlockSpec((1, A_KV + B_KV, tn, 128), lambda bi, t: (bi, 0, t, 0)),
            pl.BlockSpec((1, C_HEADS, tn, 256), lambda bi, t: (bi, 0, t, 0)),
        ],
        out_shape=[
            jax.ShapeDtypeStruct((b, n, Q_W), BF16),
            jax.ShapeDtypeStruct((b, K_HEADS, HEAD_DIM, n), BF16),
            jax.ShapeDtypeStruct((b, A_KV + B_KV, n, 128), BF16),
            jax.ShapeDtypeStruct((b, C_HEADS, n, 256), BF16),
        ],
        compiler_params=_cparams(("parallel", "parallel")),
        name="pre_attention",
    )(x, mod, g1, wqkv, ones_blk, grow, gsrow, cos_t, sin_t)


def _attn_kernel(sc_ref, gsub_ref, q_ref, kt_ref, v_ref, o_ref, m_sc, acc_sc, *, heads, hpv,
                 hpk, mode, n_ctx, n_tok, tq, ck, dv):
    u = pl.program_id(1)
    qi = pl.program_id(2)
    is_lat = qi >= n_ctx // tq
    dvx = acc_sc.shape[-1]

    for g in range(heads):
        if mode == "window":
            m_sc[g] = jnp.full((tq, 1), sc_ref[u * heads + g], F32)
            lane = lax.broadcasted_iota(jnp.int32, (tq, dvx), 1)
            acc_sc[g] = jnp.where(lane == dv, 1.0, 0.0).astype(F32)
        else:
            m_sc[g] = jnp.full((tq, 1), NEG, F32)
            acc_sc[g] = jnp.zeros((tq, dvx), F32)

    def probs(g, s, mask):
        if mask is not None:
            s = jnp.where(mask, s, NEG)
        m = m_sc[g]
        m_new = jnp.maximum(m, jnp.max(s, axis=-1, keepdims=True))
        m_sc[g] = m_new
        return jnp.exp2(s - m_new).astype(BF16), jnp.exp2(m - m_new)

    def q_of(g):
        return q_ref[0, :, g * HEAD_DIM:(g + 1) * HEAD_DIM]

    def update_all(kt_of, v_of, mask=None):
        s_next = _dot(q_of(0), kt_of(0))
        for g in range(heads):
            s = s_next
            if g + 1 < heads:
                s_next = _dot(q_of(g + 1), kt_of(g + 1))
            p, alpha = probs(g, s, mask)
            acc_sc[g] = alpha * acc_sc[g] + _dot(p, v_of(g))

    def step(k0, size):
        update_all(lambda g: kt_ref[0, g // hpk, :, pl.ds(k0, size)],
                   lambda g: v_ref[0, g // hpv, pl.ds(k0, size), :])

    if mode == "window":
        span = tq + 2 * WINDOW
        start = pl.multiple_of(jnp.clip(qi * tq - WINDOW, 0, n_tok - span), 128)
        qpos = qi * tq + lax.broadcasted_iota(jnp.int32, (tq, n_ctx + span), 0)
        col = lax.broadcasted_iota(jnp.int32, (tq, n_ctx + span), 1)
        kpos = start + col - n_ctx
        mask = (col < n_ctx) | (is_lat & (kpos >= n_ctx) & (jnp.abs(kpos - qpos) <= WINDOW))

        def v_of(g):
            gv = g // hpv
            return jnp.concatenate([v_ref[0, gv, 0:n_ctx, :], v_ref[0, gv, pl.ds(start, span), :]], axis=0)

        def kt_of(g):
            gk = g // hpk
            return jnp.concatenate([kt_ref[0, gk, :, 0:n_ctx], kt_ref[0, gk, :, pl.ds(start, span)]], axis=1)

        update_all(kt_of, v_of, mask)
    else:
        @pl.when(jnp.logical_not(is_lat))
        def _():
            step(0, n_ctx)

        def body(ci, carry):
            step(pl.multiple_of(ci * ck, 128), ck)
            return carry

        lax.fori_loop(0, jnp.where(is_lat, n_tok // ck, 0), body, 0)

    def result(g):
        acc = acc_sc[g]
        return acc[:, 0:128] / acc[:, dv:dv + 1]

    if mode == "diff":
        lam = sc_ref[heads]
        post = sc_ref[heads + 1]
        for j in range(heads // 2):
            y = result(2 * j) - lam * result(2 * j + 1)
            o_ref[0, :, j * dv:(j + 1) * dv] = (_rms_rows(y, gsub_ref[...]) * post).astype(o_ref.dtype)
    else:
        lane = lax.broadcasted_iota(jnp.int32, (tq, 128), 1)
        for j in range(heads // 2):
            pair = jnp.where(lane < dv, result(2 * j), pltpu.roll(result(2 * j + 1), dv, 1))
            o_ref[0, :, j * 128:(j + 1) * 128] = pair.astype(o_ref.dtype)


def _attention(scalars, gsub, q, kt, vx, *, heads, hpv, hpk, mode, n_ctx, units, q_unit0, k_unit0,
               v_unit0):
    b, n, _ = q.shape
    hd = HEAD_DIM
    dvx = vx.shape[-1]
    dv = gsub.shape[-1]
    assert dv < dvx
    tq = ATTN_TQ
    gk = heads // hpk
    vh = heads // hpv
    out_w = heads * hd if mode != "diff" else (heads // 2) * dv
    ck = max(c for c in range(128, MAX_KEY_CHUNK[mode] + 1, 128) if n % c == 0)
    kern = functools.partial(_attn_kernel, heads=heads, hpv=hpv, hpk=hpk, mode=mode,
                             n_ctx=n_ctx, n_tok=n, tq=tq, ck=ck, dv=dv)
    return pl.pallas_call(
        kern,
        grid=(b, units, n // tq),
        in_specs=[
            pl.BlockSpec(memory_space=pltpu.SMEM),
            pl.BlockSpec((1, dv), lambda bi, u, t: (0, 0)),
            pl.BlockSpec((1, tq, heads * hd), lambda bi, u, t: (bi, t, q_unit0 + u)),
            pl.BlockSpec((1, gk, hd, n), lambda bi, u, t: (bi, k_unit0 + u, 0, 0)),
            pl.BlockSpec((1, vh, n, dvx), lambda bi, u, t: (bi, v_unit0 + u, 0, 0)),
        ],
        out_specs=pl.BlockSpec((1, tq, out_w), lambda bi, u, t: (bi, t, u)),
        out_shape=jax.ShapeDtypeStruct((b, n, units * out_w), BF16),
        scratch_shapes=[
            pltpu.VMEM((heads, tq, 1), F32),
            pltpu.VMEM((heads, tq, dvx), F32),
        ],
        compiler_params=_cparams(("parallel", "parallel", "arbitrary")),
        name="attn_" + mode,
    )(scalars, gsub, q, kt, vx)


def _merge_kernel(x_ref, mod_ref, g1_ref, ya_ref, yb_ref, yc_ref, wg_ref, bg_ref, wa_ref, wb_ref,
                  wc_ref, wo_ref, o_ref, *, d):
    x = x_ref[0]
    mod = mod_ref[0]
    a = (_rms_rows(x, g1_ref[...]) * (1.0 + mod[:, d:2 * d]) + mod[:, 0:d]).astype(BF16)
    gate = _sigmoid(_dot(a, wg_ref[...]) + bg_ref[...])
    m = (gate[:, 0:d] * _dot(ya_ref[0], wa_ref[...])
         + gate[:, d:2 * d] * _dot(yb_ref[0], wb_ref[...])
         + gate[:, 2 * d:3 * d] * _dot(yc_ref[0], wc_ref[...]))
    mix = _dot(m.astype(BF16), wo_ref[...])
    o_ref[0] = x + mod[:, 2 * d:3 * d] * mix


def _merge(x, mod, g1, ya, yb, yc, wg, bg, wa, wb, wc, wo, n_ctx_tiles):
    b, n, d = x.shape
    tn = TOK_TILE
    nb = b
    yw = ya.shape[-1]

    def mod_idx(bi, t):
        return (jnp.where(t < n_ctx_tiles, nb, bi), 0, 0)

    tok = lambda bi, t: (bi, t, 0)
    const = lambda bi, t: (0, 0)
    return pl.pallas_call(
        functools.partial(_merge_kernel, d=d),
        grid=(b, n // tn),
        in_specs=[
            pl.BlockSpec((1, tn, d), tok),
            pl.BlockSpec((1, 1, N_MOD * d), mod_idx),
            pl.BlockSpec((1, d), const),
            pl.BlockSpec((1, tn, yw), tok),
            pl.BlockSpec((1, tn, yw), tok),
            pl.BlockSpec((1, tn, yw), tok),
            pl.BlockSpec((d, 3 * d), const),
            pl.BlockSpec((1, 3 * d), const),
            pl.BlockSpec((yw, d), const),
            pl.BlockSpec((yw, d), const),
            pl.BlockSpec((yw, d), const),
            pl.BlockSpec((d, d), const),
        ],
        out_specs=pl.BlockSpec((1, tn, d), tok),
        out_shape=jax.ShapeDtypeStruct((b, n, d), F32),
        compiler_params=_cparams(("parallel", "parallel")),
        name="merge",
    )(x, mod, g1, ya, yb, yc, wg, bg, wa, wb, wc, wo)


def _first_index(hit, idx, big):
    return jnp.min(jnp.where(hit, idx, big), axis=0, keepdims=True)


def _route_kernel(x_ref, mod_ref, g2_ref, wrt_ref, eb_ref, ws1_ref, ws3_ref, ws2_ref,
                  f_ref, idx_ref, w_ref, xs_ref, cnt_ref, *, d):
    x = x_ref[0]
    mod = mod_ref[0]
    f = _rms_rows(x, g2_ref[...]) * (1.0 + mod[:, 4 * d:5 * d]) + mod[:, 3 * d:4 * d]
    tn = f.shape[0]
    for s in range(d // 128):
        f_ref[0, pl.ds(s, tn, stride=d // 128), :] = f[:, s * 128:(s + 1) * 128]
    logits = lax.dot_general(wrt_ref[...], f, (((1,), (1,)), ((), ())),
                             preferred_element_type=F32, precision=lax.Precision.HIGHEST)
    scores = _sigmoid(logits)
    choice = scores + eb_ref[...]
    eidx = lax.broadcasted_iota(jnp.int32, (N_EXPERTS, tn), 0)
    lidx = lax.broadcasted_iota(jnp.int32, (GROUP_SIZE, tn), 0)
    gscore = []
    for g in range(N_GROUPS):
        cg = choice[g * GROUP_SIZE:(g + 1) * GROUP_SIZE, :]
        m1 = jnp.max(cg, axis=0, keepdims=True)
        first = _first_index(cg == m1, lidx, GROUP_SIZE)
        m2 = jnp.max(jnp.where(lidx == first, NEG, cg), axis=0, keepdims=True)
        gscore.append(m1 + m2)
    gs = jnp.concatenate(gscore, axis=0)
    gidx = lax.broadcasted_iota(jnp.int32, (N_GROUPS, tn), 0)
    gsel = jnp.zeros((N_GROUPS, tn), jnp.bool_)
    for _ in range(TOPK_GROUPS):
        gm = jnp.max(gs, axis=0, keepdims=True)
        first = _first_index(gs == gm, gidx, N_GROUPS)
        hit = gidx == first
        gsel = gsel | hit
        gs = jnp.where(hit, NEG, gs)
    gself = gsel.astype(F32)
    emask = jnp.concatenate(
        [jnp.broadcast_to(gself[g:g + 1, :], (GROUP_SIZE, tn)) for g in range(N_GROUPS)], axis=0)
    cur = jnp.where(emask > 0.5, choice, NEG)
    ids, ws = [], []
    chosen = jnp.zeros((N_EXPERTS, tn), F32)
    for _ in range(TOP_K):
        m = jnp.max(cur, axis=0, keepdims=True)
        first = _first_index(cur == m, eidx, N_EXPERTS)
        hit = eidx == first
        ids.append(first)
        ws.append(jnp.sum(jnp.where(hit, scores, 0.0), axis=0, keepdims=True))
        cur = jnp.where(hit, NEG, cur)
        chosen = chosen + hit.astype(F32)
    wsel = jnp.concatenate(ws, axis=0)
    idx_ref[0] = jnp.concatenate(ids, axis=0)
    cnt_ref[0, 0] = jnp.sum(chosen, axis=1, keepdims=True).astype(jnp.int32)
    w_ref[0] = wsel / jnp.sum(wsel, axis=0, keepdims=True) * ROUTE_SCALE
    fb = f.astype(BF16)
    h = _silu(_dot(fb, ws1_ref[...])) * _dot(fb, ws3_ref[...])
    xs_ref[0] = x + mod[:, 5 * d:6 * d] * _dot(h.astype(BF16), ws2_ref[...])


def _route(x, mod, g2, wrt, eb, ws1, ws3, ws2, n_ctx_tiles):
    b, n, d = x.shape
    tn = TOK_TILE
    nb = b
    ds = ws1.shape[-1]

    def mod_idx(bi, t):
        return (jnp.where(t < n_ctx_tiles, nb, bi), 0, 0)

    tok = lambda bi, t: (bi, t, 0)
    lane_tok = lambda bi, t: (bi, 0, t)
    const = lambda bi, t: (0, 0)
    return pl.pallas_call(
        functools.partial(_route_kernel, d=d),
        grid=(b, n // tn),
        in_specs=[
            pl.BlockSpec((1, tn, d), tok),
            pl.BlockSpec((1, 1, N_MOD * d), mod_idx),
            pl.BlockSpec((1, d), const),
            pl.BlockSpec((N_EXPERTS, d), const),
            pl.BlockSpec((N_EXPERTS, 1), const),
            pl.BlockSpec((d, ds), const),
            pl.BlockSpec((d, ds), const),
            pl.BlockSpec((ds, d), const),
        ],
        out_specs=[
            pl.BlockSpec((1, tn * (d // 128), 128), tok),
            pl.BlockSpec((1, TOP_K, tn), lane_tok),
            pl.BlockSpec((1, TOP_K, tn), lane_tok),
            pl.BlockSpec((1, tn, d), tok),
            pl.BlockSpec((1, 1, N_EXPERTS, 1), lambda bi, t: (bi, t, 0, 0)),
        ],
        out_shape=[
            jax.ShapeDtypeStruct((b, n * (d // 128), 128), F32),
            jax.ShapeDtypeStruct((b, TOP_K, n), jnp.int32),
            jax.ShapeDtypeStruct((b, TOP_K, n), F32),
            jax.ShapeDtypeStruct((b, n, d), F32),
            jax.ShapeDtypeStruct((b, n // tn, N_EXPERTS, 1), jnp.int32),
        ],
        compiler_params=_cparams(("parallel", "parallel")),
        name="route_shared",
    )(x, mod, g2, wrt, eb, ws1, ws3, ws2)


def _expert_kernel(iexp_ref, iblk_ref, iflag_ref, nitems_ref, tok_ref, tok_next_ref, tok_ahead_ref,
                   dst_prev_ref, dst_ref, erow_ref, f_hbm, w1_ref, w3_ref, w2_ref, y_hbm, xbuf, ybuf,
                   wb1, wb3, wb2, gsem, ssem):
    i = pl.program_id(0)
    nitems = nitems_ref[0]
    blk = iblk_ref[i]
    expert = iexp_ref[i]
    first = (iflag_ref[i] & 1) != 0
    last = (iflag_ref[i] & 2) != 0
    ring = xbuf.shape[0]
    slot = blk % ring
    slot_prev = (blk + ring - 1) % ring
    slot_prev2 = (blk + ring - 2) % ring
    slot_ahead = (blk + 2) % ring
    rows = EXPERT_BLOCK
    parts = xbuf.shape[1] // rows
    yparts = ybuf.shape[1] // rows
    n_blocks = y_hbm.shape[0] // (rows * yparts)

    def gather(idx_ref, s):
        for j in range(rows):
            src = pl.multiple_of(idx_ref[0, 0, j], parts)
            pltpu.make_async_copy(f_hbm.at[pl.ds(src, parts)], xbuf.at[s, pl.ds(j * parts, parts)],
                                  gsem.at[s]).start(priority=j % 2)

    def gather_wait(s):
        pltpu.make_async_copy(f_hbm.at[pl.ds(0, rows * parts)], xbuf.at[s], gsem.at[s]).wait()

    def scatter(idx_ref, s):
        for j in range(rows):
            dst = pl.multiple_of(idx_ref[0, 0, j], yparts)
            pltpu.make_async_copy(ybuf.at[s, pl.ds(j * yparts, yparts)], y_hbm.at[pl.ds(dst, yparts)],
                                  ssem.at[s]).start(priority=j % 2)

    def scatter_wait(s):
        pltpu.make_async_copy(ybuf.at[s], y_hbm.at[pl.ds(0, rows * yparts)], ssem.at[s]).wait()

    def expert_rows(s, after_up=None, after_down=None):
        xb = jnp.concatenate([xbuf[s, pl.ds(p, rows, stride=parts), :] for p in range(parts)],
                             axis=1).astype(BF16)
        h1 = _dot(xb, wb1[...])
        h3 = _dot(xb, wb3[...])
        if after_up is not None:
            after_up()
        y = _dot((_silu(h1) * h3).astype(BF16), wb2[...])
        if after_down is not None:
            after_down()
        return y

    def bf16_bits(v):
        return lax.bitcast_convert_type(v.astype(BF16).astype(F32), jnp.uint32)

    def store_rows(s, y, own_rows=None):
        for p in range(yparts):
            lo = bf16_bits(y[:, p * 128:(p + 1) * 128]) >> 16
            hi = bf16_bits(y[:, (yparts + p) * 128:(yparts + p + 1) * 128]) & jnp.uint32(0xFFFF0000)
            part = hi | lo
            if own_rows is not None:
                part = jnp.where(own_rows, part, ybuf[s, pl.ds(p, rows, stride=yparts), :])
            ybuf[s, pl.ds(p, rows, stride=yparts), :] = part

    valid = i < nitems

    @pl.when(valid & ((iflag_ref[i] & 4) != 0))
    def _():
        wb1[...] = w1_ref[0, 0].astype(BF16)
        wb3[...] = w3_ref[0, 0].astype(BF16)
        wb2[...] = w2_ref[0, 0].astype(BF16)

    fast = valid & first & last & (blk >= 3) & (blk <= n_blocks - 3)

    @pl.when(fast)
    def _():
        gather_wait(slot)
        scatter_wait(slot)
        y = expert_rows(slot, after_up=lambda: gather(tok_ahead_ref, slot_ahead),
                        after_down=lambda: scatter(dst_prev_ref, slot_prev))
        store_rows(slot, y)

    @pl.when(valid & jnp.logical_not(fast))
    def _():
        @pl.when(first)
        def _():
            @pl.when(blk == 0)
            def _():
                gather(tok_ref, 0)
                if n_blocks > 1:
                    gather(tok_next_ref, 1)

            @pl.when(blk + 2 < n_blocks)
            def _():
                gather(tok_ahead_ref, slot_ahead)

            @pl.when(blk >= 1)
            def _():
                scatter(dst_prev_ref, slot_prev)

            gather_wait(slot)

            @pl.when(blk >= 3)
            def _():
                scatter_wait(slot)

        y = expert_rows(slot)

        @pl.when(first)
        def _():
            store_rows(slot, y)

        @pl.when(jnp.logical_not(first))
        def _():
            ours = (erow_ref[0] == expert) & (lax.broadcasted_iota(jnp.int32, (rows, rows), 0)
                                             == lax.broadcasted_iota(jnp.int32, (rows, rows), 1))
            store_rows(slot, y, own_rows=jnp.sum(ours.astype(F32), axis=1, keepdims=True) > 0.5)

        @pl.when(last & (blk == n_blocks - 1))
        def _():
            scatter(dst_ref, slot)

            @pl.when(blk >= 2)
            def _():
                scatter_wait(slot_prev2)

            @pl.when(blk >= 1)
            def _():
                scatter_wait(slot_prev)

            scatter_wait(slot)


def _experts(iexp, iblk, iflag, nitems, row_tok, row_dst, erow, f_rows, w1, w3, w2, layer):
    n_items = iexp.shape[0]
    n_blocks = row_tok.shape[0]
    d, de = w1.shape[2], w1.shape[3]
    parts = d // 128
    yparts = parts // 2
    rows = EXPERT_BLOCK
    last = n_blocks - 1
    cur = lambda i, ie, ib, fl, nt: (ib[i], 0, 0)
    nxt = lambda i, ie, ib, fl, nt: (jnp.minimum(ib[i] + 1, last), 0, 0)
    ahd = lambda i, ie, ib, fl, nt: (jnp.minimum(ib[i] + 2, last), 0, 0)
    prv = lambda i, ie, ib, fl, nt: (jnp.maximum(ib[i] - 1, 0), 0, 0)
    wsel = lambda i, ie, ib, fl, nt: (layer, ie[i], 0, 0)
    grid_spec = pltpu.PrefetchScalarGridSpec(
        num_scalar_prefetch=4,
        grid=(n_items,),
        in_specs=[
            pl.BlockSpec((1, 1, rows), cur, memory_space=pltpu.SMEM),
            pl.BlockSpec((1, 1, rows), nxt, memory_space=pltpu.SMEM),
            pl.BlockSpec((1, 1, rows), ahd, memory_space=pltpu.SMEM),
            pl.BlockSpec((1, 1, rows), prv, memory_space=pltpu.SMEM),
            pl.BlockSpec((1, 1, rows), cur, memory_space=pltpu.SMEM),
            pl.BlockSpec((1, 1, rows), cur),
            pl.BlockSpec(memory_space=pl.ANY),
            pl.BlockSpec((1, 1, d, de), wsel),
            pl.BlockSpec((1, 1, d, de), wsel),
            pl.BlockSpec((1, 1, de, d), wsel),
        ],
        out_specs=pl.BlockSpec(memory_space=pl.ANY),
        scratch_shapes=[
            pltpu.VMEM((EXPERT_RING, rows * parts, 128), F32),
            pltpu.VMEM((EXPERT_RING, rows * yparts, 128), jnp.uint32),
            pltpu.VMEM((d, de), BF16),
            pltpu.VMEM((d, de), BF16),
            pltpu.VMEM((de, d), BF16),
            pltpu.SemaphoreType.DMA((EXPERT_RING,)),
            pltpu.SemaphoreType.DMA((EXPERT_RING,)),
        ],
    )
    return pl.pallas_call(
        _expert_kernel,
        grid_spec=grid_spec,
        out_shape=jax.ShapeDtypeStruct((n_blocks * rows * yparts, 128), jnp.uint32),
        compiler_params=_cparams(("arbitrary",)),
        name="experts",
    )(iexp, iblk, iflag, nitems, row_tok, row_tok, row_tok, row_dst, row_dst, erow, f_rows, w1, w3, w2)


ASSIGN_BITS = 20


def _routing_tables(idx, counts, n_tokens, parts):
    n_assign = n_tokens * TOP_K
    blk = EXPERT_BLOCK
    assert n_assign % blk == 0 and n_assign <= (1 << ASSIGN_BITS)
    n_blocks = n_assign // blk
    n_items = n_blocks + N_EXPERTS
    flat_e = idx.reshape(-1).astype(jnp.int32)
    key = jnp.sort((flat_e << ASSIGN_BITS) | jnp.arange(n_assign, dtype=jnp.int32), stable=False)
    e_sorted = key >> ASSIGN_BITS
    order = key & ((1 << ASSIGN_BITS) - 1)
    tok = order // TOP_K
    row_dst = ((order % TOP_K) * n_tokens + tok) * (parts // 2)
    tok = tok * parts

    experts = jnp.arange(N_EXPERTS, dtype=jnp.int32)
    ends = jnp.cumsum(counts)
    starts = ends - counts
    first_blk = starts // blk
    n_be = jnp.where(ends > starts, (ends - 1) // blk - first_blk + 1, 0)
    item_end = jnp.cumsum(n_be)
    item_off = item_end - n_be
    total = item_end[-1]
    i = jnp.arange(n_items, dtype=jnp.int32)
    iexp = jnp.minimum(jnp.sum((item_end[None, :] <= i[:, None]).astype(jnp.int32), axis=1), N_EXPERTS - 1)
    pick = (iexp[:, None] == experts[None, :]).astype(jnp.int32)
    base = jnp.sum(pick * (first_blk - item_off)[None, :], axis=1)
    iblk = jnp.where(i < total, base + i, n_blocks - 1).astype(jnp.int32)
    prev_blk = jnp.concatenate([jnp.full((1,), -1, jnp.int32), iblk[:-1]])
    next_blk = jnp.concatenate([iblk[1:], jnp.full((1,), -1, jnp.int32)])
    is_first = iblk != prev_blk
    is_last = (iblk != next_blk) | (i == total - 1)
    new_expert = iexp != jnp.concatenate([jnp.full((1,), -1, jnp.int32), iexp[:-1]])
    iflag = is_first.astype(jnp.int32) + 2 * is_last.astype(jnp.int32) + 4 * new_expert.astype(jnp.int32)
    return (iexp, iblk, iflag, total.astype(jnp.int32).reshape(1), tok.reshape(n_blocks, 1, blk),
            row_dst.reshape(n_blocks, 1, blk), e_sorted.reshape(n_blocks, 1, blk))


def _combine_kernel(xs_ref, mod_ref, w_ref, *refs, d):
    y_refs, o_ref = refs[:TOP_K], refs[TOP_K]
    w = w_ref[0]
    tn = w.shape[0]
    yparts = d // 256
    wk = [jnp.broadcast_to(w[:, k:k + 1], (tn, 128)) for k in range(TOP_K)]
    for p in range(yparts):
        lo = jnp.zeros((tn, 128), F32)
        hi = jnp.zeros((tn, 128), F32)
        for k in range(TOP_K):
            word = y_refs[k][pl.ds(p, tn, stride=yparts), :]
            lo = lo + lax.bitcast_convert_type(word << 16, F32) * wk[k]
            hi = hi + lax.bitcast_convert_type(word & jnp.uint32(0xFFFF0000), F32) * wk[k]
        for q, tot in ((p, lo), (yparts + p, hi)):
            sl = slice(q * 128, (q + 1) * 128)
            o_ref[0, :, sl] = xs_ref[0, :, sl] + mod_ref[0][:, 5 * d + q * 128:5 * d + (q + 1) * 128] * tot


def _combine(xs, mod, w, y, n_ctx_tiles):
    b, n, d = xs.shape
    tn = TOK_TILE
    nb = b
    tiles = n // tn
    per_slot = b * tiles

    def mod_idx(bi, t):
        return (jnp.where(t < n_ctx_tiles, nb, bi), 0, 0)

    tok = lambda bi, t: (bi, t, 0)
    y_specs = [pl.BlockSpec((tn * (d // 256), 128),
                            functools.partial(lambda bi, t, k: (k * per_slot + bi * tiles + t, 0), k=k))
               for k in range(TOP_K)]
    return pl.pallas_call(
        functools.partial(_combine_kernel, d=d),
        grid=(b, tiles),
        in_specs=[pl.BlockSpec((1, tn, d), tok), pl.BlockSpec((1, 1, N_MOD * d), mod_idx),
                  pl.BlockSpec((1, tn, TOP_K), tok)] + y_specs,
        out_specs=pl.BlockSpec((1, tn, d), tok),
        out_shape=jax.ShapeDtypeStruct((b, n, d), F32),
        compiler_params=_cparams(("parallel", "parallel")),
        name="combine",
    )(xs, mod, w, *([y] * TOP_K))


def _final_kernel(x_ref, g_ref, o_ref):
    o_ref[0] = _rms_rows(x_ref[0], g_ref[...])


def _final_norm(x, g, n_ctx_tiles):
    b, n, d = x.shape
    tn = TOK_TILE
    n_lat_tiles = n // tn - n_ctx_tiles
    return pl.pallas_call(
        _final_kernel,
        grid=(b, n_lat_tiles),
        in_specs=[pl.BlockSpec((1, tn, d), lambda bi, t: (bi, t + n_ctx_tiles, 0)),
                  pl.BlockSpec((1, d), lambda bi, t: (0, 0))],
        out_specs=pl.BlockSpec((1, tn, d), lambda bi, t: (bi, t, 0)),
        out_shape=jax.ShapeDtypeStruct((b, n_lat_tiles * tn, d), F32),
        compiler_params=_cparams(("parallel", "parallel")),
        name="final_norm",
    )(x, g)


def _regroup_qkv_columns(w):
    a_q, a_kv = A_HEADS * HEAD_DIM, A_KV * HEAD_DIM
    b_q, b_kv = B_HEADS * HEAD_DIM, B_KV * HEAD_DIM
    c_qk, c_v = 2 * C_HEADS * HEAD_DIM, C_HEADS * 2 * HEAD_DIM
    off = np.concatenate([[0], np.cumsum((a_q, a_kv, a_kv, b_q, b_kv, b_kv, c_qk, c_qk, c_v))])
    seg = lambda i: w[:, int(off[i]):int(off[i + 1])]
    return jnp.concatenate([seg(0), seg(3), seg(6), seg(1), seg(4), seg(7), seg(2), seg(5), seg(8)], axis=1)


def _rotary_partner(row):
    blocks = row.reshape(-1, 2, ROPE_FREQS)
    return blocks[:, ::-1, :].reshape(-1)


def _rope_tables(n_ctx, n_lat):
    t = jnp.arange(n_lat, dtype=jnp.int32)
    row_pos = (t // GRID_W).astype(F32)
    col_pos = (t % GRID_W).astype(F32)
    inv_freq = jnp.power(ROPE_THETA, -jnp.arange(ROPE_FREQS, dtype=F32) / ROPE_FREQS)
    ang_r = row_pos[:, None] * inv_freq
    ang_c = col_pos[:, None] * inv_freq
    cos64 = jnp.concatenate([jnp.cos(ang_r), jnp.cos(ang_r), jnp.cos(ang_c), jnp.cos(ang_c)], axis=1)
    sin64 = jnp.concatenate([-jnp.sin(ang_r), jnp.sin(ang_r), -jnp.sin(ang_c), jnp.sin(ang_c)], axis=1)
    cos64 = jnp.concatenate([jnp.ones((n_ctx, HEAD_DIM), F32), cos64], axis=0)
    sin64 = jnp.concatenate([jnp.zeros((n_ctx, HEAD_DIM), F32), sin64], axis=0)
    return jnp.tile(cos64, (1, 2)), jnp.tile(sin64, (1, 2))


def kernel(x, c, ctx, c_ctx, w_mod, b_mod, g_norm1, w_qkv, g_qnorm_a, g_knorm_a, sink_b, lam_q1, lam_k1, lam_q2, lam_k2, g_subln_c, w_br_a, w_br_b, w_br_c, w_gate, b_gate, w_out, g_norm2, w_router, e_bias, w1, w3, w2, ws1, ws3, ws2, g_final):
    bsz, n_lat, d = x.shape
    n_ctx = ctx.shape[1]
    depth = w_mod.shape[0]
    n_tok = n_ctx + n_lat
    assert n_ctx % TOK_TILE == 0 and n_lat % KEY_CHUNK == 0 and n_ctx % 128 == 0
    n_ctx_tiles = n_ctx // TOK_TILE
    n_all = bsz * n_tok

    rows = -(-(bsz + 1) // 8) * 8
    cvec = jnp.concatenate([c, c_ctx[None, :], jnp.zeros((rows - bsz - 1, d), F32)], axis=0)
    mod_all = _modulation(cvec, w_mod, b_mod)

    cos_t, sin_t = _rope_tables(n_ctx, n_lat)
    head_of = np.arange(R_A) // HEAD_DIM
    ones_blk = jnp.asarray((head_of[:, None] == head_of[None, :]).astype(np.float32), BF16)
    scale = HEAD_DIM ** -0.5 * LOG2E
    unit = jnp.ones((HEAD_DIM,), F32)

    xs = jnp.concatenate([ctx, x], axis=1)
    for l in range(depth):
        lam_init = 0.8 - 0.6 * math.exp(-0.3 * l)
        lam = (jnp.exp(jnp.dot(lam_q1[l], lam_k1[l])) - jnp.exp(jnp.dot(lam_q2[l], lam_k2[l]))).astype(F32) + lam_init
        mod = mod_all[l].reshape(rows, 1, N_MOD * d)
        g1 = g_norm1[l].reshape(1, d)

        wqkv = _regroup_qkv_columns(w_qkv[l]).astype(BF16)
        grow = jnp.concatenate([jnp.tile(g_qnorm_a[l] * scale, A_HEADS), jnp.tile(unit * scale, B_HEADS),
                                jnp.tile(unit * scale, 2 * C_HEADS), jnp.tile(g_knorm_a[l], A_KV),
                                jnp.tile(unit, B_KV), jnp.tile(unit, 2 * C_HEADS)])
        gsrow = _rotary_partner(grow)
        q_all, kt_all, vx_ab, vx_c = _pre_attention(xs, mod, g1, wqkv, ones_blk, grow.reshape(1, R_W),
                                                    gsrow.reshape(1, R_W), cos_t, sin_t, n_ctx_tiles)

        scal = jnp.concatenate([sink_b[l].astype(F32) * LOG2E, lam.reshape(1), jnp.full((1,), 1.0 - lam_init, F32)])
        scal_c = jnp.concatenate([jnp.zeros((DIFF_HEADS_PER_STEP,), F32), lam.reshape(1),
                                  jnp.full((1,), 1.0 - lam_init, F32)])
        gsub = g_subln_c[l].reshape(1, 2 * HEAD_DIM)
        g64 = jnp.ones((1, HEAD_DIM), F32)
        ga, gb = A_HEADS // A_KV, B_HEADS // B_KV
        ya = _attention(scal, g64, q_all, kt_all, vx_ab, heads=ga, hpv=ga, hpk=ga, mode="global",
                        n_ctx=n_ctx, units=A_KV, q_unit0=0, k_unit0=0, v_unit0=0)
        yb = _attention(scal, g64, q_all, kt_all, vx_ab, heads=B_HEADS, hpv=gb, hpk=gb, mode="window",
                        n_ctx=n_ctx, units=1, q_unit0=QA_W // QB_W, k_unit0=A_KV // B_KV, v_unit0=A_KV // B_KV)
        yc = _attention(scal_c, gsub, q_all, kt_all, vx_c, heads=DIFF_HEADS_PER_STEP, hpv=2, hpk=1,
                        mode="diff", n_ctx=n_ctx, units=2 * C_HEADS // DIFF_HEADS_PER_STEP,
                        q_unit0=(QA_W + QB_W) // 256, k_unit0=(A_KV + B_KV) // DIFF_HEADS_PER_STEP, v_unit0=0)

        x1 = _merge(xs, mod, g1, ya, yb, yc,
                    w_gate[l].astype(BF16), b_gate[l].reshape(1, -1), w_br_a[l].astype(BF16),
                    w_br_b[l].astype(BF16), w_br_c[l].astype(BF16), w_out[l].astype(BF16), n_ctx_tiles)

        f, idx_t, w_t, x_sh, tile_counts = _route(x1, mod, g_norm2[l].reshape(1, d), w_router[l].T,
                                     e_bias[l].reshape(N_EXPERTS, 1), ws1[l].astype(BF16),
                                     ws3[l].astype(BF16), ws2[l].astype(BF16), n_ctx_tiles)

        idx = idx_t.transpose(0, 2, 1).reshape(n_all, TOP_K)
        counts = jnp.sum(tile_counts, axis=(0, 1)).reshape(N_EXPERTS)
        iexp, iblk, iflag, nitems, row_tok, row_dst, erow = _routing_tables(idx, counts, n_all, d // 128)
        y = _experts(iexp, iblk, iflag, nitems, row_tok, row_dst, erow, f.reshape(n_all * (d // 128), 128),
                     w1, w3, w2, l)
        xs = _combine(x_sh, mod, w_t.transpose(0, 2, 1), y, n_ctx_tiles)

    return _final_norm(xs, g_final.reshape(1, d), n_ctx_tiles)
```

```python
import functools
import math

import numpy as np
import jax
import jax.numpy as jnp
from jax import lax
from jax.experimental import pallas as pl
from jax.experimental.pallas import tpu as pltpu

F32 = jnp.float32
BF16 = jnp.bfloat16

HEAD_DIM = 64
ROPE_FREQS = HEAD_DIM // 4
ROPE_THETA = 10000.0
GRID_W = 64
WINDOW = 128
A_HEADS, A_KV = 8, 2
B_HEADS, B_KV = 8, 2
C_HEADS = 4
N_EXPERTS = 128
TOP_K = 8
N_GROUPS = 8
TOPK_GROUPS = 4
GROUP_SIZE = N_EXPERTS // N_GROUPS
ROUTE_SCALE = 2.5
EXPERT_BLOCK = 128
EXPERT_RING = 3
N_MOD = 6
EPS = 1e-6
NEG = -1e30

QA_W, QB_W, QC_W = A_HEADS * HEAD_DIM, B_HEADS * HEAD_DIM, 2 * C_HEADS * HEAD_DIM
KA_W, KB_W, KC_W = A_KV * HEAD_DIM, B_KV * HEAD_DIM, 2 * C_HEADS * HEAD_DIM
Q_W = QA_W + QB_W + QC_W
K_W = KA_W + KB_W + KC_W
R_W = Q_W + K_W
R_A = QA_W + KA_W
V_W = A_KV * HEAD_DIM + B_KV * HEAD_DIM + C_HEADS * 2 * HEAD_DIM
K_HEADS = K_W // HEAD_DIM

TOK_TILE = 256
ATTN_TQ = 256
DIFF_HEADS_PER_STEP = 4
KEY_CHUNK = 512
MAX_KEY_CHUNK = {"global": 8448, "diff": 8448, "window": 128}
LOG2E = math.log2(math.e)
VMEM_LIMIT = 56 * 1024 * 1024


def _cparams(sem, **kw):
    return pltpu.CompilerParams(dimension_semantics=sem, vmem_limit_bytes=VMEM_LIMIT, **kw)


def _dot(a, b):
    return jnp.dot(a, b, preferred_element_type=F32)


def _sigmoid(x):
    return 1.0 / (1.0 + jnp.exp(-x))


def _silu(x):
    return x * _sigmoid(x)


def _rms_rows(x, g):
    return x * lax.rsqrt(jnp.mean(x * x, axis=-1, keepdims=True) + EPS) * g


def _mod_kernel(c_ref, w_ref, b_ref, o_ref):
    cs = _silu(c_ref[...])
    o_ref[0] = jnp.dot(cs, w_ref[0], preferred_element_type=F32,
                       precision=lax.Precision.HIGHEST) + b_ref[0]


def _modulation(cvec, w_mod, b_mod):
    depth, d, n = w_mod.shape
    rows = cvec.shape[0]
    bn = 1536
    return pl.pallas_call(
        _mod_kernel,
        grid=(depth, n // bn),
        in_specs=[
            pl.BlockSpec((rows, d), lambda l, j: (0, 0)),
            pl.BlockSpec((1, d, bn), lambda l, j: (l, 0, j)),
            pl.BlockSpec((1, 1, bn), lambda l, j: (l, 0, j)),
        ],
        out_specs=pl.BlockSpec((1, rows, bn), lambda l, j: (l, 0, j)),
        out_shape=jax.ShapeDtypeStruct((depth, rows, n), F32),
        compiler_params=_cparams(("parallel", "parallel")),
        name="modulation",
    )(cvec, w_mod, b_mod.reshape(depth, 1, n))


def _pre_kernel(x_ref, mod_ref, g1_ref, wqkv_ref, ones_ref, grow_ref, gsrow_ref,
                cos_ref, sin_ref, q_out, kt_out, vab_out, vc_out, *, d):
    x = x_ref[0]
    mod = mod_ref[0]
    tn = x.shape[0]
    a = (_rms_rows(x, g1_ref[...]) * (1.0 + mod[:, d:2 * d]) + mod[:, 0:d]).astype(BF16)
    p = _dot(a, wqkv_ref[...])
    pa = jnp.concatenate([p[:, 0:QA_W], p[:, Q_W:Q_W + KA_W]], axis=1)
    sq = pa * pa
    hi = sq.astype(BF16)
    lo = (sq - hi.astype(F32)).astype(BF16)
    ssq = _dot(hi, ones_ref[...]) + _dot(lo, ones_ref[...])
    rinv = lax.rsqrt(ssq * (1.0 / HEAD_DIM) + EPS)
    cos = cos_ref[...]
    sin = sin_ref[...]
    lane = lax.broadcasted_iota(jnp.int32, (tn, 128), 1)
    low_half = (lane % (2 * ROPE_FREQS)) < ROPE_FREQS
    first_head = lane < HEAD_DIM
    for j in range(R_W // 128):
        sl = slice(j * 128, (j + 1) * 128)
        pj = p[:, sl]
        ps = jnp.where(low_half, pltpu.roll(pj, 128 - ROPE_FREQS, 1), pltpu.roll(pj, ROPE_FREQS, 1))
        o = pj * (grow_ref[:, sl] * cos) + ps * (gsrow_ref[:, sl] * sin)
        if (j + 1) * 128 <= QA_W:
            o = o * rinv[:, sl]
        elif j * 128 == Q_W:
            o = o * rinv[:, QA_W:QA_W + KA_W]
        if j * 128 < Q_W:
            q_out[0, :, sl] = o.astype(BF16)
        else:
            ot = jnp.transpose(o).astype(BF16)
            kh = (j * 128 - Q_W) // HEAD_DIM
            kt_out[0, kh] = ot[0:HEAD_DIM]
            kt_out[0, kh + 1] = ot[HEAD_DIM:2 * HEAD_DIM]
    ones_at_64 = (lane == HEAD_DIM).astype(F32)
    for j in range(2):
        vj = p[:, R_W + j * 128:R_W + (j + 1) * 128]
        vab_out[0, 2 * j] = jnp.where(first_head, vj, ones_at_64).astype(BF16)
        vab_out[0, 2 * j + 1] = jnp.where(first_head, pltpu.roll(vj, HEAD_DIM, 1), ones_at_64).astype(BF16)
    ones_at_0 = (lane == 0).astype(BF16)
    for j in range(C_HEADS):
        vj = p[:, R_W + (2 + j) * 128:R_W + (3 + j) * 128]
        vc_out[0, j] = jnp.concatenate([vj.astype(BF16), ones_at_0], axis=1)


def _pre_attention(x, mod, g1, wqkv, ones_blk, grow, gsrow, cos_t, sin_t, n_ctx_tiles):
    b, n, d = x.shape
    tn = TOK_TILE
    nb = b

    def mod_idx(bi, t):
        return (jnp.where(t < n_ctx_tiles, nb, bi), 0, 0)

    return pl.pallas_call(
        functools.partial(_pre_kernel, d=d),
        grid=(b, n // tn),
        in_specs=[
            pl.BlockSpec((1, tn, d), lambda bi, t: (bi, t, 0)),
            pl.BlockSpec((1, 1, N_MOD * d), mod_idx),
            pl.BlockSpec((1, d), lambda bi, t: (0, 0)),
            pl.BlockSpec((d, R_W + V_W), lambda bi, t: (0, 0)),
            pl.BlockSpec((R_A, R_A), lambda bi, t: (0, 0)),
            pl.BlockSpec((1, R_W), lambda bi, t: (0, 0)),
            pl.BlockSpec((1, R_W), lambda bi, t: (0, 0)),
            pl.BlockSpec((tn, 128), lambda bi, t: (t, 0)),
            pl.BlockSpec((tn, 128), lambda bi, t: (t, 0)),
        ],
        out_specs=[
            pl.BlockSpec((1, tn, Q_W), lambda bi, t: (bi, t, 0)),
            pl.BlockSpec((1, K_HEADS, HEAD_DIM, tn), lambda bi, t: (bi, 0, 0, t)),
            pl.BlockSpec((1, A_KV + B_KV, tn, 128), lambda bi, t: (bi, 0, t, 0)),
            pl.BlockSpec((1, C_HEADS, tn, 256), lambda bi, t: (bi, 0, t, 0)),
        ],
        out_shape=[
            jax.ShapeDtypeStruct((b, n, Q_W), BF16),
            jax.ShapeDtypeStruct((b, K_HEADS, HEAD_DIM, n), BF16),
            jax.ShapeDtypeStruct((b, A_KV + B_KV, n, 128), BF16),
            jax.ShapeDtypeStruct((b, C_HEADS, n, 256), BF16),
        ],
        compiler_params=_cparams(("parallel", "parallel")),
        name="pre_attention",
    )(x, mod, g1, wqkv, ones_blk, grow, gsrow, cos_t, sin_t)


def _attn_kernel(sc_ref, gsub_ref, q_ref, kt_ref, v_ref, o_ref, m_sc, acc_sc, *, heads, hpv,
                 hpk, mode, n_ctx, n_tok, tq, ck, dv):
    u = pl.program_id(1)
    qi = pl.program_id(2)
    is_lat = qi >= n_ctx // tq
    dvx = acc_sc.shape[-1]

    for g in range(heads):
        if mode == "window":
            m_sc[g] = jnp.full((tq, 1), sc_ref[u * heads + g], F32)
            lane = lax.broadcasted_iota(jnp.int32, (tq, dvx), 1)
            acc_sc[g] = jnp.where(lane == dv, 1.0, 0.0).astype(F32)
        else:
            m_sc[g] = jnp.full((tq, 1), NEG, F32)
            acc_sc[g] = jnp.zeros((tq, dvx), F32)

    def probs(g, s, mask):
        if mask is not None:
            s = jnp.where(mask, s, NEG)
        m = m_sc[g]
        m_new = jnp.maximum(m, jnp.max(s, axis=-1, keepdims=True))
        m_sc[g] = m_new
        return jnp.exp2(s - m_new).astype(BF16), jnp.exp2(m - m_new)

    def q_of(g):
        return q_ref[0, :, g * HEAD_DIM:(g + 1) * HEAD_DIM]

    def update_all(kt_of, v_of, mask=None):
        s_next = _dot(q_of(0), kt_of(0))
        for g in range(heads):
            s = s_next
            if g + 1 < heads:
                s_next = _dot(q_of(g + 1), kt_of(g + 1))
            p, alpha = probs(g, s, mask)
            acc_sc[g] = alpha * acc_sc[g] + _dot(p, v_of(g))

    def step(k0, size):
        update_all(lambda g: kt_ref[0, g // hpk, :, pl.ds(k0, size)],
                   lambda g: v_ref[0, g // hpv, pl.ds(k0, size), :])

    if mode == "window":
        span = tq + 2 * WINDOW
        start = pl.multiple_of(jnp.clip(qi * tq - WINDOW, 0, n_tok - span), 128)
        qpos = qi * tq + lax.broadcasted_iota(jnp.int32, (tq, n_ctx + span), 0)
        col = lax.broadcasted_iota(jnp.int32, (tq, n_ctx + span), 1)
        kpos = start + col - n_ctx
        mask = (col < n_ctx) | (is_lat & (kpos >= n_ctx) & (jnp.abs(kpos - qpos) <= WINDOW))

        def v_of(g):
            gv = g // hpv
            return jnp.concatenate([v_ref[0, gv, 0:n_ctx, :], v_ref[0, gv, pl.ds(start, span), :]], axis=0)

        def kt_of(g):
            gk = g // hpk
            return jnp.concatenate([kt_ref[0, gk, :, 0:n_ctx], kt_ref[0, gk, :, pl.ds(start, span)]], axis=1)

        update_all(kt_of, v_of, mask)
    else:
        @pl.when(jnp.logical_not(is_lat))
        def _():
            step(0, n_ctx)

        def body(ci, carry):
            step(pl.multiple_of(ci * ck, 128), ck)
            return carry

        lax.fori_loop(0, jnp.where(is_lat, n_tok // ck, 0), body, 0)

    def result(g):
        acc = acc_sc[g]
        return acc[:, 0:128] / acc[:, dv:dv + 1]

    if mode == "diff":
        lam = sc_ref[heads]
        post = sc_ref[heads + 1]
        for j in range(heads // 2):
            y = result(2 * j) - lam * result(2 * j + 1)
            o_ref[0, :, j * dv:(j + 1) * dv] = (_rms_rows(y, gsub_ref[...]) * post).astype(o_ref.dtype)
    else:
        lane = lax.broadcasted_iota(jnp.int32, (tq, 128), 1)
        for j in range(heads // 2):
            pair = jnp.where(lane < dv, result(2 * j), pltpu.roll(result(2 * j + 1), dv, 1))
            o_ref[0, :, j * 128:(j + 1) * 128] = pair.astype(o_ref.dtype)


def _attention(scalars, gsub, q, kt, vx, *, heads, hpv, hpk, mode, n_ctx, units, q_unit0, k_unit0,
               v_unit0):
    b, n, _ = q.shape
    hd = HEAD_DIM
    dvx = vx.shape[-1]
    dv = gsub.shape[-1]
    assert dv < dvx
    tq = ATTN_TQ
    gk = heads // hpk
    vh = heads // hpv
    out_w = heads * hd if mode != "diff" else (heads // 2) * dv
    ck = max(c for c in range(128, MAX_KEY_CHUNK[mode] + 1, 128) if n % c == 0)
    kern = functools.partial(_attn_kernel, heads=heads, hpv=hpv, hpk=hpk, mode=mode,
                             n_ctx=n_ctx, n_tok=n, tq=tq, ck=ck, dv=dv)
    return pl.pallas_call(
        kern,
        grid=(b, units, n // tq),
        in_specs=[
            pl.BlockSpec(memory_space=pltpu.SMEM),
            pl.BlockSpec((1, dv), lambda bi, u, t: (0, 0)),
            pl.BlockSpec((1, tq, heads * hd), lambda bi, u, t: (bi, t, q_unit0 + u)),
            pl.BlockSpec((1, gk, hd, n), lambda bi, u, t: (bi, k_unit0 + u, 0, 0)),
            pl.BlockSpec((1, vh, n, dvx), lambda bi, u, t: (bi, v_unit0 + u, 0, 0)),
        ],
        out_specs=pl.BlockSpec((1, tq, out_w), lambda bi, u, t: (bi, t, u)),
        out_shape=jax.ShapeDtypeStruct((b, n, units * out_w), BF16),
        scratch_shapes=[
            pltpu.VMEM((heads, tq, 1), F32),
            pltpu.VMEM((heads, tq, dvx), F32),
        ],
        compiler_params=_cparams(("parallel", "parallel", "arbitrary")),
        name="attn_" + mode,
    )(scalars, gsub, q, kt, vx)


def _merge_kernel(x_ref, mod_ref, g1_ref, ya_ref, yb_ref, yc_ref, wg_ref, bg_ref, wa_ref, wb_ref,
                  wc_ref, wo_ref, o_ref, *, d):
    x = x_ref[0]
    mod = mod_ref[0]
    a = (_rms_rows(x, g1_ref[...]) * (1.0 + mod[:, d:2 * d]) + mod[:, 0:d]).astype(BF16)
    gate = _sigmoid(_dot(a, wg_ref[...]) + bg_ref[...])
    m = (gate[:, 0:d] * _dot(ya_ref[0], wa_ref[...])
         + gate[:, d:2 * d] * _dot(yb_ref[0], wb_ref[...])
         + gate[:, 2 * d:3 * d] * _dot(yc_ref[0], wc_ref[...]))
    mix = _dot(m.astype(BF16), wo_ref[...])
    o_ref[0] = x + mod[:, 2 * d:3 * d] * mix


def _merge(x, mod, g1, ya, yb, yc, wg, bg, wa, wb, wc, wo, n_ctx_tiles):
    b, n, d = x.shape
    tn = TOK_TILE
    nb = b
    yw = ya.shape[-1]

    def mod_idx(bi, t):
        return (jnp.where(t < n_ctx_tiles, nb, bi), 0, 0)

    tok = lambda bi, t: (bi, t, 0)
    const = lambda bi, t: (0, 0)
    return pl.pallas_call(
        functools.partial(_merge_kernel, d=d),
        grid=(b, n // tn),
        in_specs=[
            pl.BlockSpec((1, tn, d), tok),
            pl.BlockSpec((1, 1, N_MOD * d), mod_idx),
            pl.BlockSpec((1, d), const),
            pl.BlockSpec((1, tn, yw), tok),
            pl.BlockSpec((1, tn, yw), tok),
            pl.BlockSpec((1, tn, yw), tok),
            pl.BlockSpec((d, 3 * d), const),
            pl.BlockSpec((1, 3 * d), const),
            pl.BlockSpec((yw, d), const),
            pl.BlockSpec((yw, d), const),
            pl.BlockSpec((yw, d), const),
            pl.BlockSpec((d, d), const),
        ],
        out_specs=pl.BlockSpec((1, tn, d), tok),
        out_shape=jax.ShapeDtypeStruct((b, n, d), F32),
        compiler_params=_cparams(("parallel", "parallel")),
        name="merge",
    )(x, mod, g1, ya, yb, yc, wg, bg, wa, wb, wc, wo)


def _first_index(hit, idx, big):
    return jnp.min(jnp.where(hit, idx, big), axis=0, keepdims=True)


def _route_kernel(x_ref, mod_ref, g2_ref, wrt_ref, eb_ref, ws1_ref, ws3_ref, ws2_ref,
                  f_ref, idx_ref, w_ref, xs_ref, cnt_ref, *, d):
    x = x_ref[0]
    mod = mod_ref[0]
    f = _rms_rows(x, g2_ref[...]) * (1.0 + mod[:, 4 * d:5 * d]) + mod[:, 3 * d:4 * d]
    tn = f.shape[0]
    for s in range(d // 128):
        f_ref[0, pl.ds(s, tn, stride=d // 128), :] = f[:, s * 128:(s + 1) * 128]
    logits = lax.dot_general(wrt_ref[...], f, (((1,), (1,)), ((), ())),
                             preferred_element_type=F32, precision=lax.Precision.HIGHEST)
    scores = _sigmoid(logits)
    choice = scores + eb_ref[...]
    eidx = lax.broadcasted_iota(jnp.int32, (N_EXPERTS, tn), 0)
    lidx = lax.broadcasted_iota(jnp.int32, (GROUP_SIZE, tn), 0)
    gscore = []
    for g in range(N_GROUPS):
        cg = choice[g * GROUP_SIZE:(g + 1) * GROUP_SIZE, :]
        m1 = jnp.max(cg, axis=0, keepdims=True)
        first = _first_index(cg == m1, lidx, GROUP_SIZE)
        m2 = jnp.max(jnp.where(lidx == first, NEG, cg), axis=0, keepdims=True)
        gscore.append(m1 + m2)
    gs = jnp.concatenate(gscore, axis=0)
    gidx = lax.broadcasted_iota(jnp.int32, (N_GROUPS, tn), 0)
    gsel = jnp.zeros((N_GROUPS, tn), jnp.bool_)
    for _ in range(TOPK_GROUPS):
        gm = jnp.max(gs, axis=0, keepdims=True)
        first = _first_index(gs == gm, gidx, N_GROUPS)
        hit = gidx == first
        gsel = gsel | hit
        gs = jnp.where(hit, NEG, gs)
    gself = gsel.astype(F32)
    emask = jnp.concatenate(
        [jnp.broadcast_to(gself[g:g + 1, :], (GROUP_SIZE, tn)) for g in range(N_GROUPS)], axis=0)
    cur = jnp.where(emask > 0.5, choice, NEG)
    ids, ws = [], []
    chosen = jnp.zeros((N_EXPERTS, tn), F32)
    for _ in range(TOP_K):
        m = jnp.max(cur, axis=0, keepdims=True)
        first = _first_index(cur == m, eidx, N_EXPERTS)
        hit = eidx == first
        ids.append(first)
        ws.append(jnp.sum(jnp.where(hit, scores, 0.0), axis=0, keepdims=True))
        cur = jnp.where(hit, NEG, cur)
        chosen = chosen + hit.astype(F32)
    wsel = jnp.concatenate(ws, axis=0)
    idx_ref[0] = jnp.concatenate(ids, axis=0)
    cnt_ref[0, 0] = jnp.sum(chosen, axis=1, keepdims=True).astype(jnp.int32)
    w_ref[0] = wsel / jnp.sum(wsel, axis=0, keepdims=True) * ROUTE_SCALE
    fb = f.astype(BF16)
    h = _silu(_dot(fb, ws1_ref[...])) * _dot(fb, ws3_ref[...])
    xs_ref[0] = x + mod[:, 5 * d:6 * d] * _dot(h.astype(BF16), ws2_ref[...])


def _route(x, mod, g2, wrt, eb, ws1, ws3, ws2, n_ctx_tiles):
    b, n, d = x.shape
    tn = TOK_TILE
    nb = b
    ds = ws1.shape[-1]

    def mod_idx(bi, t):
        return (jnp.where(t < n_ctx_tiles, nb, bi), 0, 0)

    tok = lambda bi, t: (bi, t, 0)
    lane_tok = lambda bi, t: (bi, 0, t)
    const = lambda bi, t: (0, 0)
    return pl.pallas_call(
        functools.partial(_route_kernel, d=d),
        grid=(b, n // tn),
        in_specs=[
            pl.BlockSpec((1, tn, d), tok),
            pl.BlockSpec((1, 1, N_MOD * d), mod_idx),
            pl.BlockSpec((1, d), const),
            pl.BlockSpec((N_EXPERTS, d), const),
            pl.BlockSpec((N_EXPERTS, 1), const),
            pl.BlockSpec((d, ds), const),
            pl.BlockSpec((d, ds), const),
            pl.BlockSpec((ds, d), const),
        ],
        out_specs=[
            pl.BlockSpec((1, tn * (d // 128), 128), tok),
            pl.BlockSpec((1, TOP_K, tn), lane_tok),
            pl.BlockSpec((1, TOP_K, tn), lane_tok),
            pl.BlockSpec((1, tn, d), tok),
            pl.BlockSpec((1, 1, N_EXPERTS, 1), lambda bi, t: (bi, t, 0, 0)),
        ],
        out_shape=[
            jax.ShapeDtypeStruct((b, n * (d // 128), 128), F32),
            jax.ShapeDtypeStruct((b, TOP_K, n), jnp.int32),
            jax.ShapeDtypeStruct((b, TOP_K, n), F32),
            jax.ShapeDtypeStruct((b, n, d), F32),
            jax.ShapeDtypeStruct((b, n // tn, N_EXPERTS, 1), jnp.int32),
        ],
        compiler_params=_cparams(("parallel", "parallel")),
        name="route_shared",
    )(x, mod, g2, wrt, eb, ws1, ws3, ws2)


def _expert_kernel(iexp_ref, iblk_ref, iflag_ref, nitems_ref, tok_ref, tok_next_ref, tok_ahead_ref,
                   dst_prev_ref, dst_ref, erow_ref, f_hbm, w1_ref, w3_ref, w2_ref, y_hbm, xbuf, ybuf,
                   wb1, wb3, wb2, gsem, ssem):
    i = pl.program_id(0)
    nitems = nitems_ref[0]
    blk = iblk_ref[i]
    expert = iexp_ref[i]
    first = (iflag_ref[i] & 1) != 0
    last = (iflag_ref[i] & 2) != 0
    ring = xbuf.shape[0]
    slot = blk % ring
    slot_prev = (blk + ring - 1) % ring
    slot_prev2 = (blk + ring - 2) % ring
    slot_ahead = (blk + 2) % ring
    rows = EXPERT_BLOCK
    parts = xbuf.shape[1] // rows
    yparts = ybuf.shape[1] // rows
    n_blocks = y_hbm.shape[0] // (rows * yparts)

    def gather(idx_ref, s):
        for j in range(rows):
            src = pl.multiple_of(idx_ref[0, 0, j], parts)
            pltpu.make_async_copy(f_hbm.at[pl.ds(src, parts)], xbuf.at[s, pl.ds(j * parts, parts)],
                                  gsem.at[s]).start(priority=0)

    def gather_wait(s):
        pltpu.make_async_copy(f_hbm.at[pl.ds(0, rows * parts)], xbuf.at[s], gsem.at[s]).wait()

    def scatter(idx_ref, s):
        for j in range(rows):
            dst = pl.multiple_of(idx_ref[0, 0, j], yparts)
            pltpu.make_async_copy(ybuf.at[s, pl.ds(j * yparts, yparts)], y_hbm.at[pl.ds(dst, yparts)],
                                  ssem.at[s]).start(priority=1)

    def scatter_wait(s):
        pltpu.make_async_copy(ybuf.at[s], y_hbm.at[pl.ds(0, rows * yparts)], ssem.at[s]).wait()

    def expert_rows(s, after_up=None, after_down=None):
        xb = jnp.concatenate([xbuf[s, pl.ds(p, rows, stride=parts), :] for p in range(parts)],
                             axis=1).astype(BF16)
        h1 = _dot(xb, wb1[...])
        h3 = _dot(xb, wb3[...])
        if after_up is not None:
            after_up()
        y = _dot((_silu(h1) * h3).astype(BF16), wb2[...])
        if after_down is not None:
            after_down()
        return y

    def bf16_bits(v):
        return lax.bitcast_convert_type(v.astype(BF16).astype(F32), jnp.uint32)

    def store_rows(s, y, own_rows=None):
        for p in range(yparts):
            lo = bf16_bits(y[:, p * 128:(p + 1) * 128]) >> 16
            hi = bf16_bits(y[:, (yparts + p) * 128:(yparts + p + 1) * 128]) & jnp.uint32(0xFFFF0000)
            part = hi | lo
            if own_rows is not None:
                part = jnp.where(own_rows, part, ybuf[s, pl.ds(p, rows, stride=yparts), :])
            ybuf[s, pl.ds(p, rows, stride=yparts), :] = part

    valid = i < nitems

    @pl.when(valid & ((iflag_ref[i] & 4) != 0))
    def _():
        wb1[...] = w1_ref[0, 0].astype(BF16)
        wb3[...] = w3_ref[0, 0].astype(BF16)
        wb2[...] = w2_ref[0, 0].astype(BF16)

    fast = valid & first & last & (blk >= 3) & (blk <= n_blocks - 3)

    @pl.when(fast)
    def _():
        gather_wait(slot)
        scatter_wait(slot)
        y = expert_rows(slot, after_up=lambda: gather(tok_ahead_ref, slot_ahead),
                        after_down=lambda: scatter(dst_prev_ref, slot_prev))
        store_rows(slot, y)

    @pl.when(valid & jnp.logical_not(fast))
    def _():
        @pl.when(first)
        def _():
            @pl.when(blk == 0)
            def _():
                gather(tok_ref, 0)
                if n_blocks > 1:
                    gather(tok_next_ref, 1)

            @pl.when(blk + 2 < n_blocks)
            def _():
                gather(tok_ahead_ref, slot_ahead)

            @pl.when(blk >= 1)
            def _():
                scatter(dst_prev_ref, slot_prev)

            gather_wait(slot)

            @pl.when(blk >= 3)
            def _():
                scatter_wait(slot)

        y = expert_rows(slot)

        @pl.when(first)
        def _():
            store_rows(slot, y)

        @pl.when(jnp.logical_not(first))
        def _():
            ours = (erow_ref[0] == expert) & (lax.broadcasted_iota(jnp.int32, (rows, rows), 0)
                                             == lax.broadcasted_iota(jnp.int32, (rows, rows), 1))
            store_rows(slot, y, own_rows=jnp.sum(ours.astype(F32), axis=1, keepdims=True) > 0.5)

        @pl.when(last & (blk == n_blocks - 1))
        def _():
            scatter(dst_ref, slot)

            @pl.when(blk >= 2)
            def _():
                scatter_wait(slot_prev2)

            @pl.when(blk >= 1)
            def _():
                scatter_wait(slot_prev)

            scatter_wait(slot)


def _experts(iexp, iblk, iflag, nitems, row_tok, row_dst, erow, f_rows, w1, w3, w2, layer):
    n_items = iexp.shape[0]
    n_blocks = row_tok.shape[0]
    d, de = w1.shape[2], w1.shape[3]
    parts = d // 128
    yparts = parts // 2
    rows = EXPERT_BLOCK
    last = n_blocks - 1
    cur = lambda i, ie, ib, fl, nt: (ib[i], 0, 0)
    nxt = lambda i, ie, ib, fl, nt: (jnp.minimum(ib[i] + 1, last), 0, 0)
    ahd = lambda i, ie, ib, fl, nt: (jnp.minimum(ib[i] + 2, last), 0, 0)
    prv = lambda i, ie, ib, fl, nt: (jnp.maximum(ib[i] - 1, 0), 0, 0)
    wsel = lambda i, ie, ib, fl, nt: (layer, ie[i], 0, 0)
    grid_spec = pltpu.PrefetchScalarGridSpec(
        num_scalar_prefetch=4,
        grid=(n_items,),
        in_specs=[
            pl.BlockSpec((1, 1, rows), cur, memory_space=pltpu.SMEM),
            pl.BlockSpec((1, 1, rows), nxt, memory_space=pltpu.SMEM),
            pl.BlockSpec((1, 1, rows), ahd, memory_space=pltpu.SMEM),
            pl.BlockSpec((1, 1, rows), prv, memory_space=pltpu.SMEM),
            pl.BlockSpec((1, 1, rows), cur, memory_space=pltpu.SMEM),
            pl.BlockSpec((1, 1, rows), cur),
            pl.BlockSpec(memory_space=pl.ANY),
            pl.BlockSpec((1, 1, d, de), wsel),
            pl.BlockSpec((1, 1, d, de), wsel),
            pl.BlockSpec((1, 1, de, d), wsel),
        ],
        out_specs=pl.BlockSpec(memory_space=pl.ANY),
        scratch_shapes=[
            pltpu.VMEM((EXPERT_RING, rows * parts, 128), F32),
            pltpu.VMEM((EXPERT_RING, rows * yparts, 128), jnp.uint32),
            pltpu.VMEM((d, de), BF16),
            pltpu.VMEM((d, de), BF16),
            pltpu.VMEM((de, d), BF16),
            pltpu.SemaphoreType.DMA((EXPERT_RING,)),
            pltpu.SemaphoreType.DMA((EXPERT_RING,)),
        ],
    )
    return pl.pallas_call(
        _expert_kernel,
        grid_spec=grid_spec,
        out_shape=jax.ShapeDtypeStruct((n_blocks * rows * yparts, 128), jnp.uint32),
        compiler_params=_cparams(("arbitrary",)),
        name="experts",
    )(iexp, iblk, iflag, nitems, row_tok, row_tok, row_tok, row_dst, row_dst, erow, f_rows, w1, w3, w2)


ASSIGN_BITS = 20


def _routing_tables(idx, counts, n_tokens, parts):
    n_assign = n_tokens * TOP_K
    blk = EXPERT_BLOCK
    assert n_assign % blk == 0 and n_assign <= (1 << ASSIGN_BITS)
    n_blocks = n_assign // blk
    n_items = n_blocks + N_EXPERTS
    flat_e = idx.reshape(-1).astype(jnp.int32)
    key = jnp.sort((flat_e << ASSIGN_BITS) | jnp.arange(n_assign, dtype=jnp.int32), stable=False)
    e_sorted = key >> ASSIGN_BITS
    order = key & ((1 << ASSIGN_BITS) - 1)
    tok = order // TOP_K
    row_dst = ((order % TOP_K) * n_tokens + tok) * (parts // 2)
    tok = tok * parts

    experts = jnp.arange(N_EXPERTS, dtype=jnp.int32)
    ends = jnp.cumsum(counts)
    starts = ends - counts
    first_blk = starts // blk
    n_be = jnp.where(ends > starts, (ends - 1) // blk - first_blk + 1, 0)
    item_end = jnp.cumsum(n_be)
    item_off = item_end - n_be
    total = item_end[-1]
    i = jnp.arange(n_items, dtype=jnp.int32)
    iexp = jnp.minimum(jnp.sum((item_end[None, :] <= i[:, None]).astype(jnp.int32), axis=1), N_EXPERTS - 1)
    pick = (iexp[:, None] == experts[None, :]).astype(jnp.int32)
    base = jnp.sum(pick * (first_blk - item_off)[None, :], axis=1)
    iblk = jnp.where(i < total, base + i, n_blocks - 1).astype(jnp.int32)
    prev_blk = jnp.concatenate([jnp.full((1,), -1, jnp.int32), iblk[:-1]])
    next_blk = jnp.concatenate([iblk[1:], jnp.full((1,), -1, jnp.int32)])
    is_first = iblk != prev_blk
    is_last = (iblk != next_blk) | (i == total - 1)
    new_expert = iexp != jnp.concatenate([jnp.full((1,), -1, jnp.int32), iexp[:-1]])
    iflag = is_first.astype(jnp.int32) + 2 * is_last.astype(jnp.int32) + 4 * new_expert.astype(jnp.int32)
    return (iexp, iblk, iflag, total.astype(jnp.int32).reshape(1), tok.reshape(n_blocks, 1, blk),
            row_dst.reshape(n_blocks, 1, blk), e_sorted.reshape(n_blocks, 1, blk))


def _combine_kernel(xs_ref, mod_ref, w_ref, *refs, d):
    y_refs, o_ref = refs[:TOP_K], refs[TOP_K]
    w = w_ref[0]
    tn = w.shape[0]
    yparts = d // 256
    wk = [jnp.broadcast_to(w[:, k:k + 1], (tn, 128)) for k in range(TOP_K)]
    for p in range(yparts):
        lo = jnp.zeros((tn, 128), F32)
        hi = jnp.zeros((tn, 128), F32)
        for k in range(TOP_K):
            word = y_refs[k][pl.ds(p, tn, stride=yparts), :]
            lo = lo + lax.bitcast_convert_type(word << 16, F32) * wk[k]
            hi = hi + lax.bitcast_convert_type(word & jnp.uint32(0xFFFF0000), F32) * wk[k]
        for q, tot in ((p, lo), (yparts + p, hi)):
            sl = slice(q * 128, (q + 1) * 128)
            o_ref[0, :, sl] = xs_ref[0, :, sl] + mod_ref[0][:, 5 * d + q * 128:5 * d + (q + 1) * 128] * tot


def _combine(xs, mod, w, y, n_ctx_tiles):
    b, n, d = xs.shape
    tn = TOK_TILE
    nb = b
    tiles = n // tn
    per_slot = b * tiles

    def mod_idx(bi, t):
        return (jnp.where(t < n_ctx_tiles, nb, bi), 0, 0)

    tok = lambda bi, t: (bi, t, 0)
    y_specs = [pl.BlockSpec((tn * (d // 256), 128),
                            functools.partial(lambda bi, t, k: (k * per_slot + bi * tiles + t, 0), k=k))
               for k in range(TOP_K)]
    return pl.pallas_call(
        functools.partial(_combine_kernel, d=d),
        grid=(b, tiles),
        in_specs=[pl.BlockSpec((1, tn, d), tok), pl.BlockSpec((1, 1, N_MOD * d), mod_idx),
                  pl.BlockSpec((1, tn, TOP_K), tok)] + y_specs,
        out_specs=pl.BlockSpec((1, tn, d), tok),
        out_shape=jax.ShapeDtypeStruct((b, n, d), F32),
        compiler_params=_cparams(("parallel", "parallel")),
        name="combine",
    )(xs, mod, w, *([y] * TOP_K))


def _final_kernel(x_ref, g_ref, o_ref):
    o_ref[0] = _rms_rows(x_ref[0], g_ref[...])


def _final_norm(x, g, n_ctx_tiles):
    b, n, d = x.shape
    tn = TOK_TILE
    n_lat_tiles = n // tn - n_ctx_tiles
    return pl.pallas_call(
        _final_kernel,
        grid=(b, n_lat_tiles),
        in_specs=[pl.BlockSpec((1, tn, d), lambda bi, t: (bi, t + n_ctx_tiles, 0)),
                  pl.BlockSpec((1, d), lambda bi, t: (0, 0))],
        out_specs=pl.BlockSpec((1, tn, d), lambda bi, t: (bi, t, 0)),
        out_shape=jax.ShapeDtypeStruct((b, n_lat_tiles * tn, d), F32),
        compiler_params=_cparams(("parallel", "parallel")),
        name="final_norm",
    )(x, g)


def _regroup_qkv_columns(w):
    a_q, a_kv = A_HEADS * HEAD_DIM, A_KV * HEAD_DIM
    b_q, b_kv = B_HEADS * HEAD_DIM, B_KV * HEAD_DIM
    c_qk, c_v = 2 * C_HEADS * HEAD_DIM, C_HEADS * 2 * HEAD_DIM
    off = np.concatenate([[0], np.cumsum((a_q, a_kv, a_kv, b_q, b_kv, b_kv, c_qk, c_qk, c_v))])
    seg = lambda i: w[:, int(off[i]):int(off[i + 1])]
    return jnp.concatenate([seg(0), seg(3), seg(6), seg(1), seg(4), seg(7), seg(2), seg(5), seg(8)], axis=1)


def _rotary_partner(row):
    blocks = row.reshape(-1, 2, ROPE_FREQS)
    return blocks[:, ::-1, :].reshape(-1)


def _rope_tables(n_ctx, n_lat):
    t = jnp.arange(n_lat, dtype=jnp.int32)
    row_pos = (t // GRID_W).astype(F32)
    col_pos = (t % GRID_W).astype(F32)
    inv_freq = jnp.power(ROPE_THETA, -jnp.arange(ROPE_FREQS, dtype=F32) / ROPE_FREQS)
    ang_r = row_pos[:, None] * inv_freq
    ang_c = col_pos[:, None] * inv_freq
    cos64 = jnp.concatenate([jnp.cos(ang_r), jnp.cos(ang_r), jnp.cos(ang_c), jnp.cos(ang_c)], axis=1)
    sin64 = jnp.concatenate([-jnp.sin(ang_r), jnp.sin(ang_r), -jnp.sin(ang_c), jnp.sin(ang_c)], axis=1)
    cos64 = jnp.concatenate([jnp.ones((n_ctx, HEAD_DIM), F32), cos64], axis=0)
    sin64 = jnp.concatenate([jnp.zeros((n_ctx, HEAD_DIM), F32), sin64], axis=0)
    return jnp.tile(cos64, (1, 2)), jnp.tile(sin64, (1, 2))


def kernel(x, c, ctx, c_ctx, w_mod, b_mod, g_norm1, w_qkv, g_qnorm_a, g_knorm_a, sink_b, lam_q1, lam_k1, lam_q2, lam_k2, g_subln_c, w_br_a, w_br_b, w_br_c, w_gate, b_gate, w_out, g_norm2, w_router, e_bias, w1, w3, w2, ws1, ws3, ws2, g_final):
    bsz, n_lat, d = x.shape
    n_ctx = ctx.shape[1]
    depth = w_mod.shape[0]
    n_tok = n_ctx + n_lat
    assert n_ctx % TOK_TILE == 0 and n_lat % KEY_CHUNK == 0 and n_ctx % 128 == 0
    n_ctx_tiles = n_ctx // TOK_TILE
    n_all = bsz * n_tok

    rows = -(-(bsz + 1) // 8) * 8
    cvec = jnp.concatenate([c, c_ctx[None, :], jnp.zeros((rows - bsz - 1, d), F32)], axis=0)
    mod_all = _modulation(cvec, w_mod, b_mod)

    cos_t, sin_t = _rope_tables(n_ctx, n_lat)
    head_of = np.arange(R_A) // HEAD_DIM
    ones_blk = jnp.asarray((head_of[:, None] == head_of[None, :]).astype(np.float32), BF16)
    scale = HEAD_DIM ** -0.5 * LOG2E
    unit = jnp.ones((HEAD_DIM,), F32)

    xs = jnp.concatenate([ctx, x], axis=1)
    for l in range(depth):
        lam_init = 0.8 - 0.6 * math.exp(-0.3 * l)
        lam = (jnp.exp(jnp.dot(lam_q1[l], lam_k1[l])) - jnp.exp(jnp.dot(lam_q2[l], lam_k2[l]))).astype(F32) + lam_init
        mod = mod_all[l].reshape(rows, 1, N_MOD * d)
        g1 = g_norm1[l].reshape(1, d)

        wqkv = _regroup_qkv_columns(w_qkv[l]).astype(BF16)
        grow = jnp.concatenate([jnp.tile(g_qnorm_a[l] * scale, A_HEADS), jnp.tile(unit * scale, B_HEADS),
                                jnp.tile(unit * scale, 2 * C_HEADS), jnp.tile(g_knorm_a[l], A_KV),
                                jnp.tile(unit, B_KV), jnp.tile(unit, 2 * C_HEADS)])
        gsrow = _rotary_partner(grow)
        q_all, kt_all, vx_ab, vx_c = _pre_attention(xs, mod, g1, wqkv, ones_blk, grow.reshape(1, R_W),
                                                    gsrow.reshape(1, R_W), cos_t, sin_t, n_ctx_tiles)

        scal = jnp.concatenate([sink_b[l].astype(F32) * LOG2E, lam.reshape(1), jnp.full((1,), 1.0 - lam_init, F32)])
        scal_c = jnp.concatenate([jnp.zeros((DIFF_HEADS_PER_STEP,), F32), lam.reshape(1),
                                  jnp.full((1,), 1.0 - lam_init, F32)])
        gsub = g_subln_c[l].reshape(1, 2 * HEAD_DIM)
        g64 = jnp.ones((1, HEAD_DIM), F32)
        ga, gb = A_HEADS // A_KV, B_HEADS // B_KV
        ya = _attention(scal, g64, q_all, kt_all, vx_ab, heads=ga, hpv=ga, hpk=ga, mode="global",
                        n_ctx=n_ctx, units=A_KV, q_unit0=0, k_unit0=0, v_unit0=0)
        yb = _attention(scal, g64, q_all, kt_all, vx_ab, heads=B_HEADS, hpv=gb, hpk=gb, mode="window",
                        n_ctx=n_ctx, units=1, q_unit0=QA_W // QB_W, k_unit0=A_KV // B_KV, v_unit0=A_KV // B_KV)
        yc = _attention(scal_c, gsub, q_all, kt_all, vx_c, heads=DIFF_HEADS_PER_STEP, hpv=2, hpk=1,
                        mode="diff", n_ctx=n_ctx, units=2 * C_HEADS // DIFF_HEADS_PER_STEP,
                        q_unit0=(QA_W + QB_W) // 256, k_unit0=(A_KV + B_KV) // DIFF_HEADS_PER_STEP, v_unit0=0)

        x1 = _merge(xs, mod, g1, ya, yb, yc,
                    w_gate[l].astype(BF16), b_gate[l].reshape(1, -1), w_br_a[l].astype(BF16),
                    w_br_b[l].astype(BF16), w_br_c[l].astype(BF16), w_out[l].astype(BF16), n_ctx_tiles)

        f, idx_t, w_t, x_sh, tile_counts = _route(x1, mod, g_norm2[l].reshape(1, d), w_router[l].T,
                                     e_bias[l].reshape(N_EXPERTS, 1), ws1[l].astype(BF16),
                                     ws3[l].astype(BF16), ws2[l].astype(BF16), n_ctx_tiles)

        idx = idx_t.transpose(0, 2, 1).reshape(n_all, TOP_K)
        counts = jnp.sum(tile_counts, axis=(0, 1)).reshape(N_EXPERTS)
        iexp, iblk, iflag, nitems, row_tok, row_dst, erow = _routing_tables(idx, counts, n_all, d // 128)
        y = _experts(iexp, iblk, iflag, nitems, row_tok, row_dst, erow, f.reshape(n_all * (d // 128), 128),
                     w1, w3, w2, l)
        xs = _combine(x_sh, mod, w_t.transpose(0, 2, 1), y, n_ctx_tiles)

    return _final_norm(xs, g_final.reshape(1, d), n_ctx_tiles)
```

```python
import functools
import math

import numpy as np
import jax
import jax.numpy as jnp
from jax import lax
from jax.experimental import pallas as pl
from jax.experimental.pallas import tpu as pltpu

F32 = jnp.float32
BF16 = jnp.bfloat16

HEAD_DIM = 64
ROPE_FREQS = HEAD_DIM // 4
ROPE_THETA = 10000.0
GRID_W = 64
WINDOW = 128
A_HEADS, A_KV = 8, 2
B_HEADS, B_KV = 8, 2
C_HEADS = 4
N_EXPERTS = 128
TOP_K = 8
N_GROUPS = 8
TOPK_GROUPS = 4
GROUP_SIZE = N_EXPERTS // N_GROUPS
ROUTE_SCALE = 2.5
EXPERT_BLOCK = 256
EXPERT_RING = 3
N_MOD = 6
EPS = 1e-6
NEG = -1e30

QA_W, QB_W, QC_W = A_HEADS * HEAD_DIM, B_HEADS * HEAD_DIM, 2 * C_HEADS * HEAD_DIM
KA_W, KB_W, KC_W = A_KV * HEAD_DIM, B_KV * HEAD_DIM, 2 * C_HEADS * HEAD_DIM
Q_W = QA_W + QB_W + QC_W
K_W = KA_W + KB_W + KC_W
R_W = Q_W + K_W
R_A = QA_W + KA_W
V_W = A_KV * HEAD_DIM + B_KV * HEAD_DIM + C_HEADS * 2 * HEAD_DIM
K_HEADS = K_W // HEAD_DIM

TOK_TILE = 256
ATTN_TQ = 256
DIFF_HEADS_PER_STEP = 4
KEY_CHUNK = 512
MAX_KEY_CHUNK = {"global": 8448, "diff": 8448, "window": 128}
LOG2E = math.log2(math.e)
VMEM_LIMIT = 56 * 1024 * 1024


def _cparams(sem, **kw):
    return pltpu.CompilerParams(dimension_semantics=sem, vmem_limit_bytes=VMEM_LIMIT, **kw)


def _dot(a, b):
    return jnp.dot(a, b, preferred_element_type=F32)


def _sigmoid(x):
    return 1.0 / (1.0 + jnp.exp(-x))


def _silu(x):
    return x * _sigmoid(x)


def _rms_rows(x, g):
    return x * lax.rsqrt(jnp.mean(x * x, axis=-1, keepdims=True) + EPS) * g


def _mod_kernel(c_ref, w_ref, b_ref, o_ref):
    cs = _silu(c_ref[...])
    o_ref[0] = jnp.dot(cs, w_ref[0], preferred_element_type=F32,
                       precision=lax.Precision.HIGHEST) + b_ref[0]


def _modulation(cvec, w_mod, b_mod):
    depth, d, n = w_mod.shape
    rows = cvec.shape[0]
    bn = 1536
    return pl.pallas_call(
        _mod_kernel,
        grid=(depth, n // bn),
        in_specs=[
            pl.BlockSpec((rows, d), lambda l, j: (0, 0)),
            pl.BlockSpec((1, d, bn), lambda l, j: (l, 0, j)),
            pl.BlockSpec((1, 1, bn), lambda l, j: (l, 0, j)),
        ],
        out_specs=pl.BlockSpec((1, rows, bn), lambda l, j: (l, 0, j)),
        out_shape=jax.ShapeDtypeStruct((depth, rows, n), F32),
        compiler_params=_cparams(("parallel", "parallel")),
        name="modulation",
    )(cvec, w_mod, b_mod.reshape(depth, 1, n))


def _pre_kernel(x_ref, mod_ref, g1_ref, wqkv_ref, ones_ref, grow_ref, gsrow_ref,
                cos_ref, sin_ref, q_out, kt_out, vab_out, vc_out, *, d):
    x = x_ref[0]
    mod = mod_ref[0]
    tn = x.shape[0]
    a = (_rms_rows(x, g1_ref[...]) * (1.0 + mod[:, d:2 * d]) + mod[:, 0:d]).astype(BF16)
    p = _dot(a, wqkv_ref[...])
    pa = jnp.concatenate([p[:, 0:QA_W], p[:, Q_W:Q_W + KA_W]], axis=1)
    sq = pa * pa
    hi = sq.astype(BF16)
    lo = (sq - hi.astype(F32)).astype(BF16)
    ssq = _dot(hi, ones_ref[...]) + _dot(lo, ones_ref[...])
    rinv = lax.rsqrt(ssq * (1.0 / HEAD_DIM) + EPS)
    cos = cos_ref[...]
    sin = sin_ref[...]
    lane = lax.broadcasted_iota(jnp.int32, (tn, 128), 1)
    low_half = (lane % (2 * ROPE_FREQS)) < ROPE_FREQS
    first_head = lane < HEAD_DIM
    for j in range(R_W // 128):
        sl = slice(j * 128, (j + 1) * 128)
        pj = p[:, sl]
        ps = jnp.where(low_half, pltpu.roll(pj, 128 - ROPE_FREQS, 1), pltpu.roll(pj, ROPE_FREQS, 1))
        o = pj * (grow_ref[:, sl] * cos) + ps * (gsrow_ref[:, sl] * sin)
        if (j + 1) * 128 <= QA_W:
            o = o * rinv[:, sl]
        elif j * 128 == Q_W:
            o = o * rinv[:, QA_W:QA_W + KA_W]
        if j * 128 < Q_W:
            q_out[0, :, sl] = o.astype(BF16)
        else:
            ot = jnp.transpose(o).astype(BF16)
            kh = (j * 128 - Q_W) // HEAD_DIM
            kt_out[0, kh] = ot[0:HEAD_DIM]
            kt_out[0, kh + 1] = ot[HEAD_DIM:2 * HEAD_DIM]
    ones_at_64 = (lane == HEAD_DIM).astype(F32)
    for j in range(2):
        vj = p[:, R_W + j * 128:R_W + (j + 1) * 128]
        vab_out[0, 2 * j] = jnp.where(first_head, vj, ones_at_64).astype(BF16)
        vab_out[0, 2 * j + 1] = jnp.where(first_head, pltpu.roll(vj, HEAD_DIM, 1), ones_at_64).astype(BF16)
    ones_at_0 = (lane == 0).astype(BF16)
    for j in range(C_HEADS):
        vj = p[:, R_W + (2 + j) * 128:R_W + (3 + j) * 128]
        vc_out[0, j] = jnp.concatenate([vj.astype(BF16), ones_at_0], axis=1)


def _pre_attention(x, mod, g1, wqkv, ones_blk, grow, gsrow, cos_t, sin_t, n_ctx_tiles):
    b, n, d = x.shape
    tn = TOK_TILE
    nb = b

    def mod_idx(bi, t):
        return (jnp.where(t < n_ctx_tiles, nb, bi), 0, 0)

    return pl.pallas_call(
        functools.partial(_pre_kernel, d=d),
        grid=(b, n // tn),
        in_specs=[
            pl.BlockSpec((1, tn, d), lambda bi, t: (bi, t, 0)),
            pl.BlockSpec((1, 1, N_MOD * d), mod_idx),
            pl.BlockSpec((1, d), lambda bi, t: (0, 0)),
            pl.BlockSpec((d, R_W + V_W), lambda bi, t: (0, 0)),
            pl.BlockSpec((R_A, R_A), lambda bi, t: (0, 0)),
            pl.BlockSpec((1, R_W), lambda bi, t: (0, 0)),
            pl.BlockSpec((1, R_W), lambda bi, t: (0, 0)),
            pl.BlockSpec((tn, 128), lambda bi, t: (t, 0)),
            pl.BlockSpec((tn, 128), lambda bi, t: (t, 0)),
        ],
        out_specs=[
            pl.BlockSpec((1, tn, Q_W), lambda bi, t: (bi, t, 0)),
            pl.BlockSpec((1, K_HEADS, HEAD_DIM, tn), lambda bi, t: (bi, 0, 0, t)),
            pl.BlockSpec((1, A_KV + B_KV, tn, 128), lambda bi, t: (bi, 0, t, 0)),
            pl.BlockSpec((1, C_HEADS, tn, 256), lambda bi, t: (bi, 0, t, 0)),
        ],
        out_shape=[
            jax.ShapeDtypeStruct((b, n, Q_W), BF16),
            jax.ShapeDtypeStruct((b, K_HEADS, HEAD_DIM, n), BF16),
            jax.ShapeDtypeStruct((b, A_KV + B_KV, n, 128), BF16),
            jax.ShapeDtypeStruct((b, C_HEADS, n, 256), BF16),
        ],
        compiler_params=_cparams(("parallel", "parallel")),
        name="pre_attention",
    )(x, mod, g1, wqkv, ones_blk, grow, gsrow, cos_t, sin_t)


def _attn_kernel(sc_ref, gsub_ref, q_ref, kt_ref, v_ref, o_ref, m_sc, acc_sc, *, heads, hpv,
                 hpk, mode, n_ctx, n_tok, tq, ck, dv):
    u = pl.program_id(1)
    qi = pl.program_id(2)
    is_lat = qi >= n_ctx // tq
    dvx = acc_sc.shape[-1]

    for g in range(heads):
        if mode == "window":
            m_sc[g] = jnp.full((tq, 1), sc_ref[u * heads + g], F32)
            lane = lax.broadcasted_iota(jnp.int32, (tq, dvx), 1)
            acc_sc[g] = jnp.where(lane == dv, 1.0, 0.0).astype(F32)
        else:
            m_sc[g] = jnp.full((tq, 1), NEG, F32)
            acc_sc[g] = jnp.zeros((tq, dvx), F32)

    def probs(g, s, mask):
        if mask is not None:
            s = jnp.where(mask, s, NEG)
        m = m_sc[g]
        m_new = jnp.maximum(m, jnp.max(s, axis=-1, keepdims=True))
        m_sc[g] = m_new
        return jnp.exp2(s - m_new).astype(BF16), jnp.exp2(m - m_new)

    def q_of(g):
        return q_ref[0, :, g * HEAD_DIM:(g + 1) * HEAD_DIM]

    def update_all(kt_of, v_of, mask=None):
        s_next = _dot(q_of(0), kt_of(0))
        for g in range(heads):
            s = s_next
            if g + 1 < heads:
                s_next = _dot(q_of(g + 1), kt_of(g + 1))
            p, alpha = probs(g, s, mask)
            acc_sc[g] = alpha * acc_sc[g] + _dot(p, v_of(g))

    def step(k0, size):
        update_all(lambda g: kt_ref[0, g // hpk, :, pl.ds(k0, size)],
                   lambda g: v_ref[0, g // hpv, pl.ds(k0, size), :])

    if mode == "window":
        span = tq + 2 * WINDOW
        start = pl.multiple_of(jnp.clip(qi * tq - WINDOW, 0, n_tok - span), 128)
        qpos = qi * tq + lax.broadcasted_iota(jnp.int32, (tq, n_ctx + span), 0)
        col = lax.broadcasted_iota(jnp.int32, (tq, n_ctx + span), 1)
        kpos = start + col - n_ctx
        mask = (col < n_ctx) | (is_lat & (kpos >= n_ctx) & (jnp.abs(kpos - qpos) <= WINDOW))

        def v_of(g):
            gv = g // hpv
            return jnp.concatenate([v_ref[0, gv, 0:n_ctx, :], v_ref[0, gv, pl.ds(start, span), :]], axis=0)

        def kt_of(g):
            gk = g // hpk
            return jnp.concatenate([kt_ref[0, gk, :, 0:n_ctx], kt_ref[0, gk, :, pl.ds(start, span)]], axis=1)

        update_all(kt_of, v_of, mask)
    else:
        @pl.when(jnp.logical_not(is_lat))
        def _():
            step(0, n_ctx)

        def body(ci, carry):
            step(pl.multiple_of(ci * ck, 128), ck)
            return carry

        lax.fori_loop(0, jnp.where(is_lat, n_tok // ck, 0), body, 0)

    def result(g):
        acc = acc_sc[g]
        return acc[:, 0:128] / acc[:, dv:dv + 1]

    if mode == "diff":
        lam = sc_ref[heads]
        post = sc_ref[heads + 1]
        for j in range(heads // 2):
            y = result(2 * j) - lam * result(2 * j + 1)
            o_ref[0, :, j * dv:(j + 1) * dv] = (_rms_rows(y, gsub_ref[...]) * post).astype(o_ref.dtype)
    else:
        lane = lax.broadcasted_iota(jnp.int32, (tq, 128), 1)
        for j in range(heads // 2):
            pair = jnp.where(lane < dv, result(2 * j), pltpu.roll(result(2 * j + 1), dv, 1))
            o_ref[0, :, j * 128:(j + 1) * 128] = pair.astype(o_ref.dtype)


def _attention(scalars, gsub, q, kt, vx, *, heads, hpv, hpk, mode, n_ctx, units, q_unit0, k_unit0,
               v_unit0):
    b, n, _ = q.shape
    hd = HEAD_DIM
    dvx = vx.shape[-1]
    dv = gsub.shape[-1]
    assert dv < dvx
    tq = ATTN_TQ
    gk = heads // hpk
    vh = heads // hpv
    out_w = heads * hd if mode != "diff" else (heads // 2) * dv
    ck = max(c for c in range(128, MAX_KEY_CHUNK[mode] + 1, 128) if n % c == 0)
    kern = functools.partial(_attn_kernel, heads=heads, hpv=hpv, hpk=hpk, mode=mode,
                             n_ctx=n_ctx, n_tok=n, tq=tq, ck=ck, dv=dv)
    return pl.pallas_call(
        kern,
        grid=(b, units, n // tq),
        in_specs=[
            pl.BlockSpec(memory_space=pltpu.SMEM),
            pl.BlockSpec((1, dv), lambda bi, u, t: (0, 0)),
            pl.BlockSpec((1, tq, heads * hd), lambda bi, u, t: (bi, t, q_unit0 + u)),
            pl.BlockSpec((1, gk, hd, n), lambda bi, u, t: (bi, k_unit0 + u, 0, 0)),
            pl.BlockSpec((1, vh, n, dvx), lambda bi, u, t: (bi, v_unit0 + u, 0, 0)),
        ],
        out_specs=pl.BlockSpec((1, tq, out_w), lambda bi, u, t: (bi, t, u)),
        out_shape=jax.ShapeDtypeStruct((b, n, units * out_w), BF16),
        scratch_shapes=[
            pltpu.VMEM((heads, tq, 1), F32),
            pltpu.VMEM((heads, tq, dvx), F32),
        ],
        compiler_params=_cparams(("parallel", "parallel", "arbitrary")),
        name="attn_" + mode,
    )(scalars, gsub, q, kt, vx)


def _merge_kernel(x_ref, mod_ref, g1_ref, ya_ref, yb_ref, yc_ref, wg_ref, bg_ref, wa_ref, wb_ref,
                  wc_ref, wo_ref, o_ref, *, d):
    x = x_ref[0]
    mod = mod_ref[0]
    a = (_rms_rows(x, g1_ref[...]) * (1.0 + mod[:, d:2 * d]) + mod[:, 0:d]).astype(BF16)
    gate = _sigmoid(_dot(a, wg_ref[...]) + bg_ref[...])
    m = (gate[:, 0:d] * _dot(ya_ref[0], wa_ref[...])
         + gate[:, d:2 * d] * _dot(yb_ref[0], wb_ref[...])
         + gate[:, 2 * d:3 * d] * _dot(yc_ref[0], wc_ref[...]))
    mix = _dot(m.astype(BF16), wo_ref[...])
    o_ref[0] = x + mod[:, 2 * d:3 * d] * mix


def _merge(x, mod, g1, ya, yb, yc, wg, bg, wa, wb, wc, wo, n_ctx_tiles):
    b, n, d = x.shape
    tn = TOK_TILE
    nb = b
    yw = ya.shape[-1]

    def mod_idx(bi, t):
        return (jnp.where(t < n_ctx_tiles, nb, bi), 0, 0)

    tok = lambda bi, t: (bi, t, 0)
    const = lambda bi, t: (0, 0)
    return pl.pallas_call(
        functools.partial(_merge_kernel, d=d),
        grid=(b, n // tn),
        in_specs=[
            pl.BlockSpec((1, tn, d), tok),
            pl.BlockSpec((1, 1, N_MOD * d), mod_idx),
            pl.BlockSpec((1, d), const),
            pl.BlockSpec((1, tn, yw), tok),
            pl.BlockSpec((1, tn, yw), tok),
            pl.BlockSpec((1, tn, yw), tok),
            pl.BlockSpec((d, 3 * d), const),
            pl.BlockSpec((1, 3 * d), const),
            pl.BlockSpec((yw, d), const),
            pl.BlockSpec((yw, d), const),
            pl.BlockSpec((yw, d), const),
            pl.BlockSpec((d, d), const),
        ],
        out_specs=pl.BlockSpec((1, tn, d), tok),
        out_shape=jax.ShapeDtypeStruct((b, n, d), F32),
        compiler_params=_cparams(("parallel", "parallel")),
        name="merge",
    )(x, mod, g1, ya, yb, yc, wg, bg, wa, wb, wc, wo)


def _first_index(hit, idx, big):
    return jnp.min(jnp.where(hit, idx, big), axis=0, keepdims=True)


def _route_kernel(x_ref, mod_ref, g2_ref, wrt_ref, eb_ref, ws1_ref, ws3_ref, ws2_ref,
                  f_ref, idx_ref, w_ref, xs_ref, cnt_ref, *, d):
    x = x_ref[0]
    mod = mod_ref[0]
    f = _rms_rows(x, g2_ref[...]) * (1.0 + mod[:, 4 * d:5 * d]) + mod[:, 3 * d:4 * d]
    tn = f.shape[0]
    for s in range(d // 128):
        f_ref[0, pl.ds(s, tn, stride=d // 128), :] = f[:, s * 128:(s + 1) * 128]
    logits = lax.dot_general(wrt_ref[...], f, (((1,), (1,)), ((), ())),
                             preferred_element_type=F32, precision=lax.Precision.HIGHEST)
    scores = _sigmoid(logits)
    choice = scores + eb_ref[...]
    eidx = lax.broadcasted_iota(jnp.int32, (N_EXPERTS, tn), 0)
    lidx = lax.broadcasted_iota(jnp.int32, (GROUP_SIZE, tn), 0)
    gscore = []
    for g in range(N_GROUPS):
        cg = choice[g * GROUP_SIZE:(g + 1) * GROUP_SIZE, :]
        m1 = jnp.max(cg, axis=0, keepdims=True)
        first = _first_index(cg == m1, lidx, GROUP_SIZE)
        m2 = jnp.max(jnp.where(lidx == first, NEG, cg), axis=0, keepdims=True)
        gscore.append(m1 + m2)
    gs = jnp.concatenate(gscore, axis=0)
    gidx = lax.broadcasted_iota(jnp.int32, (N_GROUPS, tn), 0)
    gsel = jnp.zeros((N_GROUPS, tn), jnp.bool_)
    for _ in range(TOPK_GROUPS):
        gm = jnp.max(gs, axis=0, keepdims=True)
        first = _first_index(gs == gm, gidx, N_GROUPS)
        hit = gidx == first
        gsel = gsel | hit
        gs = jnp.where(hit, NEG, gs)
    gself = gsel.astype(F32)
    emask = jnp.concatenate(
        [jnp.broadcast_to(gself[g:g + 1, :], (GROUP_SIZE, tn)) for g in range(N_GROUPS)], axis=0)
    cur = jnp.where(emask > 0.5, choice, NEG)
    ids, ws = [], []
    chosen = jnp.zeros((N_EXPERTS, tn), F32)
    for _ in range(TOP_K):
        m = jnp.max(cur, axis=0, keepdims=True)
        first = _first_index(cur == m, eidx, N_EXPERTS)
        hit = eidx == first
        ids.append(first)
        ws.append(jnp.sum(jnp.where(hit, scores, 0.0), axis=0, keepdims=True))
        cur = jnp.where(hit, NEG, cur)
        chosen = chosen + hit.astype(F32)
    wsel = jnp.concatenate(ws, axis=0)
    idx_ref[0] = jnp.concatenate(ids, axis=0)
    cnt_ref[0, 0] = jnp.sum(chosen, axis=1, keepdims=True).astype(jnp.int32)
    w_ref[0] = wsel / jnp.sum(wsel, axis=0, keepdims=True) * ROUTE_SCALE
    fb = f.astype(BF16)
    h = _silu(_dot(fb, ws1_ref[...])) * _dot(fb, ws3_ref[...])
    xs_ref[0] = x + mod[:, 5 * d:6 * d] * _dot(h.astype(BF16), ws2_ref[...])


def _route(x, mod, g2, wrt, eb, ws1, ws3, ws2, n_ctx_tiles):
    b, n, d = x.shape
    tn = TOK_TILE
    nb = b
    ds = ws1.shape[-1]

    def mod_idx(bi, t):
        return (jnp.where(t < n_ctx_tiles, nb, bi), 0, 0)

    tok = lambda bi, t: (bi, t, 0)
    lane_tok = lambda bi, t: (bi, 0, t)
    const = lambda bi, t: (0, 0)
    return pl.pallas_call(
        functools.partial(_route_kernel, d=d),
        grid=(b, n // tn),
        in_specs=[
            pl.BlockSpec((1, tn, d), tok),
            pl.BlockSpec((1, 1, N_MOD * d), mod_idx),
            pl.BlockSpec((1, d), const),
            pl.BlockSpec((N_EXPERTS, d), const),
            pl.BlockSpec((N_EXPERTS, 1), const),
            pl.BlockSpec((d, ds), const),
            pl.BlockSpec((d, ds), const),
            pl.BlockSpec((ds, d), const),
        ],
        out_specs=[
            pl.BlockSpec((1, tn * (d // 128), 128), tok),
            pl.BlockSpec((1, TOP_K, tn), lane_tok),
            pl.BlockSpec((1, TOP_K, tn), lane_tok),
            pl.BlockSpec((1, tn, d), tok),
            pl.BlockSpec((1, 1, N_EXPERTS, 1), lambda bi, t: (bi, t, 0, 0)),
        ],
        out_shape=[
            jax.ShapeDtypeStruct((b, n * (d // 128), 128), F32),
            jax.ShapeDtypeStruct((b, TOP_K, n), jnp.int32),
            jax.ShapeDtypeStruct((b, TOP_K, n), F32),
            jax.ShapeDtypeStruct((b, n, d), F32),
            jax.ShapeDtypeStruct((b, n // tn, N_EXPERTS, 1), jnp.int32),
        ],
        compiler_params=_cparams(("parallel", "parallel")),
        name="route_shared",
    )(x, mod, g2, wrt, eb, ws1, ws3, ws2)


def _expert_kernel(iexp_ref, iblk_ref, iflag_ref, nitems_ref, tok_ref, tok_next_ref, tok_ahead_ref,
                   dst_prev_ref, dst_ref, erow_ref, f_hbm, w1_ref, w3_ref, w2_ref, y_hbm, xbuf, ybuf,
                   wb1, wb3, wb2, gsem, ssem):
    i = pl.program_id(0)
    nitems = nitems_ref[0]
    blk = iblk_ref[i]
    expert = iexp_ref[i]
    first = (iflag_ref[i] & 1) != 0
    last = (iflag_ref[i] & 2) != 0
    ring = xbuf.shape[0]
    slot = blk % ring
    slot_prev = (blk + ring - 1) % ring
    slot_prev2 = (blk + ring - 2) % ring
    slot_ahead = (blk + 2) % ring
    rows = EXPERT_BLOCK
    parts = xbuf.shape[1] // rows
    yparts = ybuf.shape[1] // rows
    n_blocks = y_hbm.shape[0] // (rows * yparts)

    def gather(idx_ref, s):
        for j in range(rows):
            src = pl.multiple_of(idx_ref[0, 0, j], parts)
            pltpu.make_async_copy(f_hbm.at[pl.ds(src, parts)], xbuf.at[s, pl.ds(j * parts, parts)],
                                  gsem.at[s]).start(priority=j % 2)

    def gather_wait(s):
        pltpu.make_async_copy(f_hbm.at[pl.ds(0, rows * parts)], xbuf.at[s], gsem.at[s]).wait()

    def scatter(idx_ref, s):
        for j in range(rows):
            dst = pl.multiple_of(idx_ref[0, 0, j], yparts)
            pltpu.make_async_copy(ybuf.at[s, pl.ds(j * yparts, yparts)], y_hbm.at[pl.ds(dst, yparts)],
                                  ssem.at[s]).start(priority=j % 2)

    def scatter_wait(s):
        pltpu.make_async_copy(ybuf.at[s], y_hbm.at[pl.ds(0, rows * yparts)], ssem.at[s]).wait()

    def expert_rows(s, after_up=None, after_down=None):
        xb = jnp.concatenate([xbuf[s, pl.ds(p, rows, stride=parts), :] for p in range(parts)],
                             axis=1).astype(BF16)
        h1 = _dot(xb, wb1[...])
        h3 = _dot(xb, wb3[...])
        if after_up is not None:
            after_up()
        y = _dot((_silu(h1) * h3).astype(BF16), wb2[...])
        if after_down is not None:
            after_down()
        return y

    def bf16_bits(v):
        return lax.bitcast_convert_type(v.astype(BF16).astype(F32), jnp.uint32)

    def store_rows(s, y, own_rows=None):
        for p in range(yparts):
            lo = bf16_bits(y[:, p * 128:(p + 1) * 128]) >> 16
            hi = bf16_bits(y[:, (yparts + p) * 128:(yparts + p + 1) * 128]) & jnp.uint32(0xFFFF0000)
            part = hi | lo
            if own_rows is not None:
                part = jnp.where(own_rows, part, ybuf[s, pl.ds(p, rows, stride=yparts), :])
            ybuf[s, pl.ds(p, rows, stride=yparts), :] = part

    valid = i < nitems

    @pl.when(valid & ((iflag_ref[i] & 4) != 0))
    def _():
        wb1[...] = w1_ref[0, 0].astype(BF16)
        wb3[...] = w3_ref[0, 0].astype(BF16)
        wb2[...] = w2_ref[0, 0].astype(BF16)

    fast = valid & first & last & (blk >= 3) & (blk <= n_blocks - 3)

    @pl.when(fast)
    def _():
        gather_wait(slot)
        scatter_wait(slot)
        y = expert_rows(slot, after_up=lambda: gather(tok_ahead_ref, slot_ahead),
                        after_down=lambda: scatter(dst_prev_ref, slot_prev))
        store_rows(slot, y)

    @pl.when(valid & jnp.logical_not(fast))
    def _():
        @pl.when(first)
        def _():
            @pl.when(blk == 0)
            def _():
                gather(tok_ref, 0)
                if n_blocks > 1:
                    gather(tok_next_ref, 1)

            @pl.when(blk + 2 < n_blocks)
            def _():
                gather(tok_ahead_ref, slot_ahead)

            @pl.when(blk >= 1)
            def _():
                scatter(dst_prev_ref, slot_prev)

            gather_wait(slot)

            @pl.when(blk >= 3)
            def _():
                scatter_wait(slot)

        y = expert_rows(slot)

        @pl.when(first)
        def _():
            store_rows(slot, y)

        @pl.when(jnp.logical_not(first))
        def _():
            ours = (erow_ref[0] == expert) & (lax.broadcasted_iota(jnp.int32, (rows, rows), 0)
                                             == lax.broadcasted_iota(jnp.int32, (rows, rows), 1))
            store_rows(slot, y, own_rows=jnp.sum(ours.astype(F32), axis=1, keepdims=True) > 0.5)

        @pl.when(last & (blk == n_blocks - 1))
        def _():
            scatter(dst_ref, slot)

            @pl.when(blk >= 2)
            def _():
                scatter_wait(slot_prev2)

            @pl.when(blk >= 1)
            def _():
                scatter_wait(slot_prev)

            scatter_wait(slot)


def _experts(iexp, iblk, iflag, nitems, row_tok, row_dst, erow, f_rows, w1, w3, w2, layer):
    n_items = iexp.shape[0]
    n_blocks = row_tok.shape[0]
    d, de = w1.shape[2], w1.shape[3]
    parts = d // 128
    yparts = parts // 2
    rows = EXPERT_BLOCK
    last = n_blocks - 1
    cur = lambda i, ie, ib, fl, nt: (ib[i], 0, 0)
    nxt = lambda i, ie, ib, fl, nt: (jnp.minimum(ib[i] + 1, last), 0, 0)
    ahd = lambda i, ie, ib, fl, nt: (jnp.minimum(ib[i] + 2, last), 0, 0)
    prv = lambda i, ie, ib, fl, nt: (jnp.maximum(ib[i] - 1, 0), 0, 0)
    wsel = lambda i, ie, ib, fl, nt: (layer, ie[i], 0, 0)
    grid_spec = pltpu.PrefetchScalarGridSpec(
        num_scalar_prefetch=4,
        grid=(n_items,),
        in_specs=[
            pl.BlockSpec((1, 1, rows), cur, memory_space=pltpu.SMEM),
            pl.BlockSpec((1, 1, rows), nxt, memory_space=pltpu.SMEM),
            pl.BlockSpec((1, 1, rows), ahd, memory_space=pltpu.SMEM),
            pl.BlockSpec((1, 1, rows), prv, memory_space=pltpu.SMEM),
            pl.BlockSpec((1, 1, rows), cur, memory_space=pltpu.SMEM),
            pl.BlockSpec((1, 1, rows), cur),
            pl.BlockSpec(memory_space=pl.ANY),
            pl.BlockSpec((1, 1, d, de), wsel),
            pl.BlockSpec((1, 1, d, de), wsel),
            pl.BlockSpec((1, 1, de, d), wsel),
        ],
        out_specs=pl.BlockSpec(memory_space=pl.ANY),
        scratch_shapes=[
            pltpu.VMEM((EXPERT_RING, rows * parts, 128), F32),
            pltpu.VMEM((EXPERT_RING, rows * yparts, 128), jnp.uint32),
            pltpu.VMEM((d, de), BF16),
            pltpu.VMEM((d, de), BF16),
            pltpu.VMEM((de, d), BF16),
            pltpu.SemaphoreType.DMA((EXPERT_RING,)),
            pltpu.SemaphoreType.DMA((EXPERT_RING,)),
        ],
    )
    return pl.pallas_call(
        _expert_kernel,
        grid_spec=grid_spec,
        out_shape=jax.ShapeDtypeStruct((n_blocks * rows * yparts, 128), jnp.uint32),
        compiler_params=_cparams(("arbitrary",)),
        name="experts",
    )(iexp, iblk, iflag, nitems, row_tok, row_tok, row_tok, row_dst, row_dst, erow, f_rows, w1, w3, w2)


ASSIGN_BITS = 20


def _routing_tables(idx, counts, n_tokens, parts):
    n_assign = n_tokens * TOP_K
    blk = EXPERT_BLOCK
    assert n_assign % blk == 0 and n_assign <= (1 << ASSIGN_BITS)
    n_blocks = n_assign // blk
    n_items = n_blocks + N_EXPERTS
    flat_e = idx.reshape(-1).astype(jnp.int32)
    key = jnp.sort((flat_e << ASSIGN_BITS) | jnp.arange(n_assign, dtype=jnp.int32), stable=False)
    e_sorted = key >> ASSIGN_BITS
    order = key & ((1 << ASSIGN_BITS) - 1)
    tok = order // TOP_K
    row_dst = ((order % TOP_K) * n_tokens + tok) * (parts // 2)
    tok = tok * parts

    experts = jnp.arange(N_EXPERTS, dtype=jnp.int32)
    ends = jnp.cumsum(counts)
    starts = ends - counts
    first_blk = starts // blk
    n_be = jnp.where(ends > starts, (ends - 1) // blk - first_blk + 1, 0)
    item_end = jnp.cumsum(n_be)
    item_off = item_end - n_be
    total = item_end[-1]
    i = jnp.arange(n_items, dtype=jnp.int32)
    iexp = jnp.minimum(jnp.sum((item_end[None, :] <= i[:, None]).astype(jnp.int32), axis=1), N_EXPERTS - 1)
    pick = (iexp[:, None] == experts[None, :]).astype(jnp.int32)
    base = jnp.sum(pick * (first_blk - item_off)[None, :], axis=1)
    iblk = jnp.where(i < total, base + i, n_blocks - 1).astype(jnp.int32)
    prev_blk = jnp.concatenate([jnp.full((1,), -1, jnp.int32), iblk[:-1]])
    next_blk = jnp.concatenate([iblk[1:], jnp.full((1,), -1, jnp.int32)])
    is_first = iblk != prev_blk
    is_last = (iblk != next_blk) | (i == total - 1)
    new_expert = iexp != jnp.concatenate([jnp.full((1,), -1, jnp.int32), iexp[:-1]])
    iflag = is_first.astype(jnp.int32) + 2 * is_last.astype(jnp.int32) + 4 * new_expert.astype(jnp.int32)
    return (iexp, iblk, iflag, total.astype(jnp.int32).reshape(1), tok.reshape(n_blocks, 1, blk),
            row_dst.reshape(n_blocks, 1, blk), e_sorted.reshape(n_blocks, 1, blk))


def _combine_kernel(xs_ref, mod_ref, w_ref, *refs, d):
    y_refs, o_ref = refs[:TOP_K], refs[TOP_K]
    w = w_ref[0]
    tn = w.shape[0]
    yparts = d // 256
    wk = [jnp.broadcast_to(w[:, k:k + 1], (tn, 128)) for k in range(TOP_K)]
    for p in range(yparts):
        lo = jnp.zeros((tn, 128), F32)
        hi = jnp.zeros((tn, 128), F32)
        for k in range(TOP_K):
            word = y_refs[k][pl.ds(p, tn, stride=yparts), :]
            lo = lo + lax.bitcast_convert_type(word << 16, F32) * wk[k]
            hi = hi + lax.bitcast_convert_type(word & jnp.uint32(0xFFFF0000), F32) * wk[k]
        for q, tot in ((p, lo), (yparts + p, hi)):
            sl = slice(q * 128, (q + 1) * 128)
            o_ref[0, :, sl] = xs_ref[0, :, sl] + mod_ref[0][:, 5 * d + q * 128:5 * d + (q + 1) * 128] * tot


def _combine(xs, mod, w, y, n_ctx_tiles):
    b, n, d = xs.shape
    tn = TOK_TILE
    nb = b
    tiles = n // tn
    per_slot = b * tiles

    def mod_idx(bi, t):
        return (jnp.where(t < n_ctx_tiles, nb, bi), 0, 0)

    tok = lambda bi, t: (bi, t, 0)
    y_specs = [pl.BlockSpec((tn * (d // 256), 128),
                            functools.partial(lambda bi, t, k: (k * per_slot + bi * tiles + t, 0), k=k))
               for k in range(TOP_K)]
    return pl.pallas_call(
        functools.partial(_combine_kernel, d=d),
        grid=(b, tiles),
        in_specs=[pl.BlockSpec((1, tn, d), tok), pl.BlockSpec((1, 1, N_MOD * d), mod_idx),
                  pl.BlockSpec((1, tn, TOP_K), tok)] + y_specs,
        out_specs=pl.BlockSpec((1, tn, d), tok),
        out_shape=jax.ShapeDtypeStruct((b, n, d), F32),
        compiler_params=_cparams(("parallel", "parallel")),
        name="combine",
    )(xs, mod, w, *([y] * TOP_K))


def _final_kernel(x_ref, g_ref, o_ref):
    o_ref[0] = _rms_rows(x_ref[0], g_ref[...])


def _final_norm(x, g, n_ctx_tiles):
    b, n, d = x.shape
    tn = TOK_TILE
    n_lat_tiles = n // tn - n_ctx_tiles
    return pl.pallas_call(
        _final_kernel,
        grid=(b, n_lat_tiles),
        in_specs=[pl.BlockSpec((1, tn, d), lambda bi, t: (bi, t + n_ctx_tiles, 0)),
                  pl.BlockSpec((1, d), lambda bi, t: (0, 0))],
        out_specs=pl.BlockSpec((1, tn, d), lambda bi, t: (bi, t, 0)),
        out_shape=jax.ShapeDtypeStruct((b, n_lat_tiles * tn, d), F32),
        compiler_params=_cparams(("parallel", "parallel")),
        name="final_norm",
    )(x, g)


def _regroup_qkv_columns(w):
    a_q, a_kv = A_HEADS * HEAD_DIM, A_KV * HEAD_DIM
    b_q, b_kv = B_HEADS * HEAD_DIM, B_KV * HEAD_DIM
    c_qk, c_v = 2 * C_HEADS * HEAD_DIM, C_HEADS * 2 * HEAD_DIM
    off = np.concatenate([[0], np.cumsum((a_q, a_kv, a_kv, b_q, b_kv, b_kv, c_qk, c_qk, c_v))])
    seg = lambda i: w[:, int(off[i]):int(off[i + 1])]
    return jnp.concatenate([seg(0), seg(3), seg(6), seg(1), seg(4), seg(7), seg(2), seg(5), seg(8)], axis=1)


def _rotary_partner(row):
    blocks = row.reshape(-1, 2, ROPE_FREQS)
    return blocks[:, ::-1, :].reshape(-1)


def _rope_tables(n_ctx, n_lat):
    t = jnp.arange(n_lat, dtype=jnp.int32)
    row_pos = (t // GRID_W).astype(F32)
    col_pos = (t % GRID_W).astype(F32)
    inv_freq = jnp.power(ROPE_THETA, -jnp.arange(ROPE_FREQS, dtype=F32) / ROPE_FREQS)
    ang_r = row_pos[:, None] * inv_freq
    ang_c = col_pos[:, None] * inv_freq
    cos64 = jnp.concatenate([jnp.cos(ang_r), jnp.cos(ang_r), jnp.cos(ang_c), jnp.cos(ang_c)], axis=1)
    sin64 = jnp.concatenate([-jnp.sin(ang_r), jnp.sin(ang_r), -jnp.sin(ang_c), jnp.sin(ang_c)], axis=1)
    cos64 = jnp.concatenate([jnp.ones((n_ctx, HEAD_DIM), F32), cos64], axis=0)
    sin64 = jnp.concatenate([jnp.zeros((n_ctx, HEAD_DIM), F32), sin64], axis=0)
    return jnp.tile(cos64, (1, 2)), jnp.tile(sin64, (1, 2))


def kernel(x, c, ctx, c_ctx, w_mod, b_mod, g_norm1, w_qkv, g_qnorm_a, g_knorm_a, sink_b, lam_q1, lam_k1, lam_q2, lam_k2, g_subln_c, w_br_a, w_br_b, w_br_c, w_gate, b_gate, w_out, g_norm2, w_router, e_bias, w1, w3, w2, ws1, ws3, ws2, g_final):
    bsz, n_lat, d = x.shape
    n_ctx = ctx.shape[1]
    depth = w_mod.shape[0]
    n_tok = n_ctx + n_lat
    assert n_ctx % TOK_TILE == 0 and n_lat % KEY_CHUNK == 0 and n_ctx % 128 == 0
    n_ctx_tiles = n_ctx // TOK_TILE
    n_all = bsz * n_tok

    rows = -(-(bsz + 1) // 8) * 8
    cvec = jnp.concatenate([c, c_ctx[None, :], jnp.zeros((rows - bsz - 1, d), F32)], axis=0)
    mod_all = _modulation(cvec, w_mod, b_mod)

    cos_t, sin_t = _rope_tables(n_ctx, n_lat)
    head_of = np.arange(R_A) // HEAD_DIM
    ones_blk = jnp.asarray((head_of[:, None] == head_of[None, :]).astype(np.float32), BF16)
    scale = HEAD_DIM ** -0.5 * LOG2E
    unit = jnp.ones((HEAD_DIM,), F32)

    xs = jnp.concatenate([ctx, x], axis=1)
    for l in range(depth):
        lam_init = 0.8 - 0.6 * math.exp(-0.3 * l)
        lam = (jnp.exp(jnp.dot(lam_q1[l], lam_k1[l])) - jnp.exp(jnp.dot(lam_q2[l], lam_k2[l]))).astype(F32) + lam_init
        mod = mod_all[l].reshape(rows, 1, N_MOD * d)
        g1 = g_norm1[l].reshape(1, d)

        wqkv = _regroup_qkv_columns(w_qkv[l]).astype(BF16)
        grow = jnp.concatenate([jnp.tile(g_qnorm_a[l] * scale, A_HEADS), jnp.tile(unit * scale, B_HEADS),
                                jnp.tile(unit * scale, 2 * C_HEADS), jnp.tile(g_knorm_a[l], A_KV),
                                jnp.tile(unit, B_KV), jnp.tile(unit, 2 * C_HEADS)])
        gsrow = _rotary_partner(grow)
        q_all, kt_all, vx_ab, vx_c = _pre_attention(xs, mod, g1, wqkv, ones_blk, grow.reshape(1, R_W),
                                                    gsrow.reshape(1, R_W), cos_t, sin_t, n_ctx_tiles)

        scal = jnp.concatenate([sink_b[l].astype(F32) * LOG2E, lam.reshape(1), jnp.full((1,), 1.0 - lam_init, F32)])
        scal_c = jnp.concatenate([jnp.zeros((DIFF_HEADS_PER_STEP,), F32), lam.reshape(1),
                                  jnp.full((1,), 1.0 - lam_init, F32)])
        gsub = g_subln_c[l].reshape(1, 2 * HEAD_DIM)
        g64 = jnp.ones((1, HEAD_DIM), F32)
        ga, gb = A_HEADS // A_KV, B_HEADS // B_KV
        ya = _attention(scal, g64, q_all, kt_all, vx_ab, heads=ga, hpv=ga, hpk=ga, mode="global",
                        n_ctx=n_ctx, units=A_KV, q_unit0=0, k_unit0=0, v_unit0=0)
        yb = _attention(scal, g64, q_all, kt_all, vx_ab, heads=B_HEADS, hpv=gb, hpk=gb, mode="window",
                        n_ctx=n_ctx, units=1, q_unit0=QA_W // QB_W, k_unit0=A_KV // B_KV, v_unit0=A_KV // B_KV)
        yc = _attention(scal_c, gsub, q_all, kt_all, vx_c, heads=DIFF_HEADS_PER_STEP, hpv=2, hpk=1,
                        mode="diff", n_ctx=n_ctx, units=2 * C_HEADS // DIFF_HEADS_PER_STEP,
                        q_unit0=(QA_W + QB_W) // 256, k_unit0=(A_KV + B_KV) // DIFF_HEADS_PER_STEP, v_unit0=0)

        x1 = _merge(xs, mod, g1, ya, yb, yc,
                    w_gate[l].astype(BF16), b_gate[l].reshape(1, -1), w_br_a[l].astype(BF16),
                    w_br_b[l].astype(BF16), w_br_c[l].astype(BF16), w_out[l].astype(BF16), n_ctx_tiles)

        f, idx_t, w_t, x_sh, tile_counts = _route(x1, mod, g_norm2[l].reshape(1, d), w_router[l].T,
                                     e_bias[l].reshape(N_EXPERTS, 1), ws1[l].astype(BF16),
                                     ws3[l].astype(BF16), ws2[l].astype(BF16), n_ctx_tiles)

        idx = idx_t.transpose(0, 2, 1).reshape(n_all, TOP_K)
        counts = jnp.sum(tile_counts, axis=(0, 1)).reshape(N_EXPERTS)
        iexp, iblk, iflag, nitems, row_tok, row_dst, erow = _routing_tables(idx, counts, n_all, d // 128)
        y = _experts(iexp, iblk, iflag, nitems, row_tok, row_dst, erow, f.reshape(n_all * (d // 128), 128),
                     w1, w3, w2, l)
        xs = _combine(x_sh, mod, w_t.transpose(0, 2, 1), y, n_ctx_tiles)

    return _final_norm(xs, g_final.reshape(1, d), n_ctx_tiles)
```

```python
import functools
import math

import numpy as np
import jax
import jax.numpy as jnp
from jax import lax
from jax.experimental import pallas as pl
from jax.experimental.pallas import tpu as pltpu

F32 = jnp.float32
BF16 = jnp.bfloat16

HEAD_DIM = 64
ROPE_FREQS = HEAD_DIM // 4
ROPE_THETA = 10000.0
GRID_W = 64
WINDOW = 128
A_HEADS, A_KV = 8, 2
B_HEADS, B_KV = 8, 2
C_HEADS = 4
N_EXPERTS = 128
TOP_K = 8
N_GROUPS = 8
TOPK_GROUPS = 4
GROUP_SIZE = N_EXPERTS // N_GROUPS
ROUTE_SCALE = 2.5
EXPERT_BLOCK = 512
EXPERT_RING = 3
N_MOD = 6
EPS = 1e-6
NEG = -1e30

QA_W, QB_W, QC_W = A_HEADS * HEAD_DIM, B_HEADS * HEAD_DIM, 2 * C_HEADS * HEAD_DIM
KA_W, KB_W, KC_W = A_KV * HEAD_DIM, B_KV * HEAD_DIM, 2 * C_HEADS * HEAD_DIM
Q_W = QA_W + QB_W + QC_W
K_W = KA_W + KB_W + KC_W
R_W = Q_W + K_W
R_A = QA_W + KA_W
V_W = A_KV * HEAD_DIM + B_KV * HEAD_DIM + C_HEADS * 2 * HEAD_DIM
K_HEADS = K_W // HEAD_DIM

TOK_TILE = 256
ATTN_TQ = 256
DIFF_HEADS_PER_STEP = 4
KEY_CHUNK = 512
MAX_KEY_CHUNK = {"global": 8448, "diff": 8448, "window": 128}
LOG2E = math.log2(math.e)
VMEM_LIMIT = 56 * 1024 * 1024


def _cparams(sem, **kw):
    return pltpu.CompilerParams(dimension_semantics=sem, vmem_limit_bytes=VMEM_LIMIT, **kw)


def _dot(a, b):
    return jnp.dot(a, b, preferred_element_type=F32)


def _sigmoid(x):
    return 1.0 / (1.0 + jnp.exp(-x))


def _silu(x):
    return x * _sigmoid(x)


def _rms_rows(x, g):
    return x * lax.rsqrt(jnp.mean(x * x, axis=-1, keepdims=True) + EPS) * g


def _mod_kernel(c_ref, w_ref, b_ref, o_ref):
    cs = _silu(c_ref[...])
    o_ref[0] = jnp.dot(cs, w_ref[0], preferred_element_type=F32,
                       precision=lax.Precision.HIGHEST) + b_ref[0]


def _modulation(cvec, w_mod, b_mod):
    depth, d, n = w_mod.shape
    rows = cvec.shape[0]
    bn = 1536
    return pl.pallas_call(
        _mod_kernel,
        grid=(depth, n // bn),
        in_specs=[
            pl.BlockSpec((rows, d), lambda l, j: (0, 0)),
            pl.BlockSpec((1, d, bn), lambda l, j: (l, 0, j)),
            pl.BlockSpec((1, 1, bn), lambda l, j: (l, 0, j)),
        ],
        out_specs=pl.BlockSpec((1, rows, bn), lambda l, j: (l, 0, j)),
        out_shape=jax.ShapeDtypeStruct((depth, rows, n), F32),
        compiler_params=_cparams(("parallel", "parallel")),
        name="modulation",
    )(cvec, w_mod, b_mod.reshape(depth, 1, n))


def _pre_kernel(x_ref, mod_ref, g1_ref, wqkv_ref, ones_ref, grow_ref, gsrow_ref,
                cos_ref, sin_ref, q_out, kt_out, vab_out, vc_out, *, d):
    x = x_ref[0]
    mod = mod_ref[0]
    tn = x.shape[0]
    a = (_rms_rows(x, g1_ref[...]) * (1.0 + mod[:, d:2 * d]) + mod[:, 0:d]).astype(BF16)
    p = _dot(a, wqkv_ref[...])
    pa = jnp.concatenate([p[:, 0:QA_W], p[:, Q_W:Q_W + KA_W]], axis=1)
    sq = pa * pa
    hi = sq.astype(BF16)
    lo = (sq - hi.astype(F32)).astype(BF16)
    ssq = _dot(hi, ones_ref[...]) + _dot(lo, ones_ref[...])
    rinv = lax.rsqrt(ssq * (1.0 / HEAD_DIM) + EPS)
    cos = cos_ref[...]
    sin = sin_ref[...]
    lane = lax.broadcasted_iota(jnp.int32, (tn, 128), 1)
    low_half = (lane % (2 * ROPE_FREQS)) < ROPE_FREQS
    first_head = lane < HEAD_DIM
    for j in range(R_W // 128):
        sl = slice(j * 128, (j + 1) * 128)
        pj = p[:, sl]
        ps = jnp.where(low_half, pltpu.roll(pj, 128 - ROPE_FREQS, 1), pltpu.roll(pj, ROPE_FREQS, 1))
        o = pj * (grow_ref[:, sl] * cos) + ps * (gsrow_ref[:, sl] * sin)
        if (j + 1) * 128 <= QA_W:
            o = o * rinv[:, sl]
        elif j * 128 == Q_W:
            o = o * rinv[:, QA_W:QA_W + KA_W]
        if j * 128 < Q_W:
            q_out[0, :, sl] = o.astype(BF16)
        else:
            ot = jnp.transpose(o).astype(BF16)
            kh = (j * 128 - Q_W) // HEAD_DIM
            kt_out[0, kh] = ot[0:HEAD_DIM]
            kt_out[0, kh + 1] = ot[HEAD_DIM:2 * HEAD_DIM]
    ones_at_64 = (lane == HEAD_DIM).astype(F32)
    for j in range(2):
        vj = p[:, R_W + j * 128:R_W + (j + 1) * 128]
        vab_out[0, 2 * j] = jnp.where(first_head, vj, ones_at_64).astype(BF16)
        vab_out[0, 2 * j + 1] = jnp.where(first_head, pltpu.roll(vj, HEAD_DIM, 1), ones_at_64).astype(BF16)
    ones_at_0 = (lane == 0).astype(BF16)
    for j in range(C_HEADS):
        vj = p[:, R_W + (2 + j) * 128:R_W + (3 + j) * 128]
        vc_out[0, j] = jnp.concatenate([vj.astype(BF16), ones_at_0], axis=1)


def _pre_attention(x, mod, g1, wqkv, ones_blk, grow, gsrow, cos_t, sin_t, n_ctx_tiles):
    b, n, d = x.shape
    tn = TOK_TILE
    nb = b

    def mod_idx(bi, t):
        return (jnp.where(t < n_ctx_tiles, nb, bi), 0, 0)

    return pl.pallas_call(
        functools.partial(_pre_kernel, d=d),
        grid=(b, n // tn),
        in_specs=[
            pl.BlockSpec((1, tn, d), lambda bi, t: (bi, t, 0)),
            pl.BlockSpec((1, 1, N_MOD * d), mod_idx),
            pl.BlockSpec((1, d), lambda bi, t: (0, 0)),
            pl.BlockSpec((d, R_W + V_W), lambda bi, t: (0, 0)),
            pl.BlockSpec((R_A, R_A), lambda bi, t: (0, 0)),
            pl.BlockSpec((1, R_W), lambda bi, t: (0, 0)),
            pl.BlockSpec((1, R_W), lambda bi, t: (0, 0)),
            pl.BlockSpec((tn, 128), lambda bi, t: (t, 0)),
            pl.BlockSpec((tn, 128), lambda bi, t: (t, 0)),
        ],
        out_specs=[
            pl.BlockSpec((1, tn, Q_W), lambda bi, t: (bi, t, 0)),
            pl.BlockSpec((1, K_HEADS, HEAD_DIM, tn), lambda bi, t: (bi, 0, 0, t)),
            pl.BlockSpec((1, A_KV + B_KV, tn, 128), lambda bi, t: (bi, 0, t, 0)),
            pl.BlockSpec((1, C_HEADS, tn, 256), lambda bi, t: (bi, 0, t, 0)),
        ],
        out_shape=[
            jax.ShapeDtypeStruct((b, n, Q_W), BF16),
            jax.ShapeDtypeStruct((b, K_HEADS, HEAD_DIM, n), BF16),
            jax.ShapeDtypeStruct((b, A_KV + B_KV, n, 128), BF16),
            jax.ShapeDtypeStruct((b, C_HEADS, n, 256), BF16),
        ],
        compiler_params=_cparams(("parallel", "parallel")),
        name="pre_attention",
    )(x, mod, g1, wqkv, ones_blk, grow, gsrow, cos_t, sin_t)


def _attn_kernel(sc_ref, gsub_ref, q_ref, kt_ref, v_ref, o_ref, m_sc, acc_sc, *, heads, hpv,
                 hpk, mode, n_ctx, n_tok, tq, ck, dv):
    u = pl.program_id(1)
    qi = pl.program_id(2)
    is_lat = qi >= n_ctx // tq
    dvx = acc_sc.shape[-1]

    for g in range(heads):
        if mode == "window":
            m_sc[g] = jnp.full((tq, 1), sc_ref[u * heads + g], F32)
            lane = lax.broadcasted_iota(jnp.int32, (tq, dvx), 1)
            acc_sc[g] = jnp.where(lane == dv, 1.0, 0.0).astype(F32)
        else:
            m_sc[g] = jnp.full((tq, 1), NEG, F32)
            acc_sc[g] = jnp.zeros((tq, dvx), F32)

    def probs(g, s, mask):
        if mask is not None:
            s = jnp.where(mask, s, NEG)
        m = m_sc[g]
        m_new = jnp.maximum(m, jnp.max(s, axis=-1, keepdims=True))
        m_sc[g] = m_new
        return jnp.exp2(s - m_new).astype(BF16), jnp.exp2(m - m_new)

    def q_of(g):
        return q_ref[0, :, g * HEAD_DIM:(g + 1) * HEAD_DIM]

    def update_all(kt_of, v_of, mask=None):
        s_next = _dot(q_of(0), kt_of(0))
        for g in range(heads):
            s = s_next
            if g + 1 < heads:
                s_next = _dot(q_of(g + 1), kt_of(g + 1))
            p, alpha = probs(g, s, mask)
            acc_sc[g] = alpha * acc_sc[g] + _dot(p, v_of(g))

    def step(k0, size):
        update_all(lambda g: kt_ref[0, g // hpk, :, pl.ds(k0, size)],
                   lambda g: v_ref[0, g // hpv, pl.ds(k0, size), :])

    if mode == "window":
        span = tq + 2 * WINDOW
        start = pl.multiple_of(jnp.clip(qi * tq - WINDOW, 0, n_tok - span), 128)
        qpos = qi * tq + lax.broadcasted_iota(jnp.int32, (tq, n_ctx + span), 0)
        col = lax.broadcasted_iota(jnp.int32, (tq, n_ctx + span), 1)
        kpos = start + col - n_ctx
        mask = (col < n_ctx) | (is_lat & (kpos >= n_ctx) & (jnp.abs(kpos - qpos) <= WINDOW))

        def v_of(g):
            gv = g // hpv
            return jnp.concatenate([v_ref[0, gv, 0:n_ctx, :], v_ref[0, gv, pl.ds(start, span), :]], axis=0)

        def kt_of(g):
            gk = g // hpk
            return jnp.concatenate([kt_ref[0, gk, :, 0:n_ctx], kt_ref[0, gk, :, pl.ds(start, span)]], axis=1)

        update_all(kt_of, v_of, mask)
    else:
        @pl.when(jnp.logical_not(is_lat))
        def _():
            step(0, n_ctx)

        def body(ci, carry):
            step(pl.multiple_of(ci * ck, 128), ck)
            return carry

        lax.fori_loop(0, jnp.where(is_lat, n_tok // ck, 0), body, 0)

    def result(g):
        acc = acc_sc[g]
        return acc[:, 0:128] / acc[:, dv:dv + 1]

    if mode == "diff":
        lam = sc_ref[heads]
        post = sc_ref[heads + 1]
        for j in range(heads // 2):
            y = result(2 * j) - lam * result(2 * j + 1)
            o_ref[0, :, j * dv:(j + 1) * dv] = (_rms_rows(y, gsub_ref[...]) * post).astype(o_ref.dtype)
    else:
        lane = lax.broadcasted_iota(jnp.int32, (tq, 128), 1)
        for j in range(heads // 2):
            pair = jnp.where(lane < dv, result(2 * j), pltpu.roll(result(2 * j + 1), dv, 1))
            o_ref[0, :, j * 128:(j + 1) * 128] = pair.astype(o_ref.dtype)


def _attention(scalars, gsub, q, kt, vx, *, heads, hpv, hpk, mode, n_ctx, units, q_unit0, k_unit0,
               v_unit0):
    b, n, _ = q.shape
    hd = HEAD_DIM
    dvx = vx.shape[-1]
    dv = gsub.shape[-1]
    assert dv < dvx
    tq = ATTN_TQ
    gk = heads // hpk
    vh = heads // hpv
    out_w = heads * hd if mode != "diff" else (heads // 2) * dv
    ck = max(c for c in range(128, MAX_KEY_CHUNK[mode] + 1, 128) if n % c == 0)
    kern = functools.partial(_attn_kernel, heads=heads, hpv=hpv, hpk=hpk, mode=mode,
                             n_ctx=n_ctx, n_tok=n, tq=tq, ck=ck, dv=dv)
    return pl.pallas_call(
        kern,
        grid=(b, units, n // tq),
        in_specs=[
            pl.BlockSpec(memory_space=pltpu.SMEM),
            pl.BlockSpec((1, dv), lambda bi, u, t: (0, 0)),
            pl.BlockSpec((1, tq, heads * hd), lambda bi, u, t: (bi, t, q_unit0 + u)),
            pl.BlockSpec((1, gk, hd, n), lambda bi, u, t: (bi, k_unit0 + u, 0, 0)),
            pl.BlockSpec((1, vh, n, dvx), lambda bi, u, t: (bi, v_unit0 + u, 0, 0)),
        ],
        out_specs=pl.BlockSpec((1, tq, out_w), lambda bi, u, t: (bi, t, u)),
        out_shape=jax.ShapeDtypeStruct((b, n, units * out_w), BF16),
        scratch_shapes=[
            pltpu.VMEM((heads, tq, 1), F32),
            pltpu.VMEM((heads, tq, dvx), F32),
        ],
        compiler_params=_cparams(("parallel", "parallel", "arbitrary")),
        name="attn_" + mode,
    )(scalars, gsub, q, kt, vx)


def _merge_kernel(x_ref, mod_ref, g1_ref, ya_ref, yb_ref, yc_ref, wg_ref, bg_ref, wa_ref, wb_ref,
                  wc_ref, wo_ref, o_ref, *, d):
    x = x_ref[0]
    mod = mod_ref[0]
    a = (_rms_rows(x, g1_ref[...]) * (1.0 + mod[:, d:2 * d]) + mod[:, 0:d]).astype(BF16)
    gate = _sigmoid(_dot(a, wg_ref[...]) + bg_ref[...])
    m = (gate[:, 0:d] * _dot(ya_ref[0], wa_ref[...])
         + gate[:, d:2 * d] * _dot(yb_ref[0], wb_ref[...])
         + gate[:, 2 * d:3 * d] * _dot(yc_ref[0], wc_ref[...]))
    mix = _dot(m.astype(BF16), wo_ref[...])
    o_ref[0] = x + mod[:, 2 * d:3 * d] * mix


def _merge(x, mod, g1, ya, yb, yc, wg, bg, wa, wb, wc, wo, n_ctx_tiles):
    b, n, d = x.shape
    tn = TOK_TILE
    nb = b
    yw = ya.shape[-1]

    def mod_idx(bi, t):
        return (jnp.where(t < n_ctx_tiles, nb, bi), 0, 0)

    tok = lambda bi, t: (bi, t, 0)
    const = lambda bi, t: (0, 0)
    return pl.pallas_call(
        functools.partial(_merge_kernel, d=d),
        grid=(b, n // tn),
        in_specs=[
            pl.BlockSpec((1, tn, d), tok),
            pl.BlockSpec((1, 1, N_MOD * d), mod_idx),
            pl.BlockSpec((1, d), const),
            pl.BlockSpec((1, tn, yw), tok),
            pl.BlockSpec((1, tn, yw), tok),
            pl.BlockSpec((1, tn, yw), tok),
            pl.BlockSpec((d, 3 * d), const),
            pl.BlockSpec((1, 3 * d), const),
            pl.BlockSpec((yw, d), const),
            pl.BlockSpec((yw, d), const),
            pl.BlockSpec((yw, d), const),
            pl.BlockSpec((d, d), const),
        ],
        out_specs=pl.BlockSpec((1, tn, d), tok),
        out_shape=jax.ShapeDtypeStruct((b, n, d), F32),
        compiler_params=_cparams(("parallel", "parallel")),
        name="merge",
    )(x, mod, g1, ya, yb, yc, wg, bg, wa, wb, wc, wo)


def _first_index(hit, idx, big):
    return jnp.min(jnp.where(hit, idx, big), axis=0, keepdims=True)


def _route_kernel(x_ref, mod_ref, g2_ref, wrt_ref, eb_ref, ws1_ref, ws3_ref, ws2_ref,
                  f_ref, idx_ref, w_ref, xs_ref, cnt_ref, *, d):
    x = x_ref[0]
    mod = mod_ref[0]
    f = _rms_rows(x, g2_ref[...]) * (1.0 + mod[:, 4 * d:5 * d]) + mod[:, 3 * d:4 * d]
    tn = f.shape[0]
    for s in range(d // 128):
        f_ref[0, pl.ds(s, tn, stride=d // 128), :] = f[:, s * 128:(s + 1) * 128]
    logits = lax.dot_general(wrt_ref[...], f, (((1,), (1,)), ((), ())),
                             preferred_element_type=F32, precision=lax.Precision.HIGHEST)
    scores = _sigmoid(logits)
    choice = scores + eb_ref[...]
    eidx = lax.broadcasted_iota(jnp.int32, (N_EXPERTS, tn), 0)
    lidx = lax.broadcasted_iota(jnp.int32, (GROUP_SIZE, tn), 0)
    gscore = []
    for g in range(N_GROUPS):
        cg = choice[g * GROUP_SIZE:(g + 1) * GROUP_SIZE, :]
        m1 = jnp.max(cg, axis=0, keepdims=True)
        first = _first_index(cg == m1, lidx, GROUP_SIZE)
        m2 = jnp.max(jnp.where(lidx == first, NEG, cg), axis=0, keepdims=True)
        gscore.append(m1 + m2)
    gs = jnp.concatenate(gscore, axis=0)
    gidx = lax.broadcasted_iota(jnp.int32, (N_GROUPS, tn), 0)
    gsel = jnp.zeros((N_GROUPS, tn), jnp.bool_)
    for _ in range(TOPK_GROUPS):
        gm = jnp.max(gs, axis=0, keepdims=True)
        first = _first_index(gs == gm, gidx, N_GROUPS)
        hit = gidx == first
        gsel = gsel | hit
        gs = jnp.where(hit, NEG, gs)
    gself = gsel.astype(F32)
    emask = jnp.concatenate(
        [jnp.broadcast_to(gself[g:g + 1, :], (GROUP_SIZE, tn)) for g in range(N_GROUPS)], axis=0)
    cur = jnp.where(emask > 0.5, choice, NEG)
    ids, ws = [], []
    chosen = jnp.zeros((N_EXPERTS, tn), F32)
    for _ in range(TOP_K):
        m = jnp.max(cur, axis=0, keepdims=True)
        first = _first_index(cur == m, eidx, N_EXPERTS)
        hit = eidx == first
        ids.append(first)
        ws.append(jnp.sum(jnp.where(hit, scores, 0.0), axis=0, keepdims=True))
        cur = jnp.where(hit, NEG, cur)
        chosen = chosen + hit.astype(F32)
    wsel = jnp.concatenate(ws, axis=0)
    idx_ref[0] = jnp.concatenate(ids, axis=0)
    cnt_ref[0, 0] = jnp.sum(chosen, axis=1, keepdims=True).astype(jnp.int32)
    w_ref[0] = wsel / jnp.sum(wsel, axis=0, keepdims=True) * ROUTE_SCALE
    fb = f.astype(BF16)
    h = _silu(_dot(fb, ws1_ref[...])) * _dot(fb, ws3_ref[...])
    xs_ref[0] = x + mod[:, 5 * d:6 * d] * _dot(h.astype(BF16), ws2_ref[...])


def _route(x, mod, g2, wrt, eb, ws1, ws3, ws2, n_ctx_tiles):
    b, n, d = x.shape
    tn = TOK_TILE
    nb = b
    ds = ws1.shape[-1]

    def mod_idx(bi, t):
        return (jnp.where(t < n_ctx_tiles, nb, bi), 0, 0)

    tok = lambda bi, t: (bi, t, 0)
    lane_tok = lambda bi, t: (bi, 0, t)
    const = lambda bi, t: (0, 0)
    return pl.pallas_call(
        functools.partial(_route_kernel, d=d),
        grid=(b, n // tn),
        in_specs=[
            pl.BlockSpec((1, tn, d), tok),
            pl.BlockSpec((1, 1, N_MOD * d), mod_idx),
            pl.BlockSpec((1, d), const),
            pl.BlockSpec((N_EXPERTS, d), const),
            pl.BlockSpec((N_EXPERTS, 1), const),
            pl.BlockSpec((d, ds), const),
            pl.BlockSpec((d, ds), const),
            pl.BlockSpec((ds, d), const),
        ],
        out_specs=[
            pl.BlockSpec((1, tn * (d // 128), 128), tok),
            pl.BlockSpec((1, TOP_K, tn), lane_tok),
            pl.BlockSpec((1, TOP_K, tn), lane_tok),
            pl.BlockSpec((1, tn, d), tok),
            pl.BlockSpec((1, 1, N_EXPERTS, 1), lambda bi, t: (bi, t, 0, 0)),
        ],
        out_shape=[
            jax.ShapeDtypeStruct((b, n * (d // 128), 128), F32),
            jax.ShapeDtypeStruct((b, TOP_K, n), jnp.int32),
            jax.ShapeDtypeStruct((b, TOP_K, n), F32),
            jax.ShapeDtypeStruct((b, n, d), F32),
            jax.ShapeDtypeStruct((b, n // tn, N_EXPERTS, 1), jnp.int32),
        ],
        compiler_params=_cparams(("parallel", "parallel")),
        name="route_shared",
    )(x, mod, g2, wrt, eb, ws1, ws3, ws2)


def _expert_kernel(iexp_ref, iblk_ref, iflag_ref, nitems_ref, tok_ref, tok_next_ref, tok_ahead_ref,
                   dst_prev_ref, dst_ref, erow_ref, f_hbm, w1_ref, w3_ref, w2_ref, y_hbm, xbuf, ybuf,
                   wb1, wb3, wb2, gsem, ssem):
    i = pl.program_id(0)
    nitems = nitems_ref[0]
    blk = iblk_ref[i]
    expert = iexp_ref[i]
    first = (iflag_ref[i] & 1) != 0
    last = (iflag_ref[i] & 2) != 0
    ring = xbuf.shape[0]
    slot = blk % ring
    slot_prev = (blk + ring - 1) % ring
    slot_prev2 = (blk + ring - 2) % ring
    slot_ahead = (blk + 2) % ring
    rows = EXPERT_BLOCK
    parts = xbuf.shape[1] // rows
    yparts = ybuf.shape[1] // rows
    n_blocks = y_hbm.shape[0] // (rows * yparts)

    def gather(idx_ref, s):
        for j in range(rows):
            src = pl.multiple_of(idx_ref[0, 0, j], parts)
            pltpu.make_async_copy(f_hbm.at[pl.ds(src, parts)], xbuf.at[s, pl.ds(j * parts, parts)],
                                  gsem.at[s]).start(priority=j % 2)

    def gather_wait(s):
        pltpu.make_async_copy(f_hbm.at[pl.ds(0, rows * parts)], xbuf.at[s], gsem.at[s]).wait()

    def scatter(idx_ref, s):
        for j in range(rows):
            dst = pl.multiple_of(idx_ref[0, 0, j], yparts)
            pltpu.make_async_copy(ybuf.at[s, pl.ds(j * yparts, yparts)], y_hbm.at[pl.ds(dst, yparts)],
                                  ssem.at[s]).start(priority=j % 2)

    def scatter_wait(s):
        pltpu.make_async_copy(ybuf.at[s], y_hbm.at[pl.ds(0, rows * yparts)], ssem.at[s]).wait()

    def expert_rows(s, after_up=None, after_down=None):
        xb = jnp.concatenate([xbuf[s, pl.ds(p, rows, stride=parts), :] for p in range(parts)],
                             axis=1).astype(BF16)
        h1 = _dot(xb, wb1[...])
        h3 = _dot(xb, wb3[...])
        if after_up is not None:
            after_up()
        y = _dot((_silu(h1) * h3).astype(BF16), wb2[...])
        if after_down is not None:
            after_down()
        return y

    def bf16_bits(v):
        return lax.bitcast_convert_type(v.astype(BF16).astype(F32), jnp.uint32)

    def store_rows(s, y, own_rows=None):
        for p in range(yparts):
            lo = bf16_bits(y[:, p * 128:(p + 1) * 128]) >> 16
            hi = bf16_bits(y[:, (yparts + p) * 128:(yparts + p + 1) * 128]) & jnp.uint32(0xFFFF0000)
            part = hi | lo
            if own_rows is not None:
                part = jnp.where(own_rows, part, ybuf[s, pl.ds(p, rows, stride=yparts), :])
            ybuf[s, pl.ds(p, rows, stride=yparts), :] = part

    valid = i < nitems

    @pl.when(valid & ((iflag_ref[i] & 4) != 0))
    def _():
        wb1[...] = w1_ref[0, 0].astype(BF16)
        wb3[...] = w3_ref[0, 0].astype(BF16)
        wb2[...] = w2_ref[0, 0].astype(BF16)

    fast = valid & first & last & (blk >= 3) & (blk <= n_blocks - 3)

    @pl.when(fast)
    def _():
        gather_wait(slot)
        scatter_wait(slot)
        y = expert_rows(slot, after_up=lambda: gather(tok_ahead_ref, slot_ahead),
                        after_down=lambda: scatter(dst_prev_ref, slot_prev))
        store_rows(slot, y)

    @pl.when(valid & jnp.logical_not(fast))
    def _():
        @pl.when(first)
        def _():
            @pl.when(blk == 0)
            def _():
                gather(tok_ref, 0)
                if n_blocks > 1:
                    gather(tok_next_ref, 1)

            @pl.when(blk + 2 < n_blocks)
            def _():
                gather(tok_ahead_ref, slot_ahead)

            @pl.when(blk >= 1)
            def _():
                scatter(dst_prev_ref, slot_prev)

            gather_wait(slot)

            @pl.when(blk >= 3)
            def _():
                scatter_wait(slot)

        y = expert_rows(slot)

        @pl.when(first)
        def _():
            store_rows(slot, y)

        @pl.when(jnp.logical_not(first))
        def _():
            ours = (erow_ref[0] == expert) & (lax.broadcasted_iota(jnp.int32, (rows, rows), 0)
                                             == lax.broadcasted_iota(jnp.int32, (rows, rows), 1))
            store_rows(slot, y, own_rows=jnp.sum(ours.astype(F32), axis=1, keepdims=True) > 0.5)

        @pl.when(last & (blk == n_blocks - 1))
        def _():
            scatter(dst_ref, slot)

            @pl.when(blk >= 2)
            def _():
                scatter_wait(slot_prev2)

            @pl.when(blk >= 1)
            def _():
                scatter_wait(slot_prev)

            scatter_wait(slot)


def _experts(iexp, iblk, iflag, nitems, row_tok, row_dst, erow, f_rows, w1, w3, w2, layer):
    n_items = iexp.shape[0]
    n_blocks = row_tok.shape[0]
    d, de = w1.shape[2], w1.shape[3]
    parts = d // 128
    yparts = parts // 2
    rows = EXPERT_BLOCK
    last = n_blocks - 1
    cur = lambda i, ie, ib, fl, nt: (ib[i], 0, 0)
    nxt = lambda i, ie, ib, fl, nt: (jnp.minimum(ib[i] + 1, last), 0, 0)
    ahd = lambda i, ie, ib, fl, nt: (jnp.minimum(ib[i] + 2, last), 0, 0)
    prv = lambda i, ie, ib, fl, nt: (jnp.maximum(ib[i] - 1, 0), 0, 0)
    wsel = lambda i, ie, ib, fl, nt: (layer, ie[i], 0, 0)
    grid_spec = pltpu.PrefetchScalarGridSpec(
        num_scalar_prefetch=4,
        grid=(n_items,),
        in_specs=[
            pl.BlockSpec((1, 1, rows), cur, memory_space=pltpu.SMEM),
            pl.BlockSpec((1, 1, rows), nxt, memory_space=pltpu.SMEM),
            pl.BlockSpec((1, 1, rows), ahd, memory_space=pltpu.SMEM),
            pl.BlockSpec((1, 1, rows), prv, memory_space=pltpu.SMEM),
            pl.BlockSpec((1, 1, rows), cur, memory_space=pltpu.SMEM),
            pl.BlockSpec((1, 1, rows), cur),
            pl.BlockSpec(memory_space=pl.ANY),
            pl.BlockSpec((1, 1, d, de), wsel),
            pl.BlockSpec((1, 1, d, de), wsel),
            pl.BlockSpec((1, 1, de, d), wsel),
        ],
        out_specs=pl.BlockSpec(memory_space=pl.ANY),
        scratch_shapes=[
            pltpu.VMEM((EXPERT_RING, rows * parts, 128), F32),
            pltpu.VMEM((EXPERT_RING, rows * yparts, 128), jnp.uint32),
            pltpu.VMEM((d, de), BF16),
            pltpu.VMEM((d, de), BF16),
            pltpu.VMEM((de, d), BF16),
            pltpu.SemaphoreType.DMA((EXPERT_RING,)),
            pltpu.SemaphoreType.DMA((EXPERT_RING,)),
        ],
    )
    return pl.pallas_call(
        _expert_kernel,
        grid_spec=grid_spec,
        out_shape=jax.ShapeDtypeStruct((n_blocks * rows * yparts, 128), jnp.uint32),
        compiler_params=_cparams(("arbitrary",)),
        name="experts",
    )(iexp, iblk, iflag, nitems, row_tok, row_tok, row_tok, row_dst, row_dst, erow, f_rows, w1, w3, w2)


ASSIGN_BITS = 20


def _routing_tables(idx, counts, n_tokens, parts):
    n_assign = n_tokens * TOP_K
    blk = EXPERT_BLOCK
    assert n_assign % blk == 0 and n_assign <= (1 << ASSIGN_BITS)
    n_blocks = n_assign // blk
    n_items = n_blocks + N_EXPERTS
    flat_e = idx.reshape(-1).astype(jnp.int32)
    key = jnp.sort((flat_e << ASSIGN_BITS) | jnp.arange(n_assign, dtype=jnp.int32), stable=False)
    e_sorted = key >> ASSIGN_BITS
    order = key & ((1 << ASSIGN_BITS) - 1)
    tok = order // TOP_K
    row_dst = ((order % TOP_K) * n_tokens + tok) * (parts // 2)
    tok = tok * parts

    experts = jnp.arange(N_EXPERTS, dtype=jnp.int32)
    ends = jnp.cumsum(counts)
    starts = ends - counts
    first_blk = starts // blk
    n_be = jnp.where(ends > starts, (ends - 1) // blk - first_blk + 1, 0)
    item_end = jnp.cumsum(n_be)
    item_off = item_end - n_be
    total = item_end[-1]
    i = jnp.arange(n_items, dtype=jnp.int32)
    iexp = jnp.minimum(jnp.sum((item_end[None, :] <= i[:, None]).astype(jnp.int32), axis=1), N_EXPERTS - 1)
    pick = (iexp[:, None] == experts[None, :]).astype(jnp.int32)
    base = jnp.sum(pick * (first_blk - item_off)[None, :], axis=1)
    iblk = jnp.where(i < total, base + i, n_blocks - 1).astype(jnp.int32)
    prev_blk = jnp.concatenate([jnp.full((1,), -1, jnp.int32), iblk[:-1]])
    next_blk = jnp.concatenate([iblk[1:], jnp.full((1,), -1, jnp.int32)])
    is_first = iblk != prev_blk
    is_last = (iblk != next_blk) | (i == total - 1)
    new_expert = iexp != jnp.concatenate([jnp.full((1,), -1, jnp.int32), iexp[:-1]])
    iflag = is_first.astype(jnp.int32) + 2 * is_last.astype(jnp.int32) + 4 * new_expert.astype(jnp.int32)
    return (iexp, iblk, iflag, total.astype(jnp.int32).reshape(1), tok.reshape(n_blocks, 1, blk),
            row_dst.reshape(n_blocks, 1, blk), e_sorted.reshape(n_blocks, 1, blk))


def _combine_kernel(xs_ref, mod_ref, w_ref, *refs, d):
    y_refs, o_ref = refs[:TOP_K], refs[TOP_K]
    w = w_ref[0]
    tn = w.shape[0]
    yparts = d // 256
    wk = [jnp.broadcast_to(w[:, k:k + 1], (tn, 128)) for k in range(TOP_K)]
    for p in range(yparts):
        lo = jnp.zeros((tn, 128), F32)
        hi = jnp.zeros((tn, 128), F32)
        for k in range(TOP_K):
            word = y_refs[k][pl.ds(p, tn, stride=yparts), :]
            lo = lo + lax.bitcast_convert_type(word << 16, F32) * wk[k]
            hi = hi + lax.bitcast_convert_type(word & jnp.uint32(0xFFFF0000), F32) * wk[k]
        for q, tot in ((p, lo), (yparts + p, hi)):
            sl = slice(q * 128, (q + 1) * 128)
            o_ref[0, :, sl] = xs_ref[0, :, sl] + mod_ref[0][:, 5 * d + q * 128:5 * d + (q + 1) * 128] * tot


def _combine(xs, mod, w, y, n_ctx_tiles):
    b, n, d = xs.shape
    tn = TOK_TILE
    nb = b
    tiles = n // tn
    per_slot = b * tiles

    def mod_idx(bi, t):
        return (jnp.where(t < n_ctx_tiles, nb, bi), 0, 0)

    tok = lambda bi, t: (bi, t, 0)
    y_specs = [pl.BlockSpec((tn * (d // 256), 128),
                            functools.partial(lambda bi, t, k: (k * per_slot + bi * tiles + t, 0), k=k))
               for k in range(TOP_K)]
    return pl.pallas_call(
        functools.partial(_combine_kernel, d=d),
        grid=(b, tiles),
        in_specs=[pl.BlockSpec((1, tn, d), tok), pl.BlockSpec((1, 1, N_MOD * d), mod_idx),
                  pl.BlockSpec((1, tn, TOP_K), tok)] + y_specs,
        out_specs=pl.BlockSpec((1, tn, d), tok),
        out_shape=jax.ShapeDtypeStruct((b, n, d), F32),
        compiler_params=_cparams(("parallel", "parallel")),
        name="combine",
    )(xs, mod, w, *([y] * TOP_K))


def _final_kernel(x_ref, g_ref, o_ref):
    o_ref[0] = _rms_rows(x_ref[0], g_ref[...])


def _final_norm(x, g, n_ctx_tiles):
    b, n, d = x.shape
    tn = TOK_TILE
    n_lat_tiles = n // tn - n_ctx_tiles
    return pl.pallas_call(
        _final_kernel,
        grid=(b, n_lat_tiles),
        in_specs=[pl.BlockSpec((1, tn, d), lambda bi, t: (bi, t + n_ctx_tiles, 0)),
                  pl.BlockSpec((1, d), lambda bi, t: (0, 0))],
        out_specs=pl.BlockSpec((1, tn, d), lambda bi, t: (bi, t, 0)),
        out_shape=jax.ShapeDtypeStruct((b, n_lat_tiles * tn, d), F32),
        compiler_params=_cparams(("parallel", "parallel")),
        name="final_norm",
    )(x, g)


def _regroup_qkv_columns(w):
    a_q, a_kv = A_HEADS * HEAD_DIM, A_KV * HEAD_DIM
    b_q, b_kv = B_HEADS * HEAD_DIM, B_KV * HEAD_DIM
    c_qk, c_v = 2 * C_HEADS * HEAD_DIM, C_HEADS * 2 * HEAD_DIM
    off = np.concatenate([[0], np.cumsum((a_q, a_kv, a_kv, b_q, b_kv, b_kv, c_qk, c_qk, c_v))])
    seg = lambda i: w[:, int(off[i]):int(off[i + 1])]
    return jnp.concatenate([seg(0), seg(3), seg(6), seg(1), seg(4), seg(7), seg(2), seg(5), seg(8)], axis=1)


def _rotary_partner(row):
    blocks = row.reshape(-1, 2, ROPE_FREQS)
    return blocks[:, ::-1, :].reshape(-1)


def _rope_tables(n_ctx, n_lat):
    t = jnp.arange(n_lat, dtype=jnp.int32)
    row_pos = (t // GRID_W).astype(F32)
    col_pos = (t % GRID_W).astype(F32)
    inv_freq = jnp.power(ROPE_THETA, -jnp.arange(ROPE_FREQS, dtype=F32) / ROPE_FREQS)
    ang_r = row_pos[:, None] * inv_freq
    ang_c = col_pos[:, None] * inv_freq
    cos64 = jnp.concatenate([jnp.cos(ang_r), jnp.cos(ang_r), jnp.cos(ang_c), jnp.cos(ang_c)], axis=1)
    sin64 = jnp.concatenate([-jnp.sin(ang_r), jnp.sin(ang_r), -jnp.sin(ang_c), jnp.sin(ang_c)], axis=1)
    cos64 = jnp.concatenate([jnp.ones((n_ctx, HEAD_DIM), F32), cos64], axis=0)
    sin64 = jnp.concatenate([jnp.zeros((n_ctx, HEAD_DIM), F32), sin64], axis=0)
    return jnp.tile(cos64, (1, 2)), jnp.tile(sin64, (1, 2))


def kernel(x, c, ctx, c_ctx, w_mod, b_mod, g_norm1, w_qkv, g_qnorm_a, g_knorm_a, sink_b, lam_q1, lam_k1, lam_q2, lam_k2, g_subln_c, w_br_a, w_br_b, w_br_c, w_gate, b_gate, w_out, g_norm2, w_router, e_bias, w1, w3, w2, ws1, ws3, ws2, g_final):
    bsz, n_lat, d = x.shape
    n_ctx = ctx.shape[1]
    depth = w_mod.shape[0]
    n_tok = n_ctx + n_lat
    assert n_ctx % TOK_TILE == 0 and n_lat % KEY_CHUNK == 0 and n_ctx % 128 == 0
    n_ctx_tiles = n_ctx // TOK_TILE
    n_all = bsz * n_tok

    rows = -(-(bsz + 1) // 8) * 8
    cvec = jnp.concatenate([c, c_ctx[None, :], jnp.zeros((rows - bsz - 1, d), F32)], axis=0)
    mod_all = _modulation(cvec, w_mod, b_mod)

    cos_t, sin_t = _rope_tables(n_ctx, n_lat)
    head_of = np.arange(R_A) // HEAD_DIM
    ones_blk = jnp.asarray((head_of[:, None] == head_of[None, :]).astype(np.float32), BF16)
    scale = HEAD_DIM ** -0.5 * LOG2E
    unit = jnp.ones((HEAD_DIM,), F32)

    xs = jnp.concatenate([ctx, x], axis=1)
    for l in range(depth):
        lam_init = 0.8 - 0.6 * math.exp(-0.3 * l)
        lam = (jnp.exp(jnp.dot(lam_q1[l], lam_k1[l])) - jnp.exp(jnp.dot(lam_q2[l], lam_k2[l]))).astype(F32) + lam_init
        mod = mod_all[l].reshape(rows, 1, N_MOD * d)
        g1 = g_norm1[l].reshape(1, d)

        wqkv = _regroup_qkv_columns(w_qkv[l]).astype(BF16)
        grow = jnp.concatenate([jnp.tile(g_qnorm_a[l] * scale, A_HEADS), jnp.tile(unit * scale, B_HEADS),
                                jnp.tile(unit * scale, 2 * C_HEADS), jnp.tile(g_knorm_a[l], A_KV),
                                jnp.tile(unit, B_KV), jnp.tile(unit, 2 * C_HEADS)])
        gsrow = _rotary_partner(grow)
        q_all, kt_all, vx_ab, vx_c = _pre_attention(xs, mod, g1, wqkv, ones_blk, grow.reshape(1, R_W),
                                                    gsrow.reshape(1, R_W), cos_t, sin_t, n_ctx_tiles)

        scal = jnp.concatenate([sink_b[l].astype(F32) * LOG2E, lam.reshape(1), jnp.full((1,), 1.0 - lam_init, F32)])
        scal_c = jnp.concatenate([jnp.zeros((DIFF_HEADS_PER_STEP,), F32), lam.reshape(1),
                                  jnp.full((1,), 1.0 - lam_init, F32)])
        gsub = g_subln_c[l].reshape(1, 2 * HEAD_DIM)
        g64 = jnp.ones((1, HEAD_DIM), F32)
        ga, gb = A_HEADS // A_KV, B_HEADS // B_KV
        ya = _attention(scal, g64, q_all, kt_all, vx_ab, heads=ga, hpv=ga, hpk=ga, mode="global",
                        n_ctx=n_ctx, units=A_KV, q_unit0=0, k_unit0=0, v_unit0=0)
        yb = _attention(scal, g64, q_all, kt_all, vx_ab, heads=B_HEADS, hpv=gb, hpk=gb, mode="window",
                        n_ctx=n_ctx, units=1, q_unit0=QA_W // QB_W, k_unit0=A_KV // B_KV, v_unit0=A_KV // B_KV)
        yc = _attention(scal_c, gsub, q_all, kt_all, vx_c, heads=DIFF_HEADS_PER_STEP, hpv=2, hpk=1,
                        mode="diff", n_ctx=n_ctx, units=2 * C_HEADS // DIFF_HEADS_PER_STEP,
                        q_unit0=(QA_W + QB_W) // 256, k_unit0=(A_KV + B_KV) // DIFF_HEADS_PER_STEP, v_unit0=0)

        x1 = _merge(xs, mod, g1, ya, yb, yc,
                    w_gate[l].astype(BF16), b_gate[l].reshape(1, -1), w_br_a[l].astype(BF16),
                    w_br_b[l].astype(BF16), w_br_c[l].astype(BF16), w_out[l].astype(BF16), n_ctx_tiles)

        f, idx_t, w_t, x_sh, tile_counts = _route(x1, mod, g_norm2[l].reshape(1, d), w_router[l].T,
                                     e_bias[l].reshape(N_EXPERTS, 1), ws1[l].astype(BF16),
                                     ws3[l].astype(BF16), ws2[l].astype(BF16), n_ctx_tiles)

        idx = idx_t.transpose(0, 2, 1).reshape(n_all, TOP_K)
        counts = jnp.sum(tile_counts, axis=(0, 1)).reshape(N_EXPERTS)
        iexp, iblk, iflag, nitems, row_tok, row_dst, erow = _routing_tables(idx, counts, n_all, d // 128)
        y = _experts(iexp, iblk, iflag, nitems, row_tok, row_dst, erow, f.reshape(n_all * (d // 128), 128),
                     w1, w3, w2, l)
        xs = _combine(x_sh, mod, w_t.transpose(0, 2, 1), y, n_ctx_tiles)

    return _final_norm(xs, g_final.reshape(1, d), n_ctx_tiles)
```
